```python
import math
import jax, jax.numpy as jnp
from jax import lax
import numpy as np

D_MODEL = 1024
BATCH = 8
SEQ = 2048
DEPTH = 4

BRANCH_WIDTH = D_MODEL // 2
N_BRANCH = 3
N_IN_CHUNKS = 6
SSM_GROUP = 16
SSM_GROUPS = BRANCH_WIDTH // SSM_GROUP
SSM_STATE = 64
DT_MIN = 1e-3
DT_MAX = 1e-1
CONF_KERNEL = 31
SCONV_KERNEL = 3
N_GROUPS = 4
EXPERTS_PER_GROUP = 8
N_EXPERTS = N_GROUPS * EXPERTS_PER_GROUP
TOP_K = 2
D_FF_EXPERT = D_MODEL // 2
MOE_BLOCK = 128
NORM_EPS = 1e-6

kernel_name = 'hybrid_s5_conformer_shortconv_hmoe_adaln'


def rms_norm(x, g):
    x32 = x.astype(jnp.float32)
    y = x32 * lax.rsqrt(jnp.mean(x32 * x32, axis=-1, keepdims=True) + NORM_EPS)
    return (y * g.astype(jnp.float32)).astype(x.dtype)


def layer_norm(x, g, b):
    x32 = x.astype(jnp.float32)
    xc = x32 - jnp.mean(x32, axis=-1, keepdims=True)
    y = xc * lax.rsqrt(jnp.mean(xc * xc, axis=-1, keepdims=True) + NORM_EPS)
    return (y * g.astype(jnp.float32) + b.astype(jnp.float32)).astype(x.dtype)


def causal_depthwise_conv(x, w):
    k, ch = w.shape
    xp = jnp.pad(x, ((0, 0), (k - 1, 0), (0, 0)))
    return lax.conv_general_dilated(xp, w[:, None, :].astype(x.dtype), window_strides=(1,),
                                    padding='VALID', dimension_numbers=('NWC', 'WIO', 'NWC'),
                                    feature_group_count=ch)


def _complex_linear_combine(left, right):
    ar1, ai1, br1, bi1 = left
    ar2, ai2, br2, bi2 = right
    ar = ar1 * ar2 - ai1 * ai2
    ai = ar1 * ai2 + ai1 * ar2
    br = ar2 * br1 - ai2 * bi1 + br2
    bi = ar2 * bi1 + ai2 * br1 + bi2
    return (ar, ai, br, bi)


def s5_branch(u, lam_re, lam_im, log_step, b_re, b_im, c_re, c_im, d_skip, w_glu):
    f32 = jnp.float32
    bsz, seq, _ = u.shape
    uf = u.astype(f32).reshape(bsz, seq, SSM_GROUPS, SSM_GROUP)
    lr, li = lam_re.astype(f32), lam_im.astype(f32)
    step = jnp.exp(log_step.astype(f32))[:, None]
    mag = jnp.exp(lr * step)
    ar, ai = mag * jnp.cos(li * step), mag * jnp.sin(li * step)
    den = lr * lr + li * li
    nr, ni = ar - 1.0, ai
    kr = (nr * lr + ni * li) / den
    ki = (ni * lr - nr * li) / den
    br, bi = b_re.astype(f32), b_im.astype(f32)
    bbr = kr[..., None] * br - ki[..., None] * bi
    bbi = kr[..., None] * bi + ki[..., None] * br
    bu_r = jnp.einsum('gph,bsgh->bsgp', bbr, uf)
    bu_i = jnp.einsum('gph,bsgh->bsgp', bbi, uf)
    a_r = jnp.broadcast_to(ar[None, None], (1, seq) + ar.shape)
    a_i = jnp.broadcast_to(ai[None, None], (1, seq) + ai.shape)
    _, _, xr, xi = lax.associative_scan(_complex_linear_combine, (a_r, a_i, bu_r, bu_i), axis=1)
    y = (jnp.einsum('ghp,bsgp->bsgh', c_re.astype(f32), xr)
         - jnp.einsum('ghp,bsgp->bsgh', c_im.astype(f32), xi))
    y = y.reshape(bsz, seq, BRANCH_WIDTH) + d_skip.astype(f32) * uf.reshape(bsz, seq, BRANCH_WIDTH)
    y = jax.nn.gelu(y).astype(u.dtype)
    return y * jax.nn.sigmoid(y @ w_glu)


def conformer_branch(v, g, dw_w, dw_b, ln_g, ln_b):
    z = v * jax.nn.sigmoid(g)
    z = causal_depthwise_conv(z, dw_w) + dw_b
    z = layer_norm(z, ln_g, ln_b)
    return jax.nn.silu(z)


def short_conv_branch(gate_b, gate_c, hv, sconv_w):
    return gate_b * causal_depthwise_conv(gate_c * hv, sconv_w)


def hybrid_mixer(h, w_in, lam_re, lam_im, log_step, ssm_b_re, ssm_b_im, ssm_c_re, ssm_c_im, ssm_d,
                 w_glu, conf_dw_w, conf_dw_b, conf_ln_g, conf_ln_b, sconv_w, w_branch, w_gate, b_gate,
                 w_out):
    bsz, seq, d = h.shape
    u_ssm, conf_v, conf_g, sc_b, sc_c, sc_h = jnp.split(h @ w_in, N_IN_CHUNKS, axis=-1)
    y_ssm = s5_branch(u_ssm, lam_re, lam_im, log_step, ssm_b_re, ssm_b_im, ssm_c_re, ssm_c_im,
                      ssm_d, w_glu)
    y_conf = conformer_branch(conf_v, conf_g, conf_dw_w, conf_dw_b, conf_ln_g, conf_ln_b)
    y_sc = short_conv_branch(sc_b, sc_c, sc_h, sconv_w)
    ys = jnp.stack([y_ssm, y_conf, y_sc], axis=2)
    branches = jnp.einsum('bsnw,nwd->bsnd', ys, w_branch)
    gates = jax.nn.sigmoid((h @ w_gate + b_gate).reshape(bsz, seq, N_BRANCH, d))
    return jnp.einsum('bsnd,bsnd->bsd', gates, branches) @ w_out


def grouped_experts(x, expert_idx, w1, w3, w2):
    t, d = x.shape
    n_assign = t * TOP_K
    flat_e = expert_idx.reshape(n_assign).astype(jnp.int32)
    flat_tok = jnp.arange(n_assign, dtype=jnp.int32) // TOP_K
    order = jnp.argsort(flat_e, stable=True)
    sorted_e = flat_e[order]
    counts = jnp.bincount(flat_e, length=N_EXPERTS).astype(jnp.int32)
    padded = (counts + MOE_BLOCK - 1) // MOE_BLOCK * MOE_BLOCK
    pad_end = jnp.cumsum(padded)
    pad_start = pad_end - padded
    start = jnp.cumsum(counts) - counts
    dest = pad_start[sorted_e] + (jnp.arange(n_assign, dtype=jnp.int32) - start[sorted_e])
    n_blocks = -(-n_assign // MOE_BLOCK) + N_EXPERTS
    cap = n_blocks * MOE_BLOCK
    slot_tok = jnp.full((cap,), t, jnp.int32).at[dest].set(flat_tok[order])
    block_e = jnp.minimum(jnp.searchsorted(pad_end, jnp.arange(n_blocks, dtype=jnp.int32) * MOE_BLOCK,
                                           side='right'), N_EXPERTS - 1)
    x_pad = jnp.concatenate([x, jnp.zeros((1, d), x.dtype)], axis=0)
    xb = x_pad[slot_tok].reshape(n_blocks, MOE_BLOCK, d)

    def run_block(args):
        xblk, e = args
        hid = jax.nn.silu(xblk @ w1[e]) * (xblk @ w3[e])
        return hid @ w2[e]

    yb = lax.map(run_block, (xb, block_e)).reshape(cap, d)
    slot_of_assign = jnp.zeros((n_assign,), jnp.int32).at[order].set(dest)
    return yb[slot_of_assign].reshape(t, TOP_K, d)


def hier_moe(h, w_rg, b_rg, w_re, b_re, w1, w3, w2):
    bsz, seq, d = h.shape
    t = bsz * seq
    x = h.reshape(t, d)
    tok = jnp.arange(t)
    grp_logits = (x @ w_rg + b_rg).astype(jnp.float32)
    grp_prob = jax.nn.softmax(grp_logits, axis=-1)
    g_sel = jnp.argmax(grp_logits, axis=-1)
    g_w = grp_prob[tok, g_sel][:, None]
    exp_logits = (x @ w_re + b_re).astype(jnp.float32).reshape(t, N_GROUPS, EXPERTS_PER_GROUP)
    top_v, top_i = lax.top_k(exp_logits[tok, g_sel], TOP_K)
    e_w = jax.nn.softmax(top_v, axis=-1) * g_w
    expert_idx = g_sel[:, None] * EXPERTS_PER_GROUP + top_i
    y = grouped_experts(x, expert_idx, w1, w3, w2)
    return jnp.einsum('tk,tkd->td', e_w.astype(x.dtype), y).reshape(bsz, seq, d)


def setup_inputs(seed: int = 0) -> dict:
    key = jax.random.key(seed)
    ks = jax.random.split(key, 40)
    nrm = jax.random.normal
    L, D, W = DEPTH, D_MODEL, BRANCH_WIDTH
    G, P, H = SSM_GROUPS, SSM_STATE, SSM_GROUP
    E, F = N_EXPERTS, D_FF_EXPERT
    f32 = jnp.float32
    inp = {}
    inp['x'] = nrm(ks[0], (BATCH, SEQ, D), f32)
    inp['c'] = nrm(ks[1], (BATCH, D), f32)
    inp['norm_mix_g'] = 1.0 + 0.02 * nrm(ks[2], (L, D), f32)
    inp['norm_ffn_g'] = 1.0 + 0.02 * nrm(ks[3], (L, D), f32)
    inp['w_ada'] = nrm(ks[4], (L, D, 6 * D), f32) * (0.5 * D ** -0.5)
    inp['b_ada'] = 0.01 * nrm(ks[5], (L, 6 * D), f32)
    inp['w_in'] = nrm(ks[6], (L, D, N_IN_CHUNKS * W), f32) * D ** -0.5
    inp['lam_re'] = -0.5 + 0.01 * nrm(ks[7], (L, G, P), f32)
    inp['lam_im'] = (math.pi * jnp.arange(P, dtype=f32))[None, None, :] + 0.01 * nrm(ks[8], (L, G, P), f32)
    inp['log_step'] = jax.random.uniform(ks[9], (L, G), f32, minval=math.log(DT_MIN), maxval=math.log(DT_MAX))
    inp['ssm_b_re'] = nrm(ks[10], (L, G, P, H), f32) * (2 * H) ** -0.5
    inp['ssm_b_im'] = nrm(ks[11], (L, G, P, H), f32) * (2 * H) ** -0.5
    inp['ssm_c_re'] = nrm(ks[12], (L, G, H, P), f32) * (2 * P) ** -0.5
    inp['ssm_c_im'] = nrm(ks[13], (L, G, H, P), f32) * (2 * P) ** -0.5
    inp['ssm_d'] = nrm(ks[14], (L, W), f32)
    inp['w_glu'] = nrm(ks[15], (L, W, W), f32) * W ** -0.5
    inp['conf_dw_w'] = nrm(ks[16], (L, CONF_KERNEL, W), f32) * CONF_KERNEL ** -0.5
    inp['conf_dw_b'] = 0.01 * nrm(ks[17], (L, W), f32)
    inp['conf_ln_g'] = 1.0 + 0.02 * nrm(ks[18], (L, W), f32)
    inp['conf_ln_b'] = 0.01 * nrm(ks[19], (L, W), f32)
    inp['sconv_w'] = nrm(ks[20], (L, SCONV_KERNEL, W), f32) * SCONV_KERNEL ** -0.5
    inp['w_branch'] = nrm(ks[21], (L, N_BRANCH, W, D), f32) * W ** -0.5
    inp['w_gate'] = nrm(ks[22], (L, D, N_BRANCH * D), f32) * D ** -0.5
    inp['b_gate'] = 0.01 * nrm(ks[23], (L, N_BRANCH * D), f32)
    inp['w_out'] = nrm(ks[24], (L, D, D), f32) * D ** -0.5
    inp['w_router_group'] = nrm(ks[25], (L, D, N_GROUPS), f32) * D ** -0.5
    inp['b_router_group'] = 0.01 * nrm(ks[26], (L, N_GROUPS), f32)
    inp['w_router_expert'] = nrm(ks[27], (L, D, E), f32) * D ** -0.5
    inp['b_router_expert'] = 0.01 * nrm(ks[28], (L, E), f32)
    inp['w_exp_gate'] = nrm(ks[29], (L, E, D, F), f32) * D ** -0.5
    inp['w_exp_up'] = nrm(ks[30], (L, E, D, F), f32) * D ** -0.5
    inp['w_exp_down'] = nrm(ks[31], (L, E, F, D), f32) * F ** -0.5
    inp['final_norm_g'] = 1.0 + 0.02 * nrm(ks[32], (D,), f32)
    return inp


def reference(x, c, norm_mix_g, norm_ffn_g, w_ada, b_ada, w_in, lam_re, lam_im, log_step,
              ssm_b_re, ssm_b_im, ssm_c_re, ssm_c_im, ssm_d, w_glu, conf_dw_w, conf_dw_b,
              conf_ln_g, conf_ln_b, sconv_w, w_branch, w_gate, b_gate, w_out, w_router_group,
              b_router_group, w_router_expert, b_router_expert, w_exp_gate, w_exp_up, w_exp_down,
              final_norm_g):
    cond = jax.nn.silu(c)
    for l in range(DEPTH):
        mod = (cond @ w_ada[l] + b_ada[l])[:, None, :]
        sh1, sc1, g1, sh2, sc2, g2 = jnp.split(mod, 6, axis=-1)
        h = rms_norm(x, norm_mix_g[l]) * (1.0 + sc1) + sh1
        x = x + g1 * hybrid_mixer(h, w_in[l], lam_re[l], lam_im[l], log_step[l], ssm_b_re[l],
                                  ssm_b_im[l], ssm_c_re[l], ssm_c_im[l], ssm_d[l], w_glu[l],
                                  conf_dw_w[l], conf_dw_b[l], conf_ln_g[l], conf_ln_b[l],
                                  sconv_w[l], w_branch[l], w_gate[l], b_gate[l], w_out[l])
        h = rms_norm(x, norm_ffn_g[l]) * (1.0 + sc2) + sh2
        x = x + g2 * hier_moe(h, w_router_group[l], b_router_group[l], w_router_expert[l],
                              b_router_expert[l], w_exp_gate[l], w_exp_up[l], w_exp_down[l])
    return rms_norm(x, final_norm_g)
```

```python
import functools
import math

import jax
import jax.numpy as jnp
from jax import lax
from jax.experimental import pallas as pl
from jax.experimental.pallas import tpu as pltpu

F32 = jnp.float32
BF16 = jnp.bfloat16

D_MODEL = 1024
BATCH = 8
BRANCH_WIDTH = D_MODEL // 2
N_BRANCH = 3
N_IN_CHUNKS = 6
SSM_GROUP = 16
SSM_GROUPS = BRANCH_WIDTH // SSM_GROUP
SSM_STATE = 64
CONF_KERNEL = 31
SCONV_KERNEL = 3
N_GROUPS = 4
EXPERTS_PER_GROUP = 8
N_EXPERTS = N_GROUPS * EXPERTS_PER_GROUP
TOP_K = 2
D_FF_EXPERT = D_MODEL // 2
NORM_EPS = 1e-6

SUBLANES = 8
LANES = 128
VMEM_LIMIT_BYTES = 56 * 1024 * 1024

SSM_CHUNKS = 2
SSM_CHUNK_IN = BRANCH_WIDTH // SSM_CHUNKS
SSM_CHUNK_STATE = SSM_GROUPS * SSM_STATE // SSM_CHUNKS
CONF_HIST = (CONF_KERNEL - 1) * BATCH
SCONV_HIST = (SCONV_KERNEL - 1) * BATCH
ROUTER_ROWS = 40

ADA_COLS = 1536
INPROJ_ROWS = 1024
SEQ_ROWS = 512
CONV_ROWS = 64
MERGE_ROWS = 512
ROUTE_ROWS = 512
DISPATCH_ROWS = 512
EXPERT_ROWS = 256
COMBINE_ROWS = 256


def _dot(a, b):
    return jnp.dot(a, b, preferred_element_type=F32)


def _split_bf16(v):
    hi = v.astype(BF16)
    lo = (v - hi.astype(F32)).astype(BF16)
    return hi, lo


def _sigmoid(v):
    return 1.0 / (1.0 + jnp.exp(-v))


def _gelu_tanh(v):
    return 0.5 * v * (1.0 + jnp.tanh(math.sqrt(2.0 / math.pi) * (v + 0.044715 * (v * v * v))))


def _per_batch(v, fn):
    rows, d = v.shape
    return fn(v.reshape(rows // BATCH, BATCH, d)).reshape(rows, d)


def _load_token_tiles(ref, first_token, n_tokens):
    base = first_token * SUBLANES
    chunks = [ref[pl.ds(base + s, n_tokens, stride=SUBLANES), :] for s in range(SUBLANES)]
    return jnp.concatenate(chunks, axis=1)


def _store_token_tiles(ref, value):
    for s in range(SUBLANES):
        ref[pl.ds(s, value.shape[0], stride=SUBLANES), :] = value[:, s * LANES:(s + 1) * LANES]


def _token_tile(ref, token):
    return ref.at[pl.ds(pl.multiple_of(token * SUBLANES, SUBLANES), SUBLANES)]


def _norm_modulate(x, g, scale, shift):
    ms = jnp.mean(x * x, axis=-1, keepdims=True)
    h = x * lax.rsqrt(ms + NORM_EPS) * g
    return _per_batch(h, lambda h3: h3 * (1.0 + scale) + shift)


def _ada_kernel(c_ref, w_ref, b_ref, o_ref):
    c = c_ref[...]
    cond = c * _sigmoid(c)
    chi, clo = _split_bf16(cond)
    whi, wlo = _split_bf16(w_ref[...])
    o_ref[...] = _dot(chi, whi) + _dot(chi, wlo) + _dot(clo, whi) + b_ref[...]


def _ada(c, w_ada, b_ada):
    depth, d, n = w_ada.shape
    return pl.pallas_call(
        _ada_kernel,
        out_shape=jax.ShapeDtypeStruct((depth, BATCH, n), F32),
        grid=(depth, n // ADA_COLS),
        in_specs=[
            pl.BlockSpec((BATCH, d), lambda l, j: (0, 0)),
            pl.BlockSpec((None, d, ADA_COLS), lambda l, j: (l, 0, j)),
            pl.BlockSpec((None, 1, ADA_COLS), lambda l, j: (l, 0, j)),
        ],
        out_specs=pl.BlockSpec((None, BATCH, ADA_COLS), lambda l, j: (l, 0, j)),
        compiler_params=pltpu.CompilerParams(
            dimension_semantics=("arbitrary", "arbitrary"), vmem_limit_bytes=VMEM_LIMIT_BYTES),
        name="ada_modulation",
    )(c, w_ada, b_ada.reshape(depth, 1, n))


def _inproj_kernel(x_ref, g_ref, sc_ref, sh_ref, w_ref, o_ref):
    h = _norm_modulate(x_ref[...], g_ref[...], sc_ref[...], sh_ref[...])
    o_ref[...] = _dot(h.astype(BF16), w_ref[...]).astype(o_ref.dtype)


def _inproj(x, g, mod, w_in):
    t, d = x.shape
    n = w_in.shape[1]
    return pl.pallas_call(
        _inproj_kernel,
        out_shape=jax.ShapeDtypeStruct((t, n), BF16),
        grid=(t // INPROJ_ROWS,),
        in_specs=[
            pl.BlockSpec((INPROJ_ROWS, d), lambda i: (i, 0)),
            pl.BlockSpec((1, d), lambda i: (0, 0)),
            pl.BlockSpec((BATCH, d), lambda i: (0, 1)),
            pl.BlockSpec((BATCH, d), lambda i: (0, 0)),
            pl.BlockSpec((d, n), lambda i: (0, 0)),
        ],
        out_specs=pl.BlockSpec((INPROJ_ROWS, n), lambda i: (i, 0)),
        compiler_params=pltpu.CompilerParams(
            dimension_semantics=("arbitrary",), vmem_limit_bytes=VMEM_LIMIT_BYTES),
        name="mixer_inproj",
    )(x, g, mod, mod, w_in)


def _seq_kernel(p_ref, wb_ref, wc_ref, are_ref, aim_ref, d_ref, wglu_ref, cw_ref, cb_ref,
                lng_ref, lnb_ref, sw_ref, o_ref, bu_ref, st_ref, zbuf_ref, cbuf_ref, qbuf_ref):
    rows = p_ref.shape[0]
    steps = rows // BATCH
    w = BRANCH_WIDTH

    @pl.when(pl.program_id(0) == 0)
    def _():
        st_ref[...] = jnp.zeros_like(st_ref)
        zbuf_ref[0:CONF_HIST, :] = jnp.zeros((CONF_HIST, w), F32)
        qbuf_ref[0:SCONV_HIST, :] = jnp.zeros((SCONV_HIST, w), F32)

    ys = []
    for c in range(SSM_CHUNKS):
        u_c = p_ref[:, c * SSM_CHUNK_IN:(c + 1) * SSM_CHUNK_IN]
        bu_ref[...] = _dot(u_c, wb_ref[c])
        a_re = are_ref[c]
        a_im = aim_ref[c]

        def step(t, carry):
            x_re, x_im = carry
            r0 = pl.multiple_of(t * BATCH, BATCH)
            n_re = a_re * x_re - a_im * x_im + bu_ref[pl.ds(r0, BATCH), 0:SSM_CHUNK_STATE]
            n_im = a_re * x_im + a_im * x_re + bu_ref[pl.ds(r0, BATCH), SSM_CHUNK_STATE:]
            bu_ref[pl.ds(r0, BATCH), 0:SSM_CHUNK_STATE] = n_re
            bu_ref[pl.ds(r0, BATCH), SSM_CHUNK_STATE:] = n_im
            return n_re, n_im

        x_re, x_im = lax.fori_loop(0, steps, step, (st_ref[c, 0], st_ref[c, 1]), unroll=4)
        st_ref[c, 0] = x_re
        st_ref[c, 1] = x_im
        ys.append(_dot(bu_ref[...].astype(BF16), wc_ref[c]))
    u = p_ref[:, 0:w].astype(F32)
    y = _gelu_tanh(jnp.concatenate(ys, axis=1) + d_ref[...] * u)
    y_ssm = y * _sigmoid(_dot(y.astype(BF16), wglu_ref[...]))
    o_ref[:, 0:w] = y_ssm.astype(o_ref.dtype)

    v = p_ref[:, w:2 * w].astype(F32)
    g = p_ref[:, 2 * w:3 * w].astype(F32)
    zbuf_ref[CONF_HIST:CONF_HIST + rows, :] = v * _sigmoid(g)
    win_groups = (CONV_ROWS + CONF_HIST) // SUBLANES
    out_groups = CONV_ROWS // SUBLANES

    def conv_chunk(j, carry):
        r0 = pl.multiple_of(j * CONV_ROWS, CONV_ROWS)
        for s in range(w // LANES):
            ls = slice(s * LANES, (s + 1) * LANES)
            win = zbuf_ref[pl.ds(r0, CONV_ROWS + CONF_HIST), ls].reshape(win_groups, SUBLANES, LANES)
            acc = win[0:out_groups] * cw_ref[0:SUBLANES, ls]
            for k in range(1, CONF_KERNEL):
                acc = acc + win[k:k + out_groups] * cw_ref[k * SUBLANES:(k + 1) * SUBLANES, ls]
            cbuf_ref[pl.ds(r0, CONV_ROWS), ls] = acc.reshape(CONV_ROWS, LANES)
        return carry

    lax.fori_loop(0, rows // CONV_ROWS, conv_chunk, 0)
    zbuf_ref[0:CONF_HIST, :] = zbuf_ref[rows:rows + CONF_HIST, :]
    cv = cbuf_ref[...] + cb_ref[...]
    cc = cv - jnp.mean(cv, axis=-1, keepdims=True)
    ln = cc * lax.rsqrt(jnp.mean(cc * cc, axis=-1, keepdims=True) + NORM_EPS) * lng_ref[...] + lnb_ref[...]
    o_ref[:, w:2 * w] = (ln * _sigmoid(ln)).astype(o_ref.dtype)

    gate_b = p_ref[:, 3 * w:4 * w].astype(F32)
    gate_c = p_ref[:, 4 * w:5 * w].astype(F32)
    hv = p_ref[:, 5 * w:6 * w].astype(F32)
    qbuf_ref[SCONV_HIST:SCONV_HIST + rows, :] = gate_c * hv
    q3 = qbuf_ref[...].reshape((rows + SCONV_HIST) // SUBLANES, SUBLANES, w)
    acc = q3[0:steps] * sw_ref[0:SUBLANES, :]
    for k in range(1, SCONV_KERNEL):
        acc = acc + q3[k:k + steps] * sw_ref[k * SUBLANES:(k + 1) * SUBLANES, :]
    qbuf_ref[0:SCONV_HIST, :] = qbuf_ref[rows:rows + SCONV_HIST, :]
    o_ref[:, 2 * w:3 * w] = (gate_b * acc.reshape(rows, w)).astype(o_ref.dtype)


def _seq_mixers(proj, wb, wc, a_re, a_im, d_skip, w_glu, cw, cb, lng, lnb, sw):
    t = proj.shape[0]
    w = BRANCH_WIDTH
    const2 = lambda i: (0, 0)
    const3 = lambda i: (0, 0, 0)
    return pl.pallas_call(
        _seq_kernel,
        out_shape=jax.ShapeDtypeStruct((t, N_BRANCH * w), BF16),
        grid=(t // SEQ_ROWS,),
        in_specs=[
            pl.BlockSpec((SEQ_ROWS, N_IN_CHUNKS * w), lambda i: (i, 0)),
            pl.BlockSpec(wb.shape, const3),
            pl.BlockSpec(wc.shape, const3),
            pl.BlockSpec(a_re.shape, const3),
            pl.BlockSpec(a_im.shape, const3),
            pl.BlockSpec((1, w), const2),
            pl.BlockSpec((w, w), const2),
            pl.BlockSpec(cw.shape, const2),
            pl.BlockSpec((1, w), const2),
            pl.BlockSpec((1, w), const2),
            pl.BlockSpec((1, w), const2),
            pl.BlockSpec(sw.shape, const2),
        ],
        out_specs=pl.BlockSpec((SEQ_ROWS, N_BRANCH * w), lambda i: (i, 0)),
        scratch_shapes=[
            pltpu.VMEM((SEQ_ROWS, 2 * SSM_CHUNK_STATE), F32),
            pltpu.VMEM((SSM_CHUNKS, 2, BATCH, SSM_CHUNK_STATE), F32),
            pltpu.VMEM((SEQ_ROWS + CONF_HIST, w), F32),
            pltpu.VMEM((SEQ_ROWS, w), F32),
            pltpu.VMEM((SEQ_ROWS + SCONV_HIST, w), F32),
        ],
        compiler_params=pltpu.CompilerParams(
            dimension_semantics=("arbitrary",), vmem_limit_bytes=VMEM_LIMIT_BYTES),
        name="seq_mixers",
    )(proj, wb, wc, a_re, a_im, d_skip, w_glu, cw, cb, lng, lnb, sw)


def _merge_kernel(x_ref, g_ref, sc_ref, sh_ref, g1_ref, y_ref, wg_ref, bg_ref, wbr_ref, wo_ref, o_ref):
    x = x_ref[...]
    d = x.shape[1]
    w = BRANCH_WIDTH
    hb = _norm_modulate(x, g_ref[...], sc_ref[...], sh_ref[...]).astype(BF16)
    merged = None
    for n in range(N_BRANCH):
        gate = _sigmoid(_dot(hb, wg_ref[:, n * d:(n + 1) * d]) + bg_ref[:, n * d:(n + 1) * d])
        term = gate * _dot(y_ref[:, n * w:(n + 1) * w], wbr_ref[n])
        merged = term if merged is None else merged + term
    out = _dot(merged.astype(BF16), wo_ref[...])
    o_ref[...] = x + _per_batch(out, lambda o3: o3 * g1_ref[...])


def _merge(x, g, mod, ycat, w_gate, b_gate, w_branch, w_out):
    t, d = x.shape
    w = BRANCH_WIDTH
    return pl.pallas_call(
        _merge_kernel,
        out_shape=jax.ShapeDtypeStruct((t, d), F32),
        grid=(t // MERGE_ROWS,),
        in_specs=[
            pl.BlockSpec((MERGE_ROWS, d), lambda i: (i, 0)),
            pl.BlockSpec((1, d), lambda i: (0, 0)),
            pl.BlockSpec((BATCH, d), lambda i: (0, 1)),
            pl.BlockSpec((BATCH, d), lambda i: (0, 0)),
            pl.BlockSpec((BATCH, d), lambda i: (0, 2)),
            pl.BlockSpec((MERGE_ROWS, N_BRANCH * w), lambda i: (i, 0)),
            pl.BlockSpec((d, N_BRANCH * d), lambda i: (0, 0)),
            pl.BlockSpec((1, N_BRANCH * d), lambda i: (0, 0)),
            pl.BlockSpec((N_BRANCH, w, d), lambda i: (0, 0, 0)),
            pl.BlockSpec((d, d), lambda i: (0, 0)),
        ],
        out_specs=pl.BlockSpec((MERGE_ROWS, d), lambda i: (i, 0)),
        compiler_params=pltpu.CompilerParams(
            dimension_semantics=("arbitrary",), vmem_limit_bytes=VMEM_LIMIT_BYTES),
        name="mixer_merge",
    )(x, g, mod, mod, mod, ycat, w_gate, b_gate, w_branch, w_out)


def _router_kernel(x_ref, g_ref, sc_ref, sh_ref, whi_ref, wlo_ref, b_ref, tri_ref,
                   h_ref, route_ref, cnt_ref, run_ref):
    @pl.when(pl.program_id(0) == 0)
    def _():
        run_ref[...] = jnp.zeros_like(run_ref)

    h = _norm_modulate(x_ref[...], g_ref[...], sc_ref[...], sh_ref[...])
    _store_token_tiles(h_ref, h)
    rows = h.shape[0]
    hhi, hlo = _split_bf16(h)
    contract_last = (((1,), (1,)), ((), ()))
    dott = lambda a, b: lax.dot_general(a, b, contract_last, preferred_element_type=F32)
    logits = dott(whi_ref[...], hhi) + dott(wlo_ref[...], hhi) + dott(whi_ref[...], hlo) + b_ref[...]
    lg = logits[0:N_GROUPS]
    le = logits[SUBLANES:SUBLANES + N_EXPERTS]

    g_row = lax.broadcasted_iota(jnp.int32, (N_GROUPS, rows), 0).astype(F32)
    g_max = jnp.max(lg, axis=0, keepdims=True)
    g_sel = jnp.min(jnp.where(lg == g_max, g_row, float(N_GROUPS)), axis=0, keepdims=True)
    g_w = 1.0 / jnp.sum(jnp.exp(lg - g_max), axis=0, keepdims=True)

    e_row = lax.broadcasted_iota(jnp.int32, (N_EXPERTS, rows), 0).astype(F32)
    e_grp = jnp.floor(e_row * (1.0 / EXPERTS_PER_GROUP))
    neg = float("-inf")
    lm = jnp.where(e_grp == g_sel, le, neg)
    v1 = jnp.max(lm, axis=0, keepdims=True)
    i1 = jnp.min(jnp.where(lm == v1, e_row, float(N_EXPERTS)), axis=0, keepdims=True)
    lm2 = jnp.where(e_row == i1, neg, lm)
    v2 = jnp.max(lm2, axis=0, keepdims=True)
    i2 = jnp.min(jnp.where(lm2 == v2, e_row, float(N_EXPERTS)), axis=0, keepdims=True)
    e21 = jnp.exp(v2 - v1)
    p1 = 1.0 / (1.0 + e21)
    w1 = p1 * g_w
    w2 = e21 * p1 * g_w

    hit1 = e_row == i1
    hit2 = e_row == i2
    onehot = jnp.where(hit1 | hit2, 1.0, 0.0)
    cum = _dot(onehot.astype(BF16), tri_ref[...])
    before = cum - onehot + run_ref[...]
    r1 = jnp.sum(jnp.where(hit1, before, 0.0), axis=0, keepdims=True)
    r2 = jnp.sum(jnp.where(hit2, before, 0.0), axis=0, keepdims=True)
    run_ref[...] = run_ref[...] + jnp.broadcast_to(cum[:, rows - 1:rows], run_ref.shape)
    cnt_ref[...] = run_ref[...]
    zero = jnp.zeros_like(w1)
    route_ref[...] = jnp.concatenate([i1, i2, r1, r2, w1, w2, zero, zero], axis=0)


def _router(x, g, mod, w_hi, w_lo, bias, tri):
    t, d = x.shape
    return pl.pallas_call(
        _router_kernel,
        out_shape=(
            jax.ShapeDtypeStruct((t * SUBLANES, LANES), F32),
            jax.ShapeDtypeStruct((SUBLANES, t), F32),
            jax.ShapeDtypeStruct((N_EXPERTS, ROUTE_ROWS), F32),
        ),
        grid=(t // ROUTE_ROWS,),
        in_specs=[
            pl.BlockSpec((ROUTE_ROWS, d), lambda i: (i, 0)),
            pl.BlockSpec((1, d), lambda i: (0, 0)),
            pl.BlockSpec((BATCH, d), lambda i: (0, 4)),
            pl.BlockSpec((BATCH, d), lambda i: (0, 3)),
            pl.BlockSpec((ROUTER_ROWS, d), lambda i: (0, 0)),
            pl.BlockSpec((ROUTER_ROWS, d), lambda i: (0, 0)),
            pl.BlockSpec((ROUTER_ROWS, ROUTE_ROWS), lambda i: (0, 0)),
            pl.BlockSpec((ROUTE_ROWS, ROUTE_ROWS), lambda i: (0, 0)),
        ],
        out_specs=(
            pl.BlockSpec((ROUTE_ROWS * SUBLANES, LANES), lambda i: (i, 0)),
            pl.BlockSpec((SUBLANES, ROUTE_ROWS), lambda i: (0, i)),
            pl.BlockSpec((N_EXPERTS, ROUTE_ROWS), lambda i: (0, 0)),
        ),
        scratch_shapes=[pltpu.VMEM((N_EXPERTS, ROUTE_ROWS), F32)],
        compiler_params=pltpu.CompilerParams(
            dimension_semantics=("arbitrary",), vmem_limit_bytes=VMEM_LIMIT_BYTES),
        name="moe_router",
    )(x, g, mod, mod, w_hi, w_lo, bias, tri)


def _row_copy(src_hbm, src_token, dst_hbm, dst_token, sem):
    return pltpu.make_async_copy(_token_tile(src_hbm, src_token), _token_tile(dst_hbm, dst_token), sem)


def _dispatch_kernel(zstart_ref, dest_hbm, h_hbm, zeros_hbm, xs_hbm, idx_smem, idx_sem, zero_sem, row_sem):
    i = pl.program_id(0)
    n_idx = idx_smem.shape[0]
    tile = n_idx // TOP_K
    idx_copy = pltpu.make_async_copy(dest_hbm.at[pl.ds(i * n_idx, n_idx)], idx_smem, idx_sem)
    idx_copy.start()

    def zero_copy(e):
        block_rows = EXPERT_ROWS * SUBLANES
        start = pl.multiple_of(zstart_ref[e] * SUBLANES, block_rows)
        return pltpu.make_async_copy(zeros_hbm, xs_hbm.at[pl.ds(start, block_rows)], zero_sem)

    @pl.when(i == 0)
    def _():
        def start(e, carry):
            @pl.when(zstart_ref[e] >= 0)
            def _():
                zero_copy(e).start()
            return carry

        def wait(e, carry):
            @pl.when(zstart_ref[e] >= 0)
            def _():
                zero_copy(e).wait()
            return carry

        lax.fori_loop(0, N_EXPERTS, start, 0)
        lax.fori_loop(0, N_EXPERTS, wait, 0)

    idx_copy.wait()

    def start_rows(t, carry):
        for k in range(TOP_K):
            _row_copy(h_hbm, i * tile + t, xs_hbm, idx_smem[k * tile + t], row_sem).start()
        return carry

    def wait_rows(t, carry):
        for k in range(TOP_K):
            _row_copy(h_hbm, 0, xs_hbm, 0, row_sem).wait()
        return carry

    lax.fori_loop(0, tile, start_rows, 0)
    lax.fori_loop(0, tile, wait_rows, 0)


def _dispatch(zstart, dest_tiles, h, cap):
    t = h.shape[0] // SUBLANES
    zeros = jnp.zeros((EXPERT_ROWS * SUBLANES, LANES), h.dtype)
    any_spec = pl.BlockSpec(memory_space=pl.ANY)
    return pl.pallas_call(
        _dispatch_kernel,
        out_shape=jax.ShapeDtypeStruct((cap * SUBLANES, LANES), h.dtype),
        grid_spec=pltpu.PrefetchScalarGridSpec(
            num_scalar_prefetch=1,
            grid=(t // DISPATCH_ROWS,),
            in_specs=[any_spec, any_spec, any_spec],
            out_specs=any_spec,
            scratch_shapes=[
                pltpu.SMEM((TOP_K * DISPATCH_ROWS,), jnp.int32),
                pltpu.SemaphoreType.DMA,
                pltpu.SemaphoreType.DMA,
                pltpu.SemaphoreType.DMA,
            ],
        ),
        compiler_params=pltpu.CompilerParams(
            dimension_semantics=("arbitrary",), has_side_effects=True),
        name="moe_dispatch",
    )(zstart, dest_tiles, h, zeros)


def _expert_kernel(be_ref, nv_ref, x_ref, w1_ref, w3_ref, w2_ref, o_ref):
    @pl.when(pl.program_id(0) < nv_ref[0])
    def _():
        xb = _load_token_tiles(x_ref, 0, EXPERT_ROWS).astype(BF16)
        a = _dot(xb, w1_ref[...])
        hid = a * _sigmoid(a) * _dot(xb, w3_ref[...])
        _store_token_tiles(o_ref, _dot(hid.astype(BF16), w2_ref[...]))


def _experts(block_e, n_valid, xs, w1, w3, w2):
    cap = xs.shape[0] // SUBLANES
    _, d, f = w1.shape
    last = lambda i, nv: jnp.minimum(i, nv[0] - 1)
    tile_rows = EXPERT_ROWS * SUBLANES
    return pl.pallas_call(
        _expert_kernel,
        out_shape=jax.ShapeDtypeStruct(xs.shape, F32),
        grid_spec=pltpu.PrefetchScalarGridSpec(
            num_scalar_prefetch=2,
            grid=(cap // EXPERT_ROWS,),
            in_specs=[
                pl.BlockSpec((tile_rows, LANES), lambda i, be, nv: (last(i, nv), 0)),
                pl.BlockSpec((None, d, f), lambda i, be, nv: (be[last(i, nv)], 0, 0)),
                pl.BlockSpec((None, d, f), lambda i, be, nv: (be[last(i, nv)], 0, 0)),
                pl.BlockSpec((None, f, d), lambda i, be, nv: (be[last(i, nv)], 0, 0)),
            ],
            out_specs=pl.BlockSpec((tile_rows, LANES), lambda i, be, nv: (last(i, nv), 0)),
        ),
        compiler_params=pltpu.CompilerParams(
            dimension_semantics=("arbitrary",), vmem_limit_bytes=VMEM_LIMIT_BYTES),
        name="moe_experts",
    )(block_e, n_valid, xs, w1, w3, w2)


def _combine_kernel(final_norm, dest_hbm, ys_hbm, x_ref, g2_ref, wt_ref, fg_ref, o_ref,
                    idx_smem, ybuf_ref, idx_sem, row_sem):
    i = pl.program_id(0)
    n_idx = idx_smem.shape[0]
    tile = n_idx // TOP_K
    idx_copy = pltpu.make_async_copy(dest_hbm.at[pl.ds(i * n_idx, n_idx)], idx_smem, idx_sem)
    idx_copy.start()
    idx_copy.wait()

    def row_copy(src_token, dst_token):
        return pltpu.make_async_copy(_token_tile(ys_hbm, src_token), _token_tile(ybuf_ref, dst_token), row_sem)

    def start_rows(t, carry):
        for k in range(TOP_K):
            row_copy(idx_smem[k * tile + t], k * tile + t).start()
        return carry

    def wait_rows(t, carry):
        for k in range(TOP_K):
            row_copy(0, 0).wait()
        return carry

    lax.fori_loop(0, tile, start_rows, 0)
    lax.fori_loop(0, tile, wait_rows, 0)

    wt = wt_ref[...]
    y = (wt[:, 0:1] * _load_token_tiles(ybuf_ref, 0, tile)
         + wt[:, 1:2] * _load_token_tiles(ybuf_ref, tile, tile))
    out = x_ref[...] + _per_batch(y, lambda y3: y3 * g2_ref[...])
    if final_norm:
        ms = jnp.mean(out * out, axis=-1, keepdims=True)
        out = out * lax.rsqrt(ms + NORM_EPS) * fg_ref[...]
    o_ref[...] = out


def _combine(dest_tiles, ys, x, mod, wt, final_g, final_norm):
    t, d = x.shape
    any_spec = pl.BlockSpec(memory_space=pl.ANY)
    return pl.pallas_call(
        functools.partial(_combine_kernel, final_norm),
        out_shape=jax.ShapeDtypeStruct((t, d), F32),
        grid=(t // COMBINE_ROWS,),
        in_specs=[
            any_spec,
            any_spec,
            pl.BlockSpec((COMBINE_ROWS, d), lambda i: (i, 0)),
            pl.BlockSpec((BATCH, d), lambda i: (0, 5)),
            pl.BlockSpec((COMBINE_ROWS, TOP_K), lambda i: (i, 0)),
            pl.BlockSpec((1, d), lambda i: (0, 0)),
        ],
        out_specs=pl.BlockSpec((COMBINE_ROWS, d), lambda i: (i, 0)),
        scratch_shapes=[
            pltpu.SMEM((TOP_K * COMBINE_ROWS,), jnp.int32),
            pltpu.VMEM((TOP_K * COMBINE_ROWS * SUBLANES, LANES), F32),
            pltpu.SemaphoreType.DMA,
            pltpu.SemaphoreType.DMA,
        ],
        compiler_params=pltpu.CompilerParams(
            dimension_semantics=("arbitrary",), vmem_limit_bytes=VMEM_LIMIT_BYTES),
        name="moe_combine",
    )(dest_tiles, ys, x, mod, wt, final_g)


def _ssm_params(lam_re, lam_im, log_step, b_re, b_im, c_re, c_im):
    step = jnp.exp(log_step)[:, None]
    mag = jnp.exp(lam_re * step)
    a_re, a_im = mag * jnp.cos(lam_im * step), mag * jnp.sin(lam_im * step)
    den = lam_re * lam_re + lam_im * lam_im
    n_re, n_im = a_re - 1.0, a_im
    k_re = (n_re * lam_re + n_im * lam_im) / den
    k_im = (n_im * lam_re - n_re * lam_im) / den
    bb_re = k_re[..., None] * b_re - k_im[..., None] * b_im
    bb_im = k_re[..., None] * b_im + k_im[..., None] * b_re
    gpc = SSM_GROUPS // SSM_CHUNKS
    eye = jnp.eye(gpc, dtype=F32)

    def in_blocks(bb):
        bb = bb.reshape(SSM_CHUNKS, gpc, SSM_STATE, SSM_GROUP)
        return jnp.einsum("cgph,gk->cghkp", bb, eye).reshape(SSM_CHUNKS, SSM_CHUNK_IN, SSM_CHUNK_STATE)

    def out_blocks(cc):
        cc = cc.reshape(SSM_CHUNKS, gpc, SSM_GROUP, SSM_STATE)
        return jnp.einsum("cghp,gk->cgpkh", cc, eye).reshape(SSM_CHUNKS, SSM_CHUNK_STATE, SSM_CHUNK_IN)

    wb = jnp.concatenate([in_blocks(bb_re), in_blocks(bb_im)], axis=2).astype(BF16)
    wc = jnp.concatenate([out_blocks(c_re), out_blocks(-c_im)], axis=1).astype(BF16)
    bcast = lambda a: jnp.broadcast_to(a.reshape(SSM_CHUNKS, 1, SSM_CHUNK_STATE),
                                       (SSM_CHUNKS, BATCH, SSM_CHUNK_STATE))
    return wb, wc, bcast(a_re), bcast(a_im)


def _rows8(w):
    k, c = w.shape
    return jnp.broadcast_to(w[:, None, :], (k, SUBLANES, c)).reshape(k * SUBLANES, c)


def _router_weights(w_rg, b_rg, w_re, b_re):
    d = w_rg.shape[0]
    pad = jnp.zeros((SUBLANES - N_GROUPS, d), F32)
    w_t = jnp.concatenate([w_rg.T, pad, w_re.T], axis=0)
    b = jnp.concatenate([b_rg, jnp.zeros((SUBLANES - N_GROUPS,), F32), b_re])
    hi, lo = _split_bf16(w_t)
    return hi, lo, jnp.broadcast_to(b[:, None], (ROUTER_ROWS, ROUTE_ROWS))


def _routing_tables(route, counts):
    t = route.shape[1]
    counts = counts.astype(jnp.int32)
    padded = (counts + EXPERT_ROWS - 1) // EXPERT_ROWS * EXPERT_ROWS
    pad_end = jnp.cumsum(padded)
    pad_start = pad_end - padded
    experts = route[0:TOP_K].astype(jnp.int32)
    ranks = route[TOP_K:2 * TOP_K].astype(jnp.int32)
    dest = pad_start[experts] + ranks

    def tiles(rows_per_tile):
        n = t // rows_per_tile
        return dest.reshape(TOP_K, n, rows_per_tile).transpose(1, 0, 2).reshape(-1)

    n_blocks = (TOP_K * t) // EXPERT_ROWS + N_EXPERTS
    block_start = jnp.arange(n_blocks, dtype=jnp.int32) * EXPERT_ROWS
    block_e = jnp.minimum(jnp.searchsorted(pad_end, block_start, side="right"), N_EXPERTS - 1)
    n_valid = (pad_end[-1] // EXPERT_ROWS).reshape(1)
    zstart = jnp.where(counts > 0, pad_end - EXPERT_ROWS, -1)
    weights = route[2 * TOP_K:3 * TOP_K].T
    return (tiles(DISPATCH_ROWS), tiles(COMBINE_ROWS), block_e.astype(jnp.int32),
            n_valid.astype(jnp.int32), zstart.astype(jnp.int32), weights, n_blocks * EXPERT_ROWS)


def kernel(x, c, norm_mix_g, norm_ffn_g, w_ada, b_ada, w_in, lam_re, lam_im, log_step, ssm_b_re, ssm_b_im, ssm_c_re, ssm_c_im, ssm_d, w_glu, conf_dw_w, conf_dw_b, conf_ln_g, conf_ln_b, sconv_w, w_branch, w_gate, b_gate, w_out, w_router_group, b_router_group, w_router_expert, b_router_expert, w_exp_gate, w_exp_up, w_exp_down, final_norm_g):
    bsz, seq, d = x.shape
    assert bsz == BATCH and d == D_MODEL
    depth = w_in.shape[0]
    t = bsz * seq
    row = lambda v: v.reshape(1, -1)

    mod_all = _ada(c, w_ada, b_ada)
    xt = jnp.transpose(x, (1, 0, 2)).reshape(t, d)
    tri = jnp.triu(jnp.ones((ROUTE_ROWS, ROUTE_ROWS), BF16))

    for l in range(depth):
        mod = mod_all[l]
        wb, wc, a_re, a_im = _ssm_params(lam_re[l], lam_im[l], log_step[l], ssm_b_re[l], ssm_b_im[l],
                                         ssm_c_re[l], ssm_c_im[l])
        proj = _inproj(xt, row(norm_mix_g[l]), mod, w_in[l].astype(BF16))
        ycat = _seq_mixers(proj, wb, wc, a_re, a_im, row(ssm_d[l]), w_glu[l].astype(BF16),
                           _rows8(conf_dw_w[l]), row(conf_dw_b[l]), row(conf_ln_g[l]), row(conf_ln_b[l]),
                           _rows8(sconv_w[l]))
        xt = _merge(xt, row(norm_mix_g[l]), mod, ycat, w_gate[l].astype(BF16), row(b_gate[l]),
                    w_branch[l].astype(BF16), w_out[l].astype(BF16))

        r_hi, r_lo, r_b = _router_weights(w_router_group[l], b_router_group[l],
                                          w_router_expert[l], b_router_expert[l])
        h2, route, counts = _router(xt, row(norm_ffn_g[l]), mod, r_hi, r_lo, r_b, tri)
        dest_d, dest_c, block_e, n_valid, zstart, weights, cap = _routing_tables(route, counts[:, 0])
        xs = _dispatch(zstart, dest_d, h2, cap)
        ys = _experts(block_e, n_valid, xs, w_exp_gate[l].astype(BF16), w_exp_up[l].astype(BF16),
                      w_exp_down[l].astype(BF16))
        xt = _combine(dest_c, ys, xt, mod, weights, row(final_norm_g), l == depth - 1)

    return jnp.transpose(xt.reshape(seq, bsz, d), (1, 0, 2))
```

```python
import functools
import math

import jax
import jax.numpy as jnp
from jax import lax
from jax.experimental import pallas as pl
from jax.experimental.pallas import tpu as pltpu

F32 = jnp.float32
BF16 = jnp.bfloat16

D_MODEL = 1024
BATCH = 8
BRANCH_WIDTH = D_MODEL // 2
N_BRANCH = 3
N_IN_CHUNKS = 6
SSM_GROUP = 16
SSM_GROUPS = BRANCH_WIDTH // SSM_GROUP
SSM_STATE = 64
CONF_KERNEL = 31
SCONV_KERNEL = 3
N_GROUPS = 4
EXPERTS_PER_GROUP = 8
N_EXPERTS = N_GROUPS * EXPERTS_PER_GROUP
TOP_K = 2
D_FF_EXPERT = D_MODEL // 2
NORM_EPS = 1e-6

SUBLANES = 8
LANES = 128
VMEM_LIMIT_BYTES = 56 * 1024 * 1024

SSM_CHUNKS = 2
SSM_CHUNK_IN = BRANCH_WIDTH // SSM_CHUNKS
SSM_CHUNK_STATE = SSM_GROUPS * SSM_STATE // SSM_CHUNKS
CONF_HIST = (CONF_KERNEL - 1) * BATCH
SCONV_HIST = (SCONV_KERNEL - 1) * BATCH
ROUTER_ROWS = 40
MOD_SHIFT1, MOD_SCALE1, MOD_GATE1, MOD_SHIFT2, MOD_SCALE2, MOD_GATE2 = range(6)

ADA_COLS = 1536
INPROJ_ROWS = 1024
SEQ_ROWS = 512
CONV_ROWS = 64
MERGE_ROWS = 512
ROUTE_ROWS = 512
DISPATCH_ROWS = 512
EXPERT_ROWS = 512
COMBINE_ROWS = 256


def _dot(a, b):
    return jnp.dot(a, b, preferred_element_type=F32)


def _split_bf16(v):
    hi = v.astype(BF16)
    lo = (v - hi.astype(F32)).astype(BF16)
    return hi, lo


def _sigmoid(v):
    return 1.0 / (1.0 + jnp.exp(-v))


def _gelu_tanh(v):
    return 0.5 * v * (1.0 + jnp.tanh(math.sqrt(2.0 / math.pi) * (v + 0.044715 * (v * v * v))))


def _per_batch(v, fn):
    rows, d = v.shape
    return fn(v.reshape(rows // BATCH, BATCH, d)).reshape(rows, d)


def _load_token_tiles(ref, first_token, n_tokens):
    base = first_token * SUBLANES
    chunks = [ref[pl.ds(base + s, n_tokens, stride=SUBLANES), :] for s in range(SUBLANES)]
    return jnp.concatenate(chunks, axis=1)


def _store_token_tiles(ref, value):
    for s in range(SUBLANES):
        ref[pl.ds(s, value.shape[0], stride=SUBLANES), :] = value[:, s * LANES:(s + 1) * LANES]


def _token_tile(ref, token):
    return ref.at[pl.ds(pl.multiple_of(token * SUBLANES, SUBLANES), SUBLANES)]


def _layer_spec(arr, l, col_block=None):
    tail = arr.shape[1:]
    if col_block is None:
        zeros = (0,) * len(tail)
        return pl.BlockSpec((None,) + tail, lambda *_: (l,) + zeros)
    return pl.BlockSpec((None, tail[0], D_MODEL), lambda *_: (l, 0, col_block))


def _norm_modulate(x, g, scale, shift):
    ms = jnp.mean(x * x, axis=-1, keepdims=True)
    h = x * lax.rsqrt(ms + NORM_EPS) * g
    return _per_batch(h, lambda h3: h3 * (1.0 + scale) + shift)


def _ada_kernel(c_ref, w_ref, b_ref, o_ref):
    c = c_ref[...]
    cond = c * _sigmoid(c)
    chi, clo = _split_bf16(cond)
    whi, wlo = _split_bf16(w_ref[...])
    o_ref[...] = _dot(chi, whi) + _dot(chi, wlo) + _dot(clo, whi) + b_ref[...]


def _ada(c, w_ada, b_ada):
    depth, d, n = w_ada.shape
    return pl.pallas_call(
        _ada_kernel,
        out_shape=jax.ShapeDtypeStruct((depth, BATCH, n), F32),
        grid=(depth, n // ADA_COLS),
        in_specs=[
            pl.BlockSpec((BATCH, d), lambda l, j: (0, 0)),
            pl.BlockSpec((None, d, ADA_COLS), lambda l, j: (l, 0, j)),
            pl.BlockSpec((None, 1, ADA_COLS), lambda l, j: (l, 0, j)),
        ],
        out_specs=pl.BlockSpec((None, BATCH, ADA_COLS), lambda l, j: (l, 0, j)),
        compiler_params=pltpu.CompilerParams(
            dimension_semantics=("arbitrary", "arbitrary"), vmem_limit_bytes=VMEM_LIMIT_BYTES),
        name="ada_modulation",
    )(c, w_ada, b_ada.reshape(depth, 1, n))


def _inproj_kernel(x_ref, g_ref, sc_ref, sh_ref, w_ref, o_ref):
    h = _norm_modulate(x_ref[...], g_ref[...], sc_ref[...], sh_ref[...])
    o_ref[...] = _dot(h.astype(BF16), w_ref[...]).astype(o_ref.dtype)


def _inproj(l, x, g, mod, w_in):
    t, d = x.shape
    n = w_in.shape[2]
    return pl.pallas_call(
        _inproj_kernel,
        out_shape=jax.ShapeDtypeStruct((t, n), BF16),
        grid=(t // INPROJ_ROWS,),
        in_specs=[
            pl.BlockSpec((INPROJ_ROWS, d), lambda i: (i, 0)),
            _layer_spec(g, l),
            _layer_spec(mod, l, MOD_SCALE1),
            _layer_spec(mod, l, MOD_SHIFT1),
            _layer_spec(w_in, l),
        ],
        out_specs=pl.BlockSpec((INPROJ_ROWS, n), lambda i: (i, 0)),
        compiler_params=pltpu.CompilerParams(
            dimension_semantics=("arbitrary",), vmem_limit_bytes=VMEM_LIMIT_BYTES),
        name="mixer_inproj",
    )(x, g, mod, mod, w_in)


def _seq_kernel(p_ref, wb_ref, wc_ref, are_ref, aim_ref, d_ref, wglu_ref, cw_ref, cb_ref,
                lng_ref, lnb_ref, sw_ref, o_ref, bu_ref, st_ref, zbuf_ref, cbuf_ref, qbuf_ref):
    rows = p_ref.shape[0]
    steps = rows // BATCH
    w = BRANCH_WIDTH

    @pl.when(pl.program_id(0) == 0)
    def _():
        st_ref[...] = jnp.zeros_like(st_ref)
        zbuf_ref[0:CONF_HIST, :] = jnp.zeros((CONF_HIST, w), F32)
        qbuf_ref[0:SCONV_HIST, :] = jnp.zeros((SCONV_HIST, w), F32)

    ys = []
    for c in range(SSM_CHUNKS):
        u_c = p_ref[:, c * SSM_CHUNK_IN:(c + 1) * SSM_CHUNK_IN]
        bu_ref[...] = _dot(u_c, wb_ref[c])
        a_re = are_ref[c]
        a_im = aim_ref[c]

        def step(t, carry):
            x_re, x_im = carry
            r0 = pl.multiple_of(t * BATCH, BATCH)
            n_re = a_re * x_re - a_im * x_im + bu_ref[pl.ds(r0, BATCH), 0:SSM_CHUNK_STATE]
            n_im = a_re * x_im + a_im * x_re + bu_ref[pl.ds(r0, BATCH), SSM_CHUNK_STATE:]
            bu_ref[pl.ds(r0, BATCH), 0:SSM_CHUNK_STATE] = n_re
            bu_ref[pl.ds(r0, BATCH), SSM_CHUNK_STATE:] = n_im
            return n_re, n_im

        x_re, x_im = lax.fori_loop(0, steps, step, (st_ref[c, 0], st_ref[c, 1]), unroll=4)
        st_ref[c, 0] = x_re
        st_ref[c, 1] = x_im
        ys.append(_dot(bu_ref[...].astype(BF16), wc_ref[c]))
    u = p_ref[:, 0:w].astype(F32)
    y = _gelu_tanh(jnp.concatenate(ys, axis=1) + d_ref[...] * u)
    y_ssm = y * _sigmoid(_dot(y.astype(BF16), wglu_ref[...]))
    o_ref[:, 0:w] = y_ssm.astype(o_ref.dtype)

    v = p_ref[:, w:2 * w].astype(F32)
    g = p_ref[:, 2 * w:3 * w].astype(F32)
    zbuf_ref[CONF_HIST:CONF_HIST + rows, :] = v * _sigmoid(g)
    win_groups = (CONV_ROWS + CONF_HIST) // SUBLANES
    out_groups = CONV_ROWS // SUBLANES

    def conv_chunk(j, carry):
        r0 = pl.multiple_of(j * CONV_ROWS, CONV_ROWS)
        for s in range(w // LANES):
            ls = slice(s * LANES, (s + 1) * LANES)
            win = zbuf_ref[pl.ds(r0, CONV_ROWS + CONF_HIST), ls].reshape(win_groups, SUBLANES, LANES)
            acc = win[0:out_groups] * cw_ref[0:SUBLANES, ls]
            for k in range(1, CONF_KERNEL):
                acc = acc + win[k:k + out_groups] * cw_ref[k * SUBLANES:(k + 1) * SUBLANES, ls]
            cbuf_ref[pl.ds(r0, CONV_ROWS), ls] = acc.reshape(CONV_ROWS, LANES)
        return carry

    lax.fori_loop(0, rows // CONV_ROWS, conv_chunk, 0)
    zbuf_ref[0:CONF_HIST, :] = zbuf_ref[rows:rows + CONF_HIST, :]
    cv = cbuf_ref[...] + cb_ref[...]
    cc = cv - jnp.mean(cv, axis=-1, keepdims=True)
    ln = cc * lax.rsqrt(jnp.mean(cc * cc, axis=-1, keepdims=True) + NORM_EPS) * lng_ref[...] + lnb_ref[...]
    o_ref[:, w:2 * w] = (ln * _sigmoid(ln)).astype(o_ref.dtype)

    gate_b = p_ref[:, 3 * w:4 * w].astype(F32)
    gate_c = p_ref[:, 4 * w:5 * w].astype(F32)
    hv = p_ref[:, 5 * w:6 * w].astype(F32)
    qbuf_ref[SCONV_HIST:SCONV_HIST + rows, :] = gate_c * hv
    q3 = qbuf_ref[...].reshape((rows + SCONV_HIST) // SUBLANES, SUBLANES, w)
    acc = q3[0:steps] * sw_ref[0:SUBLANES, :]
    for k in range(1, SCONV_KERNEL):
        acc = acc + q3[k:k + steps] * sw_ref[k * SUBLANES:(k + 1) * SUBLANES, :]
    qbuf_ref[0:SCONV_HIST, :] = qbuf_ref[rows:rows + SCONV_HIST, :]
    o_ref[:, 2 * w:3 * w] = (gate_b * acc.reshape(rows, w)).astype(o_ref.dtype)


def _seq_mixers(l, proj, *layer_params):
    t = proj.shape[0]
    w = BRANCH_WIDTH
    return pl.pallas_call(
        _seq_kernel,
        out_shape=jax.ShapeDtypeStruct((t, N_BRANCH * w), BF16),
        grid=(t // SEQ_ROWS,),
        in_specs=[pl.BlockSpec((SEQ_ROWS, N_IN_CHUNKS * w), lambda i: (i, 0))]
                 + [_layer_spec(p, l) for p in layer_params],
        out_specs=pl.BlockSpec((SEQ_ROWS, N_BRANCH * w), lambda i: (i, 0)),
        scratch_shapes=[
            pltpu.VMEM((SEQ_ROWS, 2 * SSM_CHUNK_STATE), F32),
            pltpu.VMEM((SSM_CHUNKS, 2, BATCH, SSM_CHUNK_STATE), F32),
            pltpu.VMEM((SEQ_ROWS + CONF_HIST, w), F32),
            pltpu.VMEM((SEQ_ROWS, w), F32),
            pltpu.VMEM((SEQ_ROWS + SCONV_HIST, w), F32),
        ],
        compiler_params=pltpu.CompilerParams(
            dimension_semantics=("arbitrary",), vmem_limit_bytes=VMEM_LIMIT_BYTES),
        name="seq_mixers",
    )(proj, *layer_params)


def _merge_kernel(x_ref, g_ref, sc_ref, sh_ref, g1_ref, y_ref, wg_ref, bg_ref, wbr_ref, wo_ref, o_ref):
    x = x_ref[...]
    d = x.shape[1]
    w = BRANCH_WIDTH
    hb = _norm_modulate(x, g_ref[...], sc_ref[...], sh_ref[...]).astype(BF16)
    merged = None
    for n in range(N_BRANCH):
        gate = _sigmoid(_dot(hb, wg_ref[:, n * d:(n + 1) * d]) + bg_ref[:, n * d:(n + 1) * d])
        term = gate * _dot(y_ref[:, n * w:(n + 1) * w], wbr_ref[n])
        merged = term if merged is None else merged + term
    out = _dot(merged.astype(BF16), wo_ref[...])
    o_ref[...] = x + _per_batch(out, lambda o3: o3 * g1_ref[...])


def _merge(l, x, g, mod, ycat, w_gate, b_gate, w_branch, w_out):
    t, d = x.shape
    w = BRANCH_WIDTH
    return pl.pallas_call(
        _merge_kernel,
        out_shape=jax.ShapeDtypeStruct((t, d), F32),
        grid=(t // MERGE_ROWS,),
        in_specs=[
            pl.BlockSpec((MERGE_ROWS, d), lambda i: (i, 0)),
            _layer_spec(g, l),
            _layer_spec(mod, l, MOD_SCALE1),
            _layer_spec(mod, l, MOD_SHIFT1),
            _layer_spec(mod, l, MOD_GATE1),
            pl.BlockSpec((MERGE_ROWS, N_BRANCH * w), lambda i: (i, 0)),
            _layer_spec(w_gate, l),
            _layer_spec(b_gate, l),
            _layer_spec(w_branch, l),
            _layer_spec(w_out, l),
        ],
        out_specs=pl.BlockSpec((MERGE_ROWS, d), lambda i: (i, 0)),
        compiler_params=pltpu.CompilerParams(
            dimension_semantics=("arbitrary",), vmem_limit_bytes=VMEM_LIMIT_BYTES),
        name="mixer_merge",
    )(x, g, mod, mod, mod, ycat, w_gate, b_gate, w_branch, w_out)


def _router_kernel(x_ref, g_ref, sc_ref, sh_ref, whi_ref, wlo_ref, b_ref, tri_ref,
                   h_ref, route_ref, cnt_ref, run_ref):
    @pl.when(pl.program_id(0) == 0)
    def _():
        run_ref[...] = jnp.zeros_like(run_ref)

    h = _norm_modulate(x_ref[...], g_ref[...], sc_ref[...], sh_ref[...])
    _store_token_tiles(h_ref, h)
    rows = h.shape[0]
    hhi, hlo = _split_bf16(h)
    contract_last = (((1,), (1,)), ((), ()))
    dott = lambda a, b: lax.dot_general(a, b, contract_last, preferred_element_type=F32)
    logits = dott(whi_ref[...], hhi) + dott(wlo_ref[...], hhi) + dott(whi_ref[...], hlo) + b_ref[...]
    lg = logits[0:N_GROUPS]
    le = logits[SUBLANES:SUBLANES + N_EXPERTS]

    g_row = lax.broadcasted_iota(jnp.int32, (N_GROUPS, rows), 0).astype(F32)
    g_max = jnp.max(lg, axis=0, keepdims=True)
    g_sel = jnp.min(jnp.where(lg == g_max, g_row, float(N_GROUPS)), axis=0, keepdims=True)
    g_w = 1.0 / jnp.sum(jnp.exp(lg - g_max), axis=0, keepdims=True)

    e_row = lax.broadcasted_iota(jnp.int32, (N_EXPERTS, rows), 0).astype(F32)
    e_grp = jnp.floor(e_row * (1.0 / EXPERTS_PER_GROUP))
    neg = float("-inf")
    lm = jnp.where(e_grp == g_sel, le, neg)
    v1 = jnp.max(lm, axis=0, keepdims=True)
    i1 = jnp.min(jnp.where(lm == v1, e_row, float(N_EXPERTS)), axis=0, keepdims=True)
    lm2 = jnp.where(e_row == i1, neg, lm)
    v2 = jnp.max(lm2, axis=0, keepdims=True)
    i2 = jnp.min(jnp.where(lm2 == v2, e_row, float(N_EXPERTS)), axis=0, keepdims=True)
    e21 = jnp.exp(v2 - v1)
    p1 = 1.0 / (1.0 + e21)
    w1 = p1 * g_w
    w2 = e21 * p1 * g_w

    hit1 = e_row == i1
    hit2 = e_row == i2
    onehot = jnp.where(hit1 | hit2, 1.0, 0.0)
    cum = _dot(onehot.astype(BF16), tri_ref[...])
    before = cum - onehot + run_ref[...]
    r1 = jnp.sum(jnp.where(hit1, before, 0.0), axis=0, keepdims=True)
    r2 = jnp.sum(jnp.where(hit2, before, 0.0), axis=0, keepdims=True)
    run_ref[...] = run_ref[...] + jnp.broadcast_to(cum[:, rows - 1:rows], run_ref.shape)
    cnt_ref[...] = run_ref[...]
    zero = jnp.zeros_like(w1)
    route_ref[...] = jnp.concatenate([i1, i2, r1, r2, w1, w2, zero, zero], axis=0)


def _router(l, x, g, mod, w_hi, w_lo, bias, tri):
    t, d = x.shape
    return pl.pallas_call(
        _router_kernel,
        out_shape=(
            jax.ShapeDtypeStruct((t * SUBLANES, LANES), F32),
            jax.ShapeDtypeStruct((SUBLANES, t), F32),
            jax.ShapeDtypeStruct((N_EXPERTS, ROUTE_ROWS), F32),
        ),
        grid=(t // ROUTE_ROWS,),
        in_specs=[
            pl.BlockSpec((ROUTE_ROWS, d), lambda i: (i, 0)),
            _layer_spec(g, l),
            _layer_spec(mod, l, MOD_SCALE2),
            _layer_spec(mod, l, MOD_SHIFT2),
            _layer_spec(w_hi, l),
            _layer_spec(w_lo, l),
            _layer_spec(bias, l),
            pl.BlockSpec((ROUTE_ROWS, ROUTE_ROWS), lambda i: (0, 0)),
        ],
        out_specs=(
            pl.BlockSpec((ROUTE_ROWS * SUBLANES, LANES), lambda i: (i, 0)),
            pl.BlockSpec((SUBLANES, ROUTE_ROWS), lambda i: (0, i)),
            pl.BlockSpec((N_EXPERTS, ROUTE_ROWS), lambda i: (0, 0)),
        ),
        scratch_shapes=[pltpu.VMEM((N_EXPERTS, ROUTE_ROWS), F32)],
        compiler_params=pltpu.CompilerParams(
            dimension_semantics=("arbitrary",), vmem_limit_bytes=VMEM_LIMIT_BYTES),
        name="moe_router",
    )(x, g, mod, mod, w_hi, w_lo, bias, tri)


def _row_copy(src_hbm, src_token, dst_hbm, dst_token, sem):
    return pltpu.make_async_copy(_token_tile(src_hbm, src_token), _token_tile(dst_hbm, dst_token), sem)


def _dispatch_kernel(zstart_ref, dest_hbm, h_ref, zeros_hbm, xs_hbm, idx_smem, idx_sem, zero_sem, row_sem):
    i = pl.program_id(0)
    n_idx = idx_smem.shape[0]
    tile = n_idx // TOP_K
    idx_copy = pltpu.make_async_copy(dest_hbm.at[pl.ds(i * n_idx, n_idx)], idx_smem, idx_sem)
    idx_copy.start()

    def zero_copy(e):
        block_rows = EXPERT_ROWS * SUBLANES
        start = pl.multiple_of(zstart_ref[e] * SUBLANES, block_rows)
        return pltpu.make_async_copy(zeros_hbm, xs_hbm.at[pl.ds(start, block_rows)], zero_sem)

    @pl.when(i == 0)
    def _():
        def start(e, carry):
            @pl.when(zstart_ref[e] >= 0)
            def _():
                zero_copy(e).start()
            return carry

        def wait(e, carry):
            @pl.when(zstart_ref[e] >= 0)
            def _():
                zero_copy(e).wait()
            return carry

        lax.fori_loop(0, N_EXPERTS, start, 0)
        lax.fori_loop(0, N_EXPERTS, wait, 0)

    idx_copy.wait()

    def start_rows(t, carry):
        for k in range(TOP_K):
            _row_copy(h_ref, t, xs_hbm, idx_smem[k * tile + t], row_sem).start(priority=k)
        return carry

    def wait_rows(t, carry):
        for k in range(TOP_K):
            _row_copy(h_ref, 0, xs_hbm, 0, row_sem).wait()
        return carry

    lax.fori_loop(0, tile, start_rows, 0, unroll=8)
    lax.fori_loop(0, tile, wait_rows, 0, unroll=8)


def _dispatch(zstart, dest_tiles, h, cap):
    t = h.shape[0] // SUBLANES
    zeros = jnp.zeros((EXPERT_ROWS * SUBLANES, LANES), h.dtype)
    any_spec = pl.BlockSpec(memory_space=pl.ANY)
    return pl.pallas_call(
        _dispatch_kernel,
        out_shape=jax.ShapeDtypeStruct((cap * SUBLANES, LANES), h.dtype),
        grid_spec=pltpu.PrefetchScalarGridSpec(
            num_scalar_prefetch=1,
            grid=(t // DISPATCH_ROWS,),
            in_specs=[any_spec,
                      pl.BlockSpec((DISPATCH_ROWS * SUBLANES, LANES), lambda i, zs: (i, 0)),
                      any_spec],
            out_specs=any_spec,
            scratch_shapes=[
                pltpu.SMEM((TOP_K * DISPATCH_ROWS,), jnp.int32),
                pltpu.SemaphoreType.DMA,
                pltpu.SemaphoreType.DMA,
                pltpu.SemaphoreType.DMA,
            ],
        ),
        compiler_params=pltpu.CompilerParams(
            dimension_semantics=("arbitrary",), has_side_effects=True),
        name="moe_dispatch",
    )(zstart, dest_tiles, h, zeros)


def _expert_kernel(be_ref, nv_ref, x_ref, w1_ref, w3_ref, w2_ref, o_ref):
    @pl.when(pl.program_id(0) < nv_ref[0])
    def _():
        xb = _load_token_tiles(x_ref, 0, EXPERT_ROWS).astype(BF16)
        a = _dot(xb, w1_ref[...])
        hid = a * _sigmoid(a) * _dot(xb, w3_ref[...])
        _store_token_tiles(o_ref, _dot(hid.astype(BF16), w2_ref[...]))


def _experts(l, block_e, n_valid, xs, w1, w3, w2):
    cap = xs.shape[0] // SUBLANES
    _, _, d, f = w1.shape
    last = lambda i, nv: jnp.minimum(i, nv[0] - 1)
    tile_rows = EXPERT_ROWS * SUBLANES
    return pl.pallas_call(
        _expert_kernel,
        out_shape=jax.ShapeDtypeStruct(xs.shape, F32),
        grid_spec=pltpu.PrefetchScalarGridSpec(
            num_scalar_prefetch=2,
            grid=(cap // EXPERT_ROWS,),
            in_specs=[
                pl.BlockSpec((tile_rows, LANES), lambda i, be, nv: (last(i, nv), 0)),
                pl.BlockSpec((None, None, d, f), lambda i, be, nv: (l, be[last(i, nv)], 0, 0)),
                pl.BlockSpec((None, None, d, f), lambda i, be, nv: (l, be[last(i, nv)], 0, 0)),
                pl.BlockSpec((None, None, f, d), lambda i, be, nv: (l, be[last(i, nv)], 0, 0)),
            ],
            out_specs=pl.BlockSpec((tile_rows, LANES), lambda i, be, nv: (last(i, nv), 0)),
        ),
        compiler_params=pltpu.CompilerParams(
            dimension_semantics=("arbitrary",), vmem_limit_bytes=VMEM_LIMIT_BYTES),
        name="moe_experts",
    )(block_e, n_valid, xs, w1, w3, w2)


def _combine_kernel(final_norm, dest_hbm, ys_hbm, x_ref, g2_ref, wt_ref, fg_ref, o_ref,
                    idx_smem, ybuf_ref, idx_sem, row_sem):
    i = pl.program_id(0)
    n_tiles = pl.num_programs(0)
    n_idx = idx_smem.shape[1]
    tile = n_idx // TOP_K
    slot = lax.rem(i, 2)
    other = 1 - slot

    def idx_copy(tile_index, s):
        src = dest_hbm.at[pl.ds(pl.multiple_of(tile_index * n_idx, n_idx), n_idx)]
        return pltpu.make_async_copy(src, idx_smem.at[s], idx_sem.at[s])

    def start_rows(s):
        def body(t, carry):
            for k in range(TOP_K):
                row = k * tile + t
                pltpu.make_async_copy(_token_tile(ys_hbm, idx_smem[s, row]),
                                      _token_tile(ybuf_ref.at[s], row), row_sem.at[s]).start(priority=k)
            return carry
        lax.fori_loop(0, tile, body, 0, unroll=8)

    def wait_rows(s):
        pltpu.make_async_copy(ybuf_ref.at[s], ybuf_ref.at[s], row_sem.at[s]).wait()

    @pl.when(i == 0)
    def _():
        idx_copy(0, 0).start()
        idx_copy(0, 0).wait()
        start_rows(0)

        @pl.when(n_tiles > 1)
        def _():
            idx_copy(1, 1).start()

    @pl.when(i + 1 < n_tiles)
    def _():
        idx_copy(i + 1, other).wait()
        start_rows(other)

    @pl.when(i + 2 < n_tiles)
    def _():
        idx_copy(i + 2, slot).start()

    wait_rows(slot)
    wt = wt_ref[...]
    ytiles = ybuf_ref.at[slot]
    y = (wt[:, 0:1] * _load_token_tiles(ytiles, 0, tile)
         + wt[:, 1:2] * _load_token_tiles(ytiles, tile, tile))
    out = x_ref[...] + _per_batch(y, lambda y3: y3 * g2_ref[...])
    if final_norm:
        ms = jnp.mean(out * out, axis=-1, keepdims=True)
        out = out * lax.rsqrt(ms + NORM_EPS) * fg_ref[...]
    o_ref[...] = out


def _combine(l, dest_tiles, ys, x, mod, wt, final_g, final_norm):
    t, d = x.shape
    any_spec = pl.BlockSpec(memory_space=pl.ANY)
    return pl.pallas_call(
        functools.partial(_combine_kernel, final_norm),
        out_shape=jax.ShapeDtypeStruct((t, d), F32),
        grid=(t // COMBINE_ROWS,),
        in_specs=[
            any_spec,
            any_spec,
            pl.BlockSpec((COMBINE_ROWS, d), lambda i: (i, 0)),
            _layer_spec(mod, l, MOD_GATE2),
            pl.BlockSpec((COMBINE_ROWS, TOP_K), lambda i: (i, 0)),
            pl.BlockSpec((1, d), lambda i: (0, 0)),
        ],
        out_specs=pl.BlockSpec((COMBINE_ROWS, d), lambda i: (i, 0)),
        scratch_shapes=[
            pltpu.SMEM((2, TOP_K * COMBINE_ROWS), jnp.int32),
            pltpu.VMEM((2, TOP_K * COMBINE_ROWS * SUBLANES, LANES), F32),
            pltpu.SemaphoreType.DMA((2,)),
            pltpu.SemaphoreType.DMA((2,)),
        ],
        compiler_params=pltpu.CompilerParams(
            dimension_semantics=("arbitrary",), vmem_limit_bytes=VMEM_LIMIT_BYTES),
        name="moe_combine",
    )(dest_tiles, ys, x, mod, wt, final_g)


def _ssm_params(lam_re, lam_im, log_step, b_re, b_im, c_re, c_im):
    step = jnp.exp(log_step)[:, None]
    mag = jnp.exp(lam_re * step)
    a_re, a_im = mag * jnp.cos(lam_im * step), mag * jnp.sin(lam_im * step)
    den = lam_re * lam_re + lam_im * lam_im
    n_re, n_im = a_re - 1.0, a_im
    k_re = (n_re * lam_re + n_im * lam_im) / den
    k_im = (n_im * lam_re - n_re * lam_im) / den
    bb_re = k_re[..., None] * b_re - k_im[..., None] * b_im
    bb_im = k_re[..., None] * b_im + k_im[..., None] * b_re
    gpc = SSM_GROUPS // SSM_CHUNKS
    eye = jnp.eye(gpc, dtype=F32)

    def in_blocks(bb):
        bb = bb.reshape(SSM_CHUNKS, gpc, SSM_STATE, SSM_GROUP)
        return jnp.einsum("cgph,gk->cghkp", bb, eye).reshape(SSM_CHUNKS, SSM_CHUNK_IN, SSM_CHUNK_STATE)

    def out_blocks(cc):
        cc = cc.reshape(SSM_CHUNKS, gpc, SSM_GROUP, SSM_STATE)
        return jnp.einsum("cghp,gk->cgpkh", cc, eye).reshape(SSM_CHUNKS, SSM_CHUNK_STATE, SSM_CHUNK_IN)

    wb = jnp.concatenate([in_blocks(bb_re), in_blocks(bb_im)], axis=2).astype(BF16)
    wc = jnp.concatenate([out_blocks(c_re), out_blocks(-c_im)], axis=1).astype(BF16)
    bcast = lambda a: jnp.broadcast_to(a.reshape(SSM_CHUNKS, 1, SSM_CHUNK_STATE),
                                       (SSM_CHUNKS, BATCH, SSM_CHUNK_STATE))
    return wb, wc, bcast(a_re), bcast(a_im)


def _rows8(w):
    depth, k, c = w.shape
    return jnp.broadcast_to(w[:, :, None, :], (depth, k, SUBLANES, c)).reshape(depth, k * SUBLANES, c)


def _router_weights(w_rg, b_rg, w_re, b_re):
    depth, d, _ = w_rg.shape
    pad = SUBLANES - N_GROUPS
    w_t = jnp.concatenate([w_rg.transpose(0, 2, 1), jnp.zeros((depth, pad, d), F32),
                           w_re.transpose(0, 2, 1)], axis=1)
    b = jnp.concatenate([b_rg, jnp.zeros((depth, pad), F32), b_re], axis=1)
    hi, lo = _split_bf16(w_t)
    return hi, lo, jnp.broadcast_to(b[:, :, None], (depth, ROUTER_ROWS, ROUTE_ROWS))


def _routing_tables(route, counts):
    t = route.shape[1]
    counts = counts.astype(jnp.int32)
    padded = (counts + EXPERT_ROWS - 1) // EXPERT_ROWS * EXPERT_ROWS
    pad_end = jnp.cumsum(padded)
    pad_start = pad_end - padded
    experts = route[0:TOP_K].astype(jnp.int32)
    ranks = route[TOP_K:2 * TOP_K].astype(jnp.int32)
    expert_ids = jnp.arange(N_EXPERTS, dtype=jnp.int32)
    is_expert = experts[:, :, None] == expert_ids
    dest = jnp.sum(jnp.where(is_expert, pad_start, 0), axis=-1) + ranks

    def tiles(rows_per_tile):
        n = t // rows_per_tile
        return dest.reshape(TOP_K, n, rows_per_tile).transpose(1, 0, 2).reshape(-1)

    n_blocks = (TOP_K * t) // EXPERT_ROWS + N_EXPERTS
    block_start = jnp.arange(n_blocks, dtype=jnp.int32) * EXPERT_ROWS
    segments_done = jnp.sum(pad_end[None, :] <= block_start[:, None], axis=1)
    block_e = jnp.minimum(segments_done, N_EXPERTS - 1)
    n_valid = (pad_end[-1] // EXPERT_ROWS).reshape(1)
    zstart = jnp.where(counts > 0, pad_end - EXPERT_ROWS, -1)
    weights = route[2 * TOP_K:3 * TOP_K].T
    return (tiles(DISPATCH_ROWS), tiles(COMBINE_ROWS), block_e.astype(jnp.int32),
            n_valid.astype(jnp.int32), zstart.astype(jnp.int32), weights, n_blocks * EXPERT_ROWS)


def kernel(x, c, norm_mix_g, norm_ffn_g, w_ada, b_ada, w_in, lam_re, lam_im, log_step, ssm_b_re, ssm_b_im, ssm_c_re, ssm_c_im, ssm_d, w_glu, conf_dw_w, conf_dw_b, conf_ln_g, conf_ln_b, sconv_w, w_branch, w_gate, b_gate, w_out, w_router_group, b_router_group, w_router_expert, b_router_expert, w_exp_gate, w_exp_up, w_exp_down, final_norm_g):
    bsz, seq, d = x.shape
    assert bsz == BATCH and d == D_MODEL
    depth = w_in.shape[0]
    t = bsz * seq
    rows = lambda v: v.reshape(depth, 1, -1)

    mix_g, ffn_g = rows(norm_mix_g), rows(norm_ffn_g)
    seq_params = (*jax.vmap(_ssm_params)(lam_re, lam_im, log_step, ssm_b_re, ssm_b_im, ssm_c_re, ssm_c_im),
                  rows(ssm_d), w_glu.astype(BF16), _rows8(conf_dw_w), rows(conf_dw_b), rows(conf_ln_g),
                  rows(conf_ln_b), _rows8(sconv_w))
    w_in_b, w_gate_b, w_branch_b, w_out_b = (w.astype(BF16) for w in (w_in, w_gate, w_branch, w_out))
    w1_b, w3_b, w2_b = (w.astype(BF16) for w in (w_exp_gate, w_exp_up, w_exp_down))
    r_hi, r_lo, r_b = _router_weights(w_router_group, b_router_group, w_router_expert, b_router_expert)
    tri = jnp.triu(jnp.ones((ROUTE_ROWS, ROUTE_ROWS), BF16))

    mod = _ada(c, w_ada, b_ada)
    xt = jnp.transpose(x, (1, 0, 2)).reshape(t, d)

    for l in range(depth):
        proj = _inproj(l, xt, mix_g, mod, w_in_b)
        ycat = _seq_mixers(l, proj, *seq_params)
        xt = _merge(l, xt, mix_g, mod, ycat, w_gate_b, rows(b_gate), w_branch_b, w_out_b)

        h2, route, counts = _router(l, xt, ffn_g, mod, r_hi, r_lo, r_b, tri)
        dest_d, dest_c, block_e, n_valid, zstart, weights, cap = _routing_tables(route, counts[:, 0])
        xs = _dispatch(zstart, dest_d, h2, cap)
        ys = _experts(l, block_e, n_valid, xs, w1_b, w3_b, w2_b)
        xt = _combine(l, dest_c, ys, xt, mod, weights, final_norm_g.reshape(1, d), l == depth - 1)

    return jnp.transpose(xt.reshape(seq, bsz, d), (1, 0, 2))
```

```python
import functools
import math

import jax
import jax.numpy as jnp
from jax import lax
from jax.experimental import pallas as pl
from jax.experimental.pallas import tpu as pltpu

F32 = jnp.float32
BF16 = jnp.bfloat16

D_MODEL = 1024
BATCH = 8
BRANCH_WIDTH = D_MODEL // 2
N_BRANCH = 3
N_IN_CHUNKS = 6
SSM_GROUP = 16
SSM_GROUPS = BRANCH_WIDTH // SSM_GROUP
SSM_STATE = 64
CONF_KERNEL = 31
SCONV_KERNEL = 3
N_GROUPS = 4
EXPERTS_PER_GROUP = 8
N_EXPERTS = N_GROUPS * EXPERTS_PER_GROUP
TOP_K = 2
D_FF_EXPERT = D_MODEL // 2
NORM_EPS = 1e-6

SUBLANES = 8
LANES = 128
VMEM_LIMIT_BYTES = 56 * 1024 * 1024

SSM_CHUNKS = 2
SSM_CHUNK_IN = BRANCH_WIDTH // SSM_CHUNKS
SSM_CHUNK_STATE = SSM_GROUPS * SSM_STATE // SSM_CHUNKS
CONF_HIST = (CONF_KERNEL - 1) * BATCH
SCONV_HIST = (SCONV_KERNEL - 1) * BATCH
ROUTER_ROWS = 40
MOD_SHIFT1, MOD_SCALE1, MOD_GATE1, MOD_SHIFT2, MOD_SCALE2, MOD_GATE2 = range(6)

ADA_COLS = 1536
INPROJ_ROWS = 1024
SEQ_ROWS = 512
CONV_ROWS = 64
MERGE_ROWS = 512
ROUTE_ROWS = 512
DISPATCH_ROWS = 512
EXPERT_ROWS = 512
ZERO_ROWS = 64
COMBINE_ROWS = 256


def _dot(a, b):
    return jnp.dot(a, b, preferred_element_type=F32)


def _split_bf16(v):
    hi = v.astype(BF16)
    lo = (v - hi.astype(F32)).astype(BF16)
    return hi, lo


def _sigmoid(v):
    return 1.0 / (1.0 + jnp.exp(-v))


def _gelu_tanh(v):
    return 0.5 * v * (1.0 + jnp.tanh(math.sqrt(2.0 / math.pi) * (v + 0.044715 * (v * v * v))))


def _per_batch(v, fn):
    rows, d = v.shape
    return fn(v.reshape(rows // BATCH, BATCH, d)).reshape(rows, d)


def _load_token_tiles(ref, first_token, n_tokens):
    base = first_token * SUBLANES
    chunks = [ref[pl.ds(base + s, n_tokens, stride=SUBLANES), :] for s in range(SUBLANES)]
    return jnp.concatenate(chunks, axis=1)


def _store_token_tiles(ref, value):
    for s in range(SUBLANES):
        ref[pl.ds(s, value.shape[0], stride=SUBLANES), :] = value[:, s * LANES:(s + 1) * LANES]


def _token_tile(ref, token):
    return ref.at[pl.ds(pl.multiple_of(token * SUBLANES, SUBLANES), SUBLANES)]


def _layer_spec(arr, l, col_block=None):
    tail = arr.shape[1:]
    if col_block is None:
        zeros = (0,) * len(tail)
        return pl.BlockSpec((None,) + tail, lambda *_: (l,) + zeros)
    return pl.BlockSpec((None, tail[0], D_MODEL), lambda *_: (l, 0, col_block))


def _norm_modulate(x, g, scale, shift):
    ms = jnp.mean(x * x, axis=-1, keepdims=True)
    h = x * lax.rsqrt(ms + NORM_EPS) * g
    return _per_batch(h, lambda h3: h3 * (1.0 + scale) + shift)


def _ada_kernel(c_ref, w_ref, b_ref, o_ref):
    c = c_ref[...]
    cond = c * _sigmoid(c)
    chi, clo = _split_bf16(cond)
    whi, wlo = _split_bf16(w_ref[...])
    o_ref[...] = _dot(chi, whi) + _dot(chi, wlo) + _dot(clo, whi) + b_ref[...]


def _ada(c, w_ada, b_ada):
    depth, d, n = w_ada.shape
    return pl.pallas_call(
        _ada_kernel,
        out_shape=jax.ShapeDtypeStruct((depth, BATCH, n), F32),
        grid=(depth, n // ADA_COLS),
        in_specs=[
            pl.BlockSpec((BATCH, d), lambda l, j: (0, 0)),
            pl.BlockSpec((None, d, ADA_COLS), lambda l, j: (l, 0, j)),
            pl.BlockSpec((None, 1, ADA_COLS), lambda l, j: (l, 0, j)),
        ],
        out_specs=pl.BlockSpec((None, BATCH, ADA_COLS), lambda l, j: (l, 0, j)),
        compiler_params=pltpu.CompilerParams(
            dimension_semantics=("arbitrary", "arbitrary"), vmem_limit_bytes=VMEM_LIMIT_BYTES),
        name="ada_modulation",
    )(c, w_ada, b_ada.reshape(depth, 1, n))


def _inproj_kernel(x_ref, g_ref, sc_ref, sh_ref, w_ref, o_ref):
    h = _norm_modulate(x_ref[...], g_ref[...], sc_ref[...], sh_ref[...])
    o_ref[...] = _dot(h.astype(BF16), w_ref[...]).astype(o_ref.dtype)


def _inproj(l, x, g, mod, w_in):
    t, d = x.shape
    n = w_in.shape[2]
    return pl.pallas_call(
        _inproj_kernel,
        out_shape=jax.ShapeDtypeStruct((t, n), BF16),
        grid=(t // INPROJ_ROWS,),
        in_specs=[
            pl.BlockSpec((INPROJ_ROWS, d), lambda i: (i, 0)),
            _layer_spec(g, l),
            _layer_spec(mod, l, MOD_SCALE1),
            _layer_spec(mod, l, MOD_SHIFT1),
            _layer_spec(w_in, l),
        ],
        out_specs=pl.BlockSpec((INPROJ_ROWS, n), lambda i: (i, 0)),
        compiler_params=pltpu.CompilerParams(
            dimension_semantics=("arbitrary",), vmem_limit_bytes=VMEM_LIMIT_BYTES),
        name="mixer_inproj",
    )(x, g, mod, mod, w_in)


def _seq_kernel(p_ref, wb_ref, wc_ref, are_ref, aim_ref, d_ref, wglu_ref, cw_ref, cb_ref,
                lng_ref, lnb_ref, sw_ref, o_ref, bu_ref, st_ref, zbuf_ref, cbuf_ref, qbuf_ref):
    rows = p_ref.shape[0]
    steps = rows // BATCH
    w = BRANCH_WIDTH

    @pl.when(pl.program_id(0) == 0)
    def _():
        st_ref[...] = jnp.zeros_like(st_ref)
        zbuf_ref[0:CONF_HIST, :] = jnp.zeros((CONF_HIST, w), F32)
        qbuf_ref[0:SCONV_HIST, :] = jnp.zeros((SCONV_HIST, w), F32)

    gate_b = p_ref[:, 3 * w:4 * w].astype(F32)
    gate_c = p_ref[:, 4 * w:5 * w].astype(F32)
    hv = p_ref[:, 5 * w:6 * w].astype(F32)
    qbuf_ref[SCONV_HIST:SCONV_HIST + rows, :] = gate_c * hv
    q3 = qbuf_ref[...].reshape((rows + SCONV_HIST) // SUBLANES, SUBLANES, w)
    acc = q3[0:steps] * sw_ref[0:SUBLANES, :]
    for k in range(1, SCONV_KERNEL):
        acc = acc + q3[k:k + steps] * sw_ref[k * SUBLANES:(k + 1) * SUBLANES, :]
    qbuf_ref[0:SCONV_HIST, :] = qbuf_ref[rows:rows + SCONV_HIST, :]
    o_ref[:, 2 * w:3 * w] = (gate_b * acc.reshape(rows, w)).astype(o_ref.dtype)

    v = p_ref[:, w:2 * w].astype(F32)
    g = p_ref[:, 2 * w:3 * w].astype(F32)
    zbuf_ref[CONF_HIST:CONF_HIST + rows, :] = v * _sigmoid(g)

    ys = []
    for c in range(SSM_CHUNKS):
        u_c = p_ref[:, c * SSM_CHUNK_IN:(c + 1) * SSM_CHUNK_IN]
        bu_ref[...] = _dot(u_c, wb_ref[c])
        a_re = are_ref[c]
        a_im = aim_ref[c]

        def step(t, carry):
            x_re, x_im = carry
            r0 = pl.multiple_of(t * BATCH, BATCH)
            n_re = a_re * x_re - a_im * x_im + bu_ref[pl.ds(r0, BATCH), 0:SSM_CHUNK_STATE]
            n_im = a_re * x_im + a_im * x_re + bu_ref[pl.ds(r0, BATCH), SSM_CHUNK_STATE:]
            bu_ref[pl.ds(r0, BATCH), 0:SSM_CHUNK_STATE] = n_re
            bu_ref[pl.ds(r0, BATCH), SSM_CHUNK_STATE:] = n_im
            return n_re, n_im

        x_re, x_im = lax.fori_loop(0, steps, step, (st_ref[c, 0], st_ref[c, 1]), unroll=4)
        st_ref[c, 0] = x_re
        st_ref[c, 1] = x_im
        ys.append(_dot(bu_ref[...].astype(BF16), wc_ref[c]))
    u = p_ref[:, 0:w].astype(F32)
    y = _gelu_tanh(jnp.concatenate(ys, axis=1) + d_ref[...] * u)
    y_ssm = y * _sigmoid(_dot(y.astype(BF16), wglu_ref[...]))
    o_ref[:, 0:w] = y_ssm.astype(o_ref.dtype)

    win_groups = (CONV_ROWS + CONF_HIST) // SUBLANES
    out_groups = CONV_ROWS // SUBLANES

    def conv_chunk(j, carry):
        r0 = pl.multiple_of(j * CONV_ROWS, CONV_ROWS)
        for s in range(w // LANES):
            ls = slice(s * LANES, (s + 1) * LANES)
            win = zbuf_ref[pl.ds(r0, CONV_ROWS + CONF_HIST), ls].reshape(win_groups, SUBLANES, LANES)
            acc = win[0:out_groups] * cw_ref[0:SUBLANES, ls]
            for k in range(1, CONF_KERNEL):
                acc = acc + win[k:k + out_groups] * cw_ref[k * SUBLANES:(k + 1) * SUBLANES, ls]
            cbuf_ref[pl.ds(r0, CONV_ROWS), ls] = acc.reshape(CONV_ROWS, LANES)
        return carry

    lax.fori_loop(0, rows // CONV_ROWS, conv_chunk, 0)
    zbuf_ref[0:CONF_HIST, :] = zbuf_ref[rows:rows + CONF_HIST, :]
    cv = cbuf_ref[...] + cb_ref[...]
    cc = cv - jnp.mean(cv, axis=-1, keepdims=True)
    ln = cc * lax.rsqrt(jnp.mean(cc * cc, axis=-1, keepdims=True) + NORM_EPS) * lng_ref[...] + lnb_ref[...]
    o_ref[:, w:2 * w] = (ln * _sigmoid(ln)).astype(o_ref.dtype)


def _seq_mixers(l, proj, *layer_params):
    t = proj.shape[0]
    w = BRANCH_WIDTH
    return pl.pallas_call(
        _seq_kernel,
        out_shape=jax.ShapeDtypeStruct((t, N_BRANCH * w), BF16),
        grid=(t // SEQ_ROWS,),
        in_specs=[pl.BlockSpec((SEQ_ROWS, N_IN_CHUNKS * w), lambda i: (i, 0))]
                 + [_layer_spec(p, l) for p in layer_params],
        out_specs=pl.BlockSpec((SEQ_ROWS, N_BRANCH * w), lambda i: (i, 0)),
        scratch_shapes=[
            pltpu.VMEM((SEQ_ROWS, 2 * SSM_CHUNK_STATE), F32),
            pltpu.VMEM((SSM_CHUNKS, 2, BATCH, SSM_CHUNK_STATE), F32),
            pltpu.VMEM((SEQ_ROWS + CONF_HIST, w), F32),
            pltpu.VMEM((SEQ_ROWS, w), F32),
            pltpu.VMEM((SEQ_ROWS + SCONV_HIST, w), F32),
        ],
        compiler_params=pltpu.CompilerParams(
            dimension_semantics=("arbitrary",), vmem_limit_bytes=VMEM_LIMIT_BYTES),
        name="seq_mixers",
    )(proj, *layer_params)


def _merge_kernel(x_ref, g_ref, sc_ref, sh_ref, g1_ref, y_ref, wg_ref, bg_ref, wbr_ref, wo_ref, o_ref):
    x = x_ref[...]
    d = x.shape[1]
    w = BRANCH_WIDTH
    hb = _norm_modulate(x, g_ref[...], sc_ref[...], sh_ref[...]).astype(BF16)
    merged = None
    for n in range(N_BRANCH):
        gate = _sigmoid(_dot(hb, wg_ref[:, n * d:(n + 1) * d]) + bg_ref[:, n * d:(n + 1) * d])
        term = gate * _dot(y_ref[:, n * w:(n + 1) * w], wbr_ref[n])
        merged = term if merged is None else merged + term
    out = _dot(merged.astype(BF16), wo_ref[...])
    o_ref[...] = x + _per_batch(out, lambda o3: o3 * g1_ref[...])


def _merge(l, x, g, mod, ycat, w_gate, b_gate, w_branch, w_out):
    t, d = x.shape
    w = BRANCH_WIDTH
    return pl.pallas_call(
        _merge_kernel,
        out_shape=jax.ShapeDtypeStruct((t, d), F32),
        grid=(t // MERGE_ROWS,),
        in_specs=[
            pl.BlockSpec((MERGE_ROWS, d), lambda i: (i, 0)),
            _layer_spec(g, l),
            _layer_spec(mod, l, MOD_SCALE1),
            _layer_spec(mod, l, MOD_SHIFT1),
            _layer_spec(mod, l, MOD_GATE1),
            pl.BlockSpec((MERGE_ROWS, N_BRANCH * w), lambda i: (i, 0)),
            _layer_spec(w_gate, l),
            _layer_spec(b_gate, l),
            _layer_spec(w_branch, l),
            _layer_spec(w_out, l),
        ],
        out_specs=pl.BlockSpec((MERGE_ROWS, d), lambda i: (i, 0)),
        compiler_params=pltpu.CompilerParams(
            dimension_semantics=("arbitrary",), vmem_limit_bytes=VMEM_LIMIT_BYTES),
        name="mixer_merge",
    )(x, g, mod, mod, mod, ycat, w_gate, b_gate, w_branch, w_out)


def _router_kernel(x_ref, g_ref, sc_ref, sh_ref, whi_ref, wlo_ref, b_ref, tri_ref,
                   h_ref, route_ref, cnt_ref, run_ref):
    @pl.when(pl.program_id(0) == 0)
    def _():
        run_ref[...] = jnp.zeros_like(run_ref)

    h = _norm_modulate(x_ref[...], g_ref[...], sc_ref[...], sh_ref[...])
    _store_token_tiles(h_ref, h)
    rows = h.shape[0]
    hhi, hlo = _split_bf16(h)
    contract_last = (((1,), (1,)), ((), ()))
    dott = lambda a, b: lax.dot_general(a, b, contract_last, preferred_element_type=F32)
    logits = dott(whi_ref[...], hhi) + dott(wlo_ref[...], hhi) + dott(whi_ref[...], hlo) + b_ref[...]
    lg = logits[0:N_GROUPS]
    le = logits[SUBLANES:SUBLANES + N_EXPERTS]

    g_row = lax.broadcasted_iota(jnp.int32, (N_GROUPS, rows), 0).astype(F32)
    g_max = jnp.max(lg, axis=0, keepdims=True)
    g_sel = jnp.min(jnp.where(lg == g_max, g_row, float(N_GROUPS)), axis=0, keepdims=True)
    g_w = 1.0 / jnp.sum(jnp.exp(lg - g_max), axis=0, keepdims=True)

    e_row = lax.broadcasted_iota(jnp.int32, (N_EXPERTS, rows), 0).astype(F32)
    e_grp = jnp.floor(e_row * (1.0 / EXPERTS_PER_GROUP))
    neg = float("-inf")
    lm = jnp.where(e_grp == g_sel, le, neg)
    v1 = jnp.max(lm, axis=0, keepdims=True)
    i1 = jnp.min(jnp.where(lm == v1, e_row, float(N_EXPERTS)), axis=0, keepdims=True)
    lm2 = jnp.where(e_row == i1, neg, lm)
    v2 = jnp.max(lm2, axis=0, keepdims=True)
    i2 = jnp.min(jnp.where(lm2 == v2, e_row, float(N_EXPERTS)), axis=0, keepdims=True)
    e21 = jnp.exp(v2 - v1)
    p1 = 1.0 / (1.0 + e21)
    w1 = p1 * g_w
    w2 = e21 * p1 * g_w

    hit1 = e_row == i1
    hit2 = e_row == i2
    onehot = jnp.where(hit1 | hit2, 1.0, 0.0)
    cum = _dot(onehot.astype(BF16), tri_ref[...])
    before = cum - onehot + run_ref[...]
    r1 = jnp.sum(jnp.where(hit1, before, 0.0), axis=0, keepdims=True)
    r2 = jnp.sum(jnp.where(hit2, before, 0.0), axis=0, keepdims=True)
    run_ref[...] = run_ref[...] + jnp.broadcast_to(cum[:, rows - 1:rows], run_ref.shape)
    cnt_ref[...] = run_ref[...]
    zero = jnp.zeros_like(w1)
    route_ref[...] = jnp.concatenate([i1, i2, r1, r2, w1, w2, zero, zero], axis=0)


def _router(l, x, g, mod, w_hi, w_lo, bias, tri):
    t, d = x.shape
    return pl.pallas_call(
        _router_kernel,
        out_shape=(
            jax.ShapeDtypeStruct((t * SUBLANES, LANES), F32),
            jax.ShapeDtypeStruct((SUBLANES, t), F32),
            jax.ShapeDtypeStruct((N_EXPERTS, ROUTE_ROWS), F32),
        ),
        grid=(t // ROUTE_ROWS,),
        in_specs=[
            pl.BlockSpec((ROUTE_ROWS, d), lambda i: (i, 0)),
            _layer_spec(g, l),
            _layer_spec(mod, l, MOD_SCALE2),
            _layer_spec(mod, l, MOD_SHIFT2),
            _layer_spec(w_hi, l),
            _layer_spec(w_lo, l),
            _layer_spec(bias, l),
            pl.BlockSpec((ROUTE_ROWS, ROUTE_ROWS), lambda i: (0, 0)),
        ],
        out_specs=(
            pl.BlockSpec((ROUTE_ROWS * SUBLANES, LANES), lambda i: (i, 0)),
            pl.BlockSpec((SUBLANES, ROUTE_ROWS), lambda i: (0, i)),
            pl.BlockSpec((N_EXPERTS, ROUTE_ROWS), lambda i: (0, 0)),
        ),
        scratch_shapes=[pltpu.VMEM((N_EXPERTS, ROUTE_ROWS), F32)],
        compiler_params=pltpu.CompilerParams(
            dimension_semantics=("arbitrary",), vmem_limit_bytes=VMEM_LIMIT_BYTES),
        name="moe_router",
    )(x, g, mod, mod, w_hi, w_lo, bias, tri)


def _row_copy(src_hbm, src_token, dst_hbm, dst_token, sem):
    return pltpu.make_async_copy(_token_tile(src_hbm, src_token), _token_tile(dst_hbm, dst_token), sem)


def _dispatch_kernel(zstart_ref, dest_hbm, h_ref, xs_hbm, idx0_smem, idx1_smem, zeros_ref,
                     idx_sem, zero_sem, row_sem):
    i = pl.program_id(0)
    n_tiles = pl.num_programs(0)
    idx_smem = (idx0_smem, idx1_smem)
    n_idx = idx0_smem.shape[0]
    tile = n_idx // TOP_K

    def idx_copy(tile_index, s):
        src = dest_hbm.at[pl.ds(pl.multiple_of(tile_index * n_idx, n_idx), n_idx)]
        return pltpu.make_async_copy(src, idx_smem[s], idx_sem.at[s])

    @pl.when(i == 0)
    def _():
        idx_copy(0, 0).start()

    def zero_copy(j):
        piece_rows = ZERO_ROWS * SUBLANES
        start = pl.multiple_of(zstart_ref[j] * SUBLANES, piece_rows)
        return pltpu.make_async_copy(zeros_ref, xs_hbm.at[pl.ds(start, piece_rows)], zero_sem)

    @pl.when(i == 0)
    def _():
        zeros_ref[...] = jnp.zeros_like(zeros_ref)

        def start(j, carry):
            @pl.when(zstart_ref[j] >= 0)
            def _():
                zero_copy(j).start()
            return carry

        def wait(j, carry):
            @pl.when(zstart_ref[j] >= 0)
            def _():
                zero_copy(j).wait()
            return carry

        lax.fori_loop(0, zstart_ref.shape[0], start, 0)
        lax.fori_loop(0, zstart_ref.shape[0], wait, 0)

    def step(slot):
        @pl.when(i + 1 < n_tiles)
        def _():
            idx_copy(i + 1, 1 - slot).start()

        idx_copy(i, slot).wait()

        def start_rows(t, carry):
            for k in range(TOP_K):
                _row_copy(h_ref, t, xs_hbm, idx_smem[slot][k * tile + t], row_sem).start(priority=k)
            return carry

        lax.fori_loop(0, tile, start_rows, 0, unroll=8)
        for k in range(TOP_K):
            pltpu.make_async_copy(h_ref, xs_hbm.at[pl.ds(0, tile * SUBLANES)], row_sem).wait()

    for slot in range(2):
        pl.when(lax.rem(i, 2) == slot)(functools.partial(step, slot))


def _dispatch(zstart, dest_tiles, h, cap):
    t = h.shape[0] // SUBLANES
    any_spec = pl.BlockSpec(memory_space=pl.ANY)
    return pl.pallas_call(
        _dispatch_kernel,
        out_shape=jax.ShapeDtypeStruct((cap * SUBLANES, LANES), h.dtype),
        grid_spec=pltpu.PrefetchScalarGridSpec(
            num_scalar_prefetch=1,
            grid=(t // DISPATCH_ROWS,),
            in_specs=[any_spec,
                      pl.BlockSpec((DISPATCH_ROWS * SUBLANES, LANES), lambda i, zs: (i, 0))],
            out_specs=any_spec,
            scratch_shapes=[
                pltpu.SMEM((TOP_K * DISPATCH_ROWS,), jnp.int32),
                pltpu.SMEM((TOP_K * DISPATCH_ROWS,), jnp.int32),
                pltpu.VMEM((ZERO_ROWS * SUBLANES, LANES), h.dtype),
                pltpu.SemaphoreType.DMA((2,)),
                pltpu.SemaphoreType.DMA,
                pltpu.SemaphoreType.DMA,
            ],
        ),
        compiler_params=pltpu.CompilerParams(
            dimension_semantics=("arbitrary",), has_side_effects=True),
        name="moe_dispatch",
    )(zstart, dest_tiles, h)


def _expert_kernel(be_ref, nv_ref, x_ref, w1_ref, w3_ref, w2_ref, o_ref, w1b_ref, w3b_ref, w2b_ref):
    i = pl.program_id(0)

    @pl.when(i < nv_ref[0])
    def _():
        @pl.when((i == 0) | (be_ref[i] != be_ref[jnp.maximum(i - 1, 0)]))
        def _():
            w1b_ref[...] = w1_ref[...].astype(BF16)
            w3b_ref[...] = w3_ref[...].astype(BF16)
            w2b_ref[...] = w2_ref[...].astype(BF16)

        xb = _load_token_tiles(x_ref, 0, EXPERT_ROWS).astype(BF16)
        a = _dot(xb, w1b_ref[...])
        hid = a * _sigmoid(a) * _dot(xb, w3b_ref[...])
        _store_token_tiles(o_ref, _dot(hid.astype(BF16), w2b_ref[...]))


def _experts(l, block_e, n_valid, xs, w1, w3, w2):
    cap = xs.shape[0] // SUBLANES
    _, _, d, f = w1.shape
    last = lambda i, nv: jnp.minimum(i, nv[0] - 1)
    tile_rows = EXPERT_ROWS * SUBLANES
    return pl.pallas_call(
        _expert_kernel,
        out_shape=jax.ShapeDtypeStruct(xs.shape, F32),
        grid_spec=pltpu.PrefetchScalarGridSpec(
            num_scalar_prefetch=2,
            grid=(cap // EXPERT_ROWS,),
            in_specs=[
                pl.BlockSpec((tile_rows, LANES), lambda i, be, nv: (last(i, nv), 0)),
                pl.BlockSpec((None, None, d, f), lambda i, be, nv: (l, be[last(i, nv)], 0, 0)),
                pl.BlockSpec((None, None, d, f), lambda i, be, nv: (l, be[last(i, nv)], 0, 0)),
                pl.BlockSpec((None, None, f, d), lambda i, be, nv: (l, be[last(i, nv)], 0, 0)),
            ],
            out_specs=pl.BlockSpec((tile_rows, LANES), lambda i, be, nv: (last(i, nv), 0)),
            scratch_shapes=[pltpu.VMEM((d, f), BF16), pltpu.VMEM((d, f), BF16), pltpu.VMEM((f, d), BF16)],
        ),
        compiler_params=pltpu.CompilerParams(
            dimension_semantics=("arbitrary",), vmem_limit_bytes=VMEM_LIMIT_BYTES),
        name="moe_experts",
    )(block_e, n_valid, xs, w1, w3, w2)


def _combine_kernel(final_norm, dest_hbm, ys_hbm, x_ref, g2_ref, wt_ref, fg_ref, o_ref,
                    idx0_smem, idx1_smem, ybuf_ref, idx_sem, row_sem):
    i = pl.program_id(0)
    n_tiles = pl.num_programs(0)
    idx_smem = (idx0_smem, idx1_smem)
    n_idx = idx0_smem.shape[0]
    tile = n_idx // TOP_K

    def idx_copy(tile_index, s):
        src = dest_hbm.at[pl.ds(pl.multiple_of(tile_index * n_idx, n_idx), n_idx)]
        return pltpu.make_async_copy(src, idx_smem[s], idx_sem.at[s])

    def start_rows(s):
        def body(t, carry):
            for k in range(TOP_K):
                row = k * tile + t
                pltpu.make_async_copy(_token_tile(ys_hbm, idx_smem[s][row]),
                                      _token_tile(ybuf_ref.at[s], row), row_sem.at[s]).start(priority=k)
            return carry
        lax.fori_loop(0, tile, body, 0, unroll=8)

    def wait_rows(s):
        pltpu.make_async_copy(ybuf_ref.at[s], ybuf_ref.at[s], row_sem.at[s]).wait()

    @pl.when(i == 0)
    def _():
        idx_copy(0, 0).start()
        idx_copy(0, 0).wait()
        start_rows(0)

        @pl.when(n_tiles > 1)
        def _():
            idx_copy(1, 1).start()

    def step(slot):
        other = 1 - slot

        @pl.when(i + 1 < n_tiles)
        def _():
            idx_copy(i + 1, other).wait()
            start_rows(other)

        @pl.when(i + 2 < n_tiles)
        def _():
            idx_copy(i + 2, slot).start()

        wait_rows(slot)
        wt = wt_ref[...]
        ytiles = ybuf_ref.at[slot]
        y = (wt[:, 0:1] * _load_token_tiles(ytiles, 0, tile)
             + wt[:, 1:2] * _load_token_tiles(ytiles, tile, tile))
        out = x_ref[...] + _per_batch(y, lambda y3: y3 * g2_ref[...])
        if final_norm:
            ms = jnp.mean(out * out, axis=-1, keepdims=True)
            out = out * lax.rsqrt(ms + NORM_EPS) * fg_ref[...]
        o_ref[...] = out

    for slot in range(2):
        pl.when(lax.rem(i, 2) == slot)(functools.partial(step, slot))


def _combine(l, dest_tiles, ys, x, mod, wt, final_g, final_norm):
    t, d = x.shape
    any_spec = pl.BlockSpec(memory_space=pl.ANY)
    return pl.pallas_call(
        functools.partial(_combine_kernel, final_norm),
        out_shape=jax.ShapeDtypeStruct((t, d), F32),
        grid=(t // COMBINE_ROWS,),
        in_specs=[
            any_spec,
            any_spec,
            pl.BlockSpec((COMBINE_ROWS, d), lambda i: (i, 0)),
            _layer_spec(mod, l, MOD_GATE2),
            pl.BlockSpec((COMBINE_ROWS, TOP_K), lambda i: (i, 0)),
            pl.BlockSpec((1, d), lambda i: (0, 0)),
        ],
        out_specs=pl.BlockSpec((COMBINE_ROWS, d), lambda i: (i, 0)),
        scratch_shapes=[
            pltpu.SMEM((TOP_K * COMBINE_ROWS,), jnp.int32),
            pltpu.SMEM((TOP_K * COMBINE_ROWS,), jnp.int32),
            pltpu.VMEM((2, TOP_K * COMBINE_ROWS * SUBLANES, LANES), F32),
            pltpu.SemaphoreType.DMA((2,)),
            pltpu.SemaphoreType.DMA((2,)),
        ],
        compiler_params=pltpu.CompilerParams(
            dimension_semantics=("arbitrary",), vmem_limit_bytes=VMEM_LIMIT_BYTES),
        name="moe_combine",
    )(dest_tiles, ys, x, mod, wt, final_g)


def _ssm_params(lam_re, lam_im, log_step, b_re, b_im, c_re, c_im):
    step = jnp.exp(log_step)[:, None]
    mag = jnp.exp(lam_re * step)
    a_re, a_im = mag * jnp.cos(lam_im * step), mag * jnp.sin(lam_im * step)
    den = lam_re * lam_re + lam_im * lam_im
    n_re, n_im = a_re - 1.0, a_im
    k_re = (n_re * lam_re + n_im * lam_im) / den
    k_im = (n_im * lam_re - n_re * lam_im) / den
    bb_re = k_re[..., None] * b_re - k_im[..., None] * b_im
    bb_im = k_re[..., None] * b_im + k_im[..., None] * b_re
    gpc = SSM_GROUPS // SSM_CHUNKS
    eye = jnp.eye(gpc, dtype=F32)

    def in_blocks(bb):
        bb = bb.reshape(SSM_CHUNKS, gpc, SSM_STATE, SSM_GROUP)
        return jnp.einsum("cgph,gk->cghkp", bb, eye).reshape(SSM_CHUNKS, SSM_CHUNK_IN, SSM_CHUNK_STATE)

    def out_blocks(cc):
        cc = cc.reshape(SSM_CHUNKS, gpc, SSM_GROUP, SSM_STATE)
        return jnp.einsum("cghp,gk->cgpkh", cc, eye).reshape(SSM_CHUNKS, SSM_CHUNK_STATE, SSM_CHUNK_IN)

    wb = jnp.concatenate([in_blocks(bb_re), in_blocks(bb_im)], axis=2).astype(BF16)
    wc = jnp.concatenate([out_blocks(c_re), out_blocks(-c_im)], axis=1).astype(BF16)
    bcast = lambda a: jnp.broadcast_to(a.reshape(SSM_CHUNKS, 1, SSM_CHUNK_STATE),
                                       (SSM_CHUNKS, BATCH, SSM_CHUNK_STATE))
    return wb, wc, bcast(a_re), bcast(a_im)


def _rows8(w):
    depth, k, c = w.shape
    return jnp.broadcast_to(w[:, :, None, :], (depth, k, SUBLANES, c)).reshape(depth, k * SUBLANES, c)


def _router_weights(w_rg, b_rg, w_re, b_re):
    depth, d, _ = w_rg.shape
    pad = SUBLANES - N_GROUPS
    w_t = jnp.concatenate([w_rg.transpose(0, 2, 1), jnp.zeros((depth, pad, d), F32),
                           w_re.transpose(0, 2, 1)], axis=1)
    b = jnp.concatenate([b_rg, jnp.zeros((depth, pad), F32), b_re], axis=1)
    hi, lo = _split_bf16(w_t)
    return hi, lo, jnp.broadcast_to(b[:, :, None], (depth, ROUTER_ROWS, ROUTE_ROWS))


def _routing_tables(route, counts):
    t = route.shape[1]
    counts = counts.astype(jnp.int32)
    padded = (counts + EXPERT_ROWS - 1) // EXPERT_ROWS * EXPERT_ROWS
    pad_end = jnp.cumsum(padded)
    pad_start = pad_end - padded
    experts = route[0:TOP_K].astype(jnp.int32)
    ranks = route[TOP_K:2 * TOP_K].astype(jnp.int32)
    expert_ids = jnp.arange(N_EXPERTS, dtype=jnp.int32)
    is_expert = experts[:, :, None] == expert_ids
    dest = jnp.sum(jnp.where(is_expert, pad_start, 0), axis=-1) + ranks

    def tiles(rows_per_tile):
        n = t // rows_per_tile
        return dest.reshape(TOP_K, n, rows_per_tile).transpose(1, 0, 2).reshape(-1)

    n_blocks = (TOP_K * t) // EXPERT_ROWS + N_EXPERTS
    block_start = jnp.arange(n_blocks, dtype=jnp.int32) * EXPERT_ROWS
    segments_done = jnp.sum(pad_end[None, :] <= block_start[:, None], axis=1)
    block_e = jnp.minimum(segments_done, N_EXPERTS - 1)
    n_valid = (pad_end[-1] // EXPERT_ROWS).reshape(1)
    last_start = pad_end - EXPERT_ROWS
    last_valid = counts - (padded - EXPERT_ROWS)
    piece = jnp.arange(EXPERT_ROWS // ZERO_ROWS, dtype=jnp.int32)
    has_pad = (counts[:, None] > 0) & ((piece[None, :] + 1) * ZERO_ROWS > last_valid[:, None])
    zstart = jnp.where(has_pad, last_start[:, None] + piece[None, :] * ZERO_ROWS, -1).reshape(-1)
    weights = route[2 * TOP_K:3 * TOP_K].T
    return (tiles(DISPATCH_ROWS), tiles(COMBINE_ROWS), block_e.astype(jnp.int32),
            n_valid.astype(jnp.int32), zstart.astype(jnp.int32), weights, n_blocks * EXPERT_ROWS)


def kernel(x, c, norm_mix_g, norm_ffn_g, w_ada, b_ada, w_in, lam_re, lam_im, log_step, ssm_b_re, ssm_b_im, ssm_c_re, ssm_c_im, ssm_d, w_glu, conf_dw_w, conf_dw_b, conf_ln_g, conf_ln_b, sconv_w, w_branch, w_gate, b_gate, w_out, w_router_group, b_router_group, w_router_expert, b_router_expert, w_exp_gate, w_exp_up, w_exp_down, final_norm_g):
    bsz, seq, d = x.shape
    assert bsz == BATCH and d == D_MODEL
    depth = w_in.shape[0]
    t = bsz * seq
    rows = lambda v: v.reshape(depth, 1, -1)

    mix_g, ffn_g = rows(norm_mix_g), rows(norm_ffn_g)
    seq_params = (*jax.vmap(_ssm_params)(lam_re, lam_im, log_step, ssm_b_re, ssm_b_im, ssm_c_re, ssm_c_im),
                  rows(ssm_d), w_glu.astype(BF16), _rows8(conf_dw_w), rows(conf_dw_b), rows(conf_ln_g),
                  rows(conf_ln_b), _rows8(sconv_w))
    w_in_b, w_gate_b, w_branch_b, w_out_b = (w.astype(BF16) for w in (w_in, w_gate, w_branch, w_out))
    r_hi, r_lo, r_b = _router_weights(w_router_group, b_router_group, w_router_expert, b_router_expert)
    tri = jnp.triu(jnp.ones((ROUTE_ROWS, ROUTE_ROWS), BF16))

    mod = _ada(c, w_ada, b_ada)
    xt = jnp.transpose(x, (1, 0, 2)).reshape(t, d)

    for l in range(depth):
        proj = _inproj(l, xt, mix_g, mod, w_in_b)
        ycat = _seq_mixers(l, proj, *seq_params)
        xt = _merge(l, xt, mix_g, mod, ycat, w_gate_b, rows(b_gate), w_branch_b, w_out_b)

        h2, route, counts = _router(l, xt, ffn_g, mod, r_hi, r_lo, r_b, tri)
        dest_d, dest_c, block_e, n_valid, zstart, weights, cap = _routing_tables(route, counts[:, 0])
        xs = _dispatch(zstart, dest_d, h2, cap)
        ys = _experts(l, block_e, n_valid, xs, w_exp_gate, w_exp_up, w_exp_down)
        xt = _combine(l, dest_c, ys, xt, mod, weights, final_norm_g.reshape(1, d), l == depth - 1)

    return jnp.transpose(xt.reshape(seq, bsz, d), (1, 0, 2))
```

```python
import functools
import math

import jax
import jax.numpy as jnp
from jax import lax
from jax.experimental import pallas as pl
from jax.experimental.pallas import tpu as pltpu

F32 = jnp.float32
BF16 = jnp.bfloat16

D_MODEL = 1024
BATCH = 8
BRANCH_WIDTH = D_MODEL // 2
N_BRANCH = 3
N_IN_CHUNKS = 6
SSM_GROUP = 16
SSM_GROUPS = BRANCH_WIDTH // SSM_GROUP
SSM_STATE = 64
CONF_KERNEL = 31
SCONV_KERNEL = 3
N_GROUPS = 4
EXPERTS_PER_GROUP = 8
N_EXPERTS = N_GROUPS * EXPERTS_PER_GROUP
TOP_K = 2
D_FF_EXPERT = D_MODEL // 2
NORM_EPS = 1e-6

SUBLANES = 8
LANES = 128
VMEM_LIMIT_BYTES = 56 * 1024 * 1024

SSM_CHUNKS = 2
SSM_CHUNK_IN = BRANCH_WIDTH // SSM_CHUNKS
SSM_CHUNK_STATE = SSM_GROUPS * SSM_STATE // SSM_CHUNKS
CONF_HIST = (CONF_KERNEL - 1) * BATCH
SCONV_HIST = (SCONV_KERNEL - 1) * BATCH
ROUTER_ROWS = 40
MOD_SHIFT1, MOD_SCALE1, MOD_GATE1, MOD_SHIFT2, MOD_SCALE2, MOD_GATE2 = range(6)

ADA_COLS = 1536
INPROJ_ROWS = 1024
SEQ_ROWS = 512
CONV_ROWS = 64
EW_ROWS = 32
MERGE_ROWS = 512
ROUTE_ROWS = 512
DISPATCH_ROWS = 512
EXPERT_ROWS = 512
ZERO_ROWS = 64
COMBINE_ROWS = 256


def _dot(a, b):
    return jnp.dot(a, b, preferred_element_type=F32)


def _split_bf16(v):
    hi = v.astype(BF16)
    lo = (v - hi.astype(F32)).astype(BF16)
    return hi, lo


def _sigmoid(v):
    return 1.0 / (1.0 + jnp.exp(-v))


def _gelu_tanh(v):
    return 0.5 * v * (1.0 + jnp.tanh(math.sqrt(2.0 / math.pi) * (v + 0.044715 * (v * v * v))))


def _per_batch(v, fn):
    rows, d = v.shape
    return fn(v.reshape(rows // BATCH, BATCH, d)).reshape(rows, d)


def _load_token_tiles(ref, first_token, n_tokens):
    base = first_token * SUBLANES
    chunks = [ref[pl.ds(base + s, n_tokens, stride=SUBLANES), :] for s in range(SUBLANES)]
    return jnp.concatenate(chunks, axis=1)


def _store_token_tiles(ref, value):
    for s in range(SUBLANES):
        ref[pl.ds(s, value.shape[0], stride=SUBLANES), :] = value[:, s * LANES:(s + 1) * LANES]


def _token_tile(ref, token):
    return ref.at[pl.ds(pl.multiple_of(token * SUBLANES, SUBLANES), SUBLANES)]


def _layer_spec(arr, l, col_block=None):
    tail = arr.shape[1:]
    if col_block is None:
        zeros = (0,) * len(tail)
        return pl.BlockSpec((None,) + tail, lambda *_: (l,) + zeros)
    return pl.BlockSpec((None, tail[0], D_MODEL), lambda *_: (l, 0, col_block))


def _norm_modulate(x, g, scale, shift):
    ms = jnp.mean(x * x, axis=-1, keepdims=True)
    h = x * lax.rsqrt(ms + NORM_EPS) * g
    return _per_batch(h, lambda h3: h3 * (1.0 + scale) + shift)


def _ada_kernel(c_ref, w_ref, b_ref, o_ref):
    c = c_ref[...]
    cond = c * _sigmoid(c)
    chi, clo = _split_bf16(cond)
    whi, wlo = _split_bf16(w_ref[...])
    o_ref[...] = _dot(chi, whi) + _dot(chi, wlo) + _dot(clo, whi) + b_ref[...]


def _ada(c, w_ada, b_ada):
    depth, d, n = w_ada.shape
    return pl.pallas_call(
        _ada_kernel,
        out_shape=jax.ShapeDtypeStruct((depth, BATCH, n), F32),
        grid=(depth, n // ADA_COLS),
        in_specs=[
            pl.BlockSpec((BATCH, d), lambda l, j: (0, 0)),
            pl.BlockSpec((None, d, ADA_COLS), lambda l, j: (l, 0, j)),
            pl.BlockSpec((None, 1, ADA_COLS), lambda l, j: (l, 0, j)),
        ],
        out_specs=pl.BlockSpec((None, BATCH, ADA_COLS), lambda l, j: (l, 0, j)),
        compiler_params=pltpu.CompilerParams(
            dimension_semantics=("arbitrary", "arbitrary"), vmem_limit_bytes=VMEM_LIMIT_BYTES),
        name="ada_modulation",
    )(c, w_ada, b_ada.reshape(depth, 1, n))


def _inproj_kernel(x_ref, g_ref, sc_ref, sh_ref, w_ref, o_ref):
    h = _norm_modulate(x_ref[...], g_ref[...], sc_ref[...], sh_ref[...])
    o_ref[...] = _dot(h.astype(BF16), w_ref[...]).astype(o_ref.dtype)


def _inproj(l, x, g, mod, w_in):
    t, d = x.shape
    n = w_in.shape[2]
    return pl.pallas_call(
        _inproj_kernel,
        out_shape=jax.ShapeDtypeStruct((t, n), BF16),
        grid=(t // INPROJ_ROWS,),
        in_specs=[
            pl.BlockSpec((INPROJ_ROWS, d), lambda i: (i, 0)),
            _layer_spec(g, l),
            _layer_spec(mod, l, MOD_SCALE1),
            _layer_spec(mod, l, MOD_SHIFT1),
            _layer_spec(w_in, l),
        ],
        out_specs=pl.BlockSpec((INPROJ_ROWS, n), lambda i: (i, 0)),
        compiler_params=pltpu.CompilerParams(
            dimension_semantics=("arbitrary",), vmem_limit_bytes=VMEM_LIMIT_BYTES),
        name="mixer_inproj",
    )(x, g, mod, mod, w_in)


def _seq_kernel(p_ref, wb_ref, wc_ref, are_ref, aim_ref, d_ref, wglu_ref, cw_ref, cb_ref,
                lng_ref, lnb_ref, sw_ref, o_ref, bu_ref, st_ref, zbuf_ref, cbuf_ref, qbuf_ref):
    rows = p_ref.shape[0]
    steps = rows // BATCH
    w = BRANCH_WIDTH

    @pl.when(pl.program_id(0) == 0)
    def _():
        st_ref[...] = jnp.zeros_like(st_ref)
        zbuf_ref[0:CONF_HIST, :] = jnp.zeros((CONF_HIST, w), F32)
        qbuf_ref[0:SCONV_HIST, :] = jnp.zeros((SCONV_HIST, w), F32)

    chunk_groups = EW_ROWS // SUBLANES

    def pointwise_chunk(j, carry):
        r0 = pl.multiple_of(j * EW_ROWS, EW_ROWS)
        rs = pl.ds(r0, EW_ROWS)
        gate_c = p_ref[rs, 4 * w:5 * w].astype(F32)
        hv = p_ref[rs, 5 * w:6 * w].astype(F32)
        qbuf_ref[pl.ds(r0 + SCONV_HIST, EW_ROWS), :] = gate_c * hv
        q3 = qbuf_ref[pl.ds(r0, EW_ROWS + SCONV_HIST), :].reshape(
            chunk_groups + SCONV_KERNEL - 1, SUBLANES, w)
        acc = q3[0:chunk_groups] * sw_ref[0:SUBLANES, :]
        for k in range(1, SCONV_KERNEL):
            acc = acc + q3[k:k + chunk_groups] * sw_ref[k * SUBLANES:(k + 1) * SUBLANES, :]
        gate_b = p_ref[rs, 3 * w:4 * w].astype(F32)
        o_ref[rs, 2 * w:3 * w] = (gate_b * acc.reshape(EW_ROWS, w)).astype(o_ref.dtype)
        v = p_ref[rs, w:2 * w].astype(F32)
        g = p_ref[rs, 2 * w:3 * w].astype(F32)
        zbuf_ref[pl.ds(r0 + CONF_HIST, EW_ROWS), :] = v * _sigmoid(g)
        return carry

    lax.fori_loop(0, rows // EW_ROWS, pointwise_chunk, 0)
    qbuf_ref[0:SCONV_HIST, :] = qbuf_ref[rows:rows + SCONV_HIST, :]

    ys = []
    for c in range(SSM_CHUNKS):
        u_c = p_ref[:, c * SSM_CHUNK_IN:(c + 1) * SSM_CHUNK_IN]
        bu_ref[...] = _dot(u_c, wb_ref[c])
        a_re = are_ref[c]
        a_im = aim_ref[c]

        def step(t, carry):
            x_re, x_im = carry
            r0 = pl.multiple_of(t * BATCH, BATCH)
            n_re = a_re * x_re - a_im * x_im + bu_ref[pl.ds(r0, BATCH), 0:SSM_CHUNK_STATE]
            n_im = a_re * x_im + a_im * x_re + bu_ref[pl.ds(r0, BATCH), SSM_CHUNK_STATE:]
            bu_ref[pl.ds(r0, BATCH), 0:SSM_CHUNK_STATE] = n_re
            bu_ref[pl.ds(r0, BATCH), SSM_CHUNK_STATE:] = n_im
            return n_re, n_im

        x_re, x_im = lax.fori_loop(0, steps, step, (st_ref[c, 0], st_ref[c, 1]), unroll=4)
        st_ref[c, 0] = x_re
        st_ref[c, 1] = x_im
        ys.append(_dot(bu_ref[...].astype(BF16), wc_ref[c]))
    u = p_ref[:, 0:w].astype(F32)
    y = _gelu_tanh(jnp.concatenate(ys, axis=1) + d_ref[...] * u)
    y_ssm = y * _sigmoid(_dot(y.astype(BF16), wglu_ref[...]))
    o_ref[:, 0:w] = y_ssm.astype(o_ref.dtype)

    win_groups = (CONV_ROWS + CONF_HIST) // SUBLANES
    out_groups = CONV_ROWS // SUBLANES

    def conv_chunk(j, carry):
        r0 = pl.multiple_of(j * CONV_ROWS, CONV_ROWS)
        for s in range(w // LANES):
            ls = slice(s * LANES, (s + 1) * LANES)
            win = zbuf_ref[pl.ds(r0, CONV_ROWS + CONF_HIST), ls].reshape(win_groups, SUBLANES, LANES)
            acc = win[0:out_groups] * cw_ref[0:SUBLANES, ls]
            for k in range(1, CONF_KERNEL):
                acc = acc + win[k:k + out_groups] * cw_ref[k * SUBLANES:(k + 1) * SUBLANES, ls]
            cbuf_ref[pl.ds(r0, CONV_ROWS), ls] = acc.reshape(CONV_ROWS, LANES)
        return carry

    lax.fori_loop(0, rows // CONV_ROWS, conv_chunk, 0)
    zbuf_ref[0:CONF_HIST, :] = zbuf_ref[rows:rows + CONF_HIST, :]
    cv = cbuf_ref[...] + cb_ref[...]
    cc = cv - jnp.mean(cv, axis=-1, keepdims=True)
    ln = cc * lax.rsqrt(jnp.mean(cc * cc, axis=-1, keepdims=True) + NORM_EPS) * lng_ref[...] + lnb_ref[...]
    o_ref[:, w:2 * w] = (ln * _sigmoid(ln)).astype(o_ref.dtype)


def _seq_mixers(l, proj, *layer_params):
    t = proj.shape[0]
    w = BRANCH_WIDTH
    return pl.pallas_call(
        _seq_kernel,
        out_shape=jax.ShapeDtypeStruct((t, N_BRANCH * w), BF16),
        grid=(t // SEQ_ROWS,),
        in_specs=[pl.BlockSpec((SEQ_ROWS, N_IN_CHUNKS * w), lambda i: (i, 0))]
                 + [_layer_spec(p, l) for p in layer_params],
        out_specs=pl.BlockSpec((SEQ_ROWS, N_BRANCH * w), lambda i: (i, 0)),
        scratch_shapes=[
            pltpu.VMEM((SEQ_ROWS, 2 * SSM_CHUNK_STATE), F32),
            pltpu.VMEM((SSM_CHUNKS, 2, BATCH, SSM_CHUNK_STATE), F32),
            pltpu.VMEM((SEQ_ROWS + CONF_HIST, w), F32),
            pltpu.VMEM((SEQ_ROWS, w), F32),
            pltpu.VMEM((SEQ_ROWS + SCONV_HIST, w), F32),
        ],
        compiler_params=pltpu.CompilerParams(
            dimension_semantics=("arbitrary",), vmem_limit_bytes=VMEM_LIMIT_BYTES),
        name="seq_mixers",
    )(proj, *layer_params)


def _merge_kernel(x_ref, g_ref, sc_ref, sh_ref, g1_ref, y_ref, wg_ref, bg_ref, wbr_ref, wo_ref, o_ref):
    x = x_ref[...]
    d = x.shape[1]
    w = BRANCH_WIDTH
    hb = _norm_modulate(x, g_ref[...], sc_ref[...], sh_ref[...]).astype(BF16)
    merged = None
    for n in range(N_BRANCH):
        gate = _sigmoid(_dot(hb, wg_ref[:, n * d:(n + 1) * d]) + bg_ref[:, n * d:(n + 1) * d])
        term = gate * _dot(y_ref[:, n * w:(n + 1) * w], wbr_ref[n])
        merged = term if merged is None else merged + term
    out = _dot(merged.astype(BF16), wo_ref[...])
    o_ref[...] = x + _per_batch(out, lambda o3: o3 * g1_ref[...])


def _merge(l, x, g, mod, ycat, w_gate, b_gate, w_branch, w_out):
    t, d = x.shape
    w = BRANCH_WIDTH
    return pl.pallas_call(
        _merge_kernel,
        out_shape=jax.ShapeDtypeStruct((t, d), F32),
        grid=(t // MERGE_ROWS,),
        in_specs=[
            pl.BlockSpec((MERGE_ROWS, d), lambda i: (i, 0)),
            _layer_spec(g, l),
            _layer_spec(mod, l, MOD_SCALE1),
            _layer_spec(mod, l, MOD_SHIFT1),
            _layer_spec(mod, l, MOD_GATE1),
            pl.BlockSpec((MERGE_ROWS, N_BRANCH * w), lambda i: (i, 0)),
            _layer_spec(w_gate, l),
            _layer_spec(b_gate, l),
            _layer_spec(w_branch, l),
            _layer_spec(w_out, l),
        ],
        out_specs=pl.BlockSpec((MERGE_ROWS, d), lambda i: (i, 0)),
        compiler_params=pltpu.CompilerParams(
            dimension_semantics=("arbitrary",), vmem_limit_bytes=VMEM_LIMIT_BYTES),
        name="mixer_merge",
    )(x, g, mod, mod, mod, ycat, w_gate, b_gate, w_branch, w_out)


def _router_kernel(x_ref, g_ref, sc_ref, sh_ref, whi_ref, wlo_ref, b_ref, tri_ref,
                   h_ref, route_ref, cnt_ref, run_ref):
    @pl.when(pl.program_id(0) == 0)
    def _():
        run_ref[...] = jnp.zeros_like(run_ref)

    h = _norm_modulate(x_ref[...], g_ref[...], sc_ref[...], sh_ref[...])
    _store_token_tiles(h_ref, h)
    rows = h.shape[0]
    hhi, hlo = _split_bf16(h)
    contract_last = (((1,), (1,)), ((), ()))
    dott = lambda a, b: lax.dot_general(a, b, contract_last, preferred_element_type=F32)
    logits = dott(whi_ref[...], hhi) + dott(wlo_ref[...], hhi) + dott(whi_ref[...], hlo) + b_ref[...]
    lg = logits[0:N_GROUPS]
    le = logits[SUBLANES:SUBLANES + N_EXPERTS]

    g_row = lax.broadcasted_iota(jnp.int32, (N_GROUPS, rows), 0).astype(F32)
    g_max = jnp.max(lg, axis=0, keepdims=True)
    g_sel = jnp.min(jnp.where(lg == g_max, g_row, float(N_GROUPS)), axis=0, keepdims=True)
    g_w = 1.0 / jnp.sum(jnp.exp(lg - g_max), axis=0, keepdims=True)

    e_row = lax.broadcasted_iota(jnp.int32, (N_EXPERTS, rows), 0).astype(F32)
    e_grp = jnp.floor(e_row * (1.0 / EXPERTS_PER_GROUP))
    neg = float("-inf")
    lm = jnp.where(e_grp == g_sel, le, neg)
    v1 = jnp.max(lm, axis=0, keepdims=True)
    i1 = jnp.min(jnp.where(lm == v1, e_row, float(N_EXPERTS)), axis=0, keepdims=True)
    lm2 = jnp.where(e_row == i1, neg, lm)
    v2 = jnp.max(lm2, axis=0, keepdims=True)
    i2 = jnp.min(jnp.where(lm2 == v2, e_row, float(N_EXPERTS)), axis=0, keepdims=True)
    e21 = jnp.exp(v2 - v1)
    p1 = 1.0 / (1.0 + e21)
    w1 = p1 * g_w
    w2 = e21 * p1 * g_w

    hit1 = e_row == i1
    hit2 = e_row == i2
    onehot = jnp.where(hit1 | hit2, 1.0, 0.0)
    cum = _dot(onehot.astype(BF16), tri_ref[...])
    before = cum - onehot + run_ref[...]
    r1 = jnp.sum(jnp.where(hit1, before, 0.0), axis=0, keepdims=True)
    r2 = jnp.sum(jnp.where(hit2, before, 0.0), axis=0, keepdims=True)
    run_ref[...] = run_ref[...] + jnp.broadcast_to(cum[:, rows - 1:rows], run_ref.shape)
    cnt_ref[...] = run_ref[...]
    zero = jnp.zeros_like(w1)
    route_ref[...] = jnp.concatenate([i1, i2, r1, r2, w1, w2, zero, zero], axis=0)


def _router(l, x, g, mod, w_hi, w_lo, bias, tri):
    t, d = x.shape
    return pl.pallas_call(
        _router_kernel,
        out_shape=(
            jax.ShapeDtypeStruct((t * SUBLANES, LANES), F32),
            jax.ShapeDtypeStruct((SUBLANES, t), F32),
            jax.ShapeDtypeStruct((N_EXPERTS, ROUTE_ROWS), F32),
        ),
        grid=(t // ROUTE_ROWS,),
        in_specs=[
            pl.BlockSpec((ROUTE_ROWS, d), lambda i: (i, 0)),
            _layer_spec(g, l),
            _layer_spec(mod, l, MOD_SCALE2),
            _layer_spec(mod, l, MOD_SHIFT2),
            _layer_spec(w_hi, l),
            _layer_spec(w_lo, l),
            _layer_spec(bias, l),
            pl.BlockSpec((ROUTE_ROWS, ROUTE_ROWS), lambda i: (0, 0)),
        ],
        out_specs=(
            pl.BlockSpec((ROUTE_ROWS * SUBLANES, LANES), lambda i: (i, 0)),
            pl.BlockSpec((SUBLANES, ROUTE_ROWS), lambda i: (0, i)),
            pl.BlockSpec((N_EXPERTS, ROUTE_ROWS), lambda i: (0, 0)),
        ),
        scratch_shapes=[pltpu.VMEM((N_EXPERTS, ROUTE_ROWS), F32)],
        compiler_params=pltpu.CompilerParams(
            dimension_semantics=("arbitrary",), vmem_limit_bytes=VMEM_LIMIT_BYTES),
        name="moe_router",
    )(x, g, mod, mod, w_hi, w_lo, bias, tri)


def _row_copy(src_hbm, src_token, dst_hbm, dst_token, sem):
    return pltpu.make_async_copy(_token_tile(src_hbm, src_token), _token_tile(dst_hbm, dst_token), sem)


def _dispatch_kernel(zstart_ref, dest_hbm, h_ref, xs_hbm, idx0_smem, idx1_smem, zeros_ref,
                     idx_sem, zero_sem, row_sem):
    i = pl.program_id(0)
    n_tiles = pl.num_programs(0)
    idx_smem = (idx0_smem, idx1_smem)
    n_idx = idx0_smem.shape[0]
    tile = n_idx // TOP_K

    def idx_copy(tile_index, s):
        src = dest_hbm.at[pl.ds(pl.multiple_of(tile_index * n_idx, n_idx), n_idx)]
        return pltpu.make_async_copy(src, idx_smem[s], idx_sem.at[s])

    @pl.when(i == 0)
    def _():
        idx_copy(0, 0).start()

    def zero_copy(j):
        piece_rows = ZERO_ROWS * SUBLANES
        start = pl.multiple_of(zstart_ref[j] * SUBLANES, piece_rows)
        return pltpu.make_async_copy(zeros_ref, xs_hbm.at[pl.ds(start, piece_rows)], zero_sem)

    @pl.when(i == 0)
    def _():
        zeros_ref[...] = jnp.zeros_like(zeros_ref)

        def start(j, carry):
            @pl.when(zstart_ref[j] >= 0)
            def _():
                zero_copy(j).start()
            return carry

        def wait(j, carry):
            @pl.when(zstart_ref[j] >= 0)
            def _():
                zero_copy(j).wait()
            return carry

        lax.fori_loop(0, zstart_ref.shape[0], start, 0)
        lax.fori_loop(0, zstart_ref.shape[0], wait, 0)

    def step(slot):
        @pl.when(i + 1 < n_tiles)
        def _():
            idx_copy(i + 1, 1 - slot).start()

        idx_copy(i, slot).wait()

        def start_rows(t, carry):
            for k in range(TOP_K):
                _row_copy(h_ref, t, xs_hbm, idx_smem[slot][k * tile + t], row_sem).start(priority=k)
            return carry

        lax.fori_loop(0, tile, start_rows, 0, unroll=8)
        for k in range(TOP_K):
            pltpu.make_async_copy(h_ref, xs_hbm.at[pl.ds(0, tile * SUBLANES)], row_sem).wait()

    for slot in range(2):
        pl.when(lax.rem(i, 2) == slot)(functools.partial(step, slot))


def _dispatch(zstart, dest_tiles, h, cap):
    t = h.shape[0] // SUBLANES
    any_spec = pl.BlockSpec(memory_space=pl.ANY)
    return pl.pallas_call(
        _dispatch_kernel,
        out_shape=jax.ShapeDtypeStruct((cap * SUBLANES, LANES), h.dtype),
        grid_spec=pltpu.PrefetchScalarGridSpec(
            num_scalar_prefetch=1,
            grid=(t // DISPATCH_ROWS,),
            in_specs=[any_spec,
                      pl.BlockSpec((DISPATCH_ROWS * SUBLANES, LANES), lambda i, zs: (i, 0))],
            out_specs=any_spec,
            scratch_shapes=[
                pltpu.SMEM((TOP_K * DISPATCH_ROWS,), jnp.int32),
                pltpu.SMEM((TOP_K * DISPATCH_ROWS,), jnp.int32),
                pltpu.VMEM((ZERO_ROWS * SUBLANES, LANES), h.dtype),
                pltpu.SemaphoreType.DMA((2,)),
                pltpu.SemaphoreType.DMA,
                pltpu.SemaphoreType.DMA,
            ],
        ),
        compiler_params=pltpu.CompilerParams(
            dimension_semantics=("arbitrary",), has_side_effects=True),
        name="moe_dispatch",
    )(zstart, dest_tiles, h)


def _expert_kernel(first_ref, nblk_ref, xs_hbm, w1_ref, w3_ref, w2_ref, ys_hbm,
                   w1b_ref, w3b_ref, w2b_ref, xbuf_ref, ybuf_ref, in_sem, out_sem):
    e = pl.program_id(0)
    n_blocks = nblk_ref[e]
    first = first_ref[e]
    block_rows = EXPERT_ROWS * SUBLANES

    def rows_of(block):
        return pl.ds(pl.multiple_of(block * block_rows, block_rows), block_rows)

    def in_copy(block, slot):
        return pltpu.make_async_copy(xs_hbm.at[rows_of(block)], xbuf_ref.at[slot], in_sem.at[slot])

    def out_copy(block, slot):
        return pltpu.make_async_copy(ybuf_ref.at[slot], ys_hbm.at[rows_of(block)], out_sem.at[slot])

    @pl.when(n_blocks > 0)
    def _():
        in_copy(first, 0).start()
        w1b_ref[...] = w1_ref[...].astype(BF16)
        w3b_ref[...] = w3_ref[...].astype(BF16)
        w2b_ref[...] = w2_ref[...].astype(BF16)

        def block_step(j, carry):
            slot = lax.rem(j, 2)

            @pl.when(j + 1 < n_blocks)
            def _():
                in_copy(first + j + 1, 1 - slot).start()

            in_copy(first + j, slot).wait()

            @pl.when(j >= 2)
            def _():
                out_copy(first + j - 2, slot).wait()

            xb = _load_token_tiles(xbuf_ref.at[slot], 0, EXPERT_ROWS).astype(BF16)
            a = _dot(xb, w1b_ref[...])
            hid = a * _sigmoid(a) * _dot(xb, w3b_ref[...])
            _store_token_tiles(ybuf_ref.at[slot], _dot(hid.astype(BF16), w2b_ref[...]))
            out_copy(first + j, slot).start()
            return carry

        lax.fori_loop(0, n_blocks, block_step, 0)

        @pl.when(n_blocks >= 2)
        def _():
            out_copy(first + n_blocks - 2, lax.rem(n_blocks, 2)).wait()

        out_copy(first + n_blocks - 1, lax.rem(n_blocks - 1, 2)).wait()


def _experts(l, first_block, n_blocks, xs, w1, w3, w2):
    _, n_experts, d, f = w1.shape
    tile = (EXPERT_ROWS * SUBLANES, LANES)
    any_spec = pl.BlockSpec(memory_space=pl.ANY)
    weight_spec = lambda rows, cols: pl.BlockSpec((None, None, rows, cols), lambda e, fb, nb: (l, e, 0, 0))
    return pl.pallas_call(
        _expert_kernel,
        out_shape=jax.ShapeDtypeStruct(xs.shape, F32),
        grid_spec=pltpu.PrefetchScalarGridSpec(
            num_scalar_prefetch=2,
            grid=(n_experts,),
            in_specs=[any_spec, weight_spec(d, f), weight_spec(d, f), weight_spec(f, d)],
            out_specs=any_spec,
            scratch_shapes=[
                pltpu.VMEM((d, f), BF16), pltpu.VMEM((d, f), BF16), pltpu.VMEM((f, d), BF16),
                pltpu.VMEM((2,) + tile, F32), pltpu.VMEM((2,) + tile, F32),
                pltpu.SemaphoreType.DMA((2,)), pltpu.SemaphoreType.DMA((2,)),
            ],
        ),
        compiler_params=pltpu.CompilerParams(
            dimension_semantics=("arbitrary",), vmem_limit_bytes=VMEM_LIMIT_BYTES),
        name="moe_experts",
    )(first_block, n_blocks, xs, w1, w3, w2)


def _combine_kernel(final_norm, dest_hbm, ys_hbm, x_ref, g2_ref, wt_ref, fg_ref, o_ref,
                    idx0_smem, idx1_smem, ybuf_ref, idx_sem, row_sem):
    i = pl.program_id(0)
    n_tiles = pl.num_programs(0)
    idx_smem = (idx0_smem, idx1_smem)
    n_idx = idx0_smem.shape[0]
    tile = n_idx // TOP_K

    def idx_copy(tile_index, s):
        src = dest_hbm.at[pl.ds(pl.multiple_of(tile_index * n_idx, n_idx), n_idx)]
        return pltpu.make_async_copy(src, idx_smem[s], idx_sem.at[s])

    def start_rows(s):
        def body(t, carry):
            for k in range(TOP_K):
                row = k * tile + t
                pltpu.make_async_copy(_token_tile(ys_hbm, idx_smem[s][row]),
                                      _token_tile(ybuf_ref.at[s], row), row_sem.at[s]).start(priority=k)
            return carry
        lax.fori_loop(0, tile, body, 0, unroll=8)

    def wait_rows(s):
        pltpu.make_async_copy(ybuf_ref.at[s], ybuf_ref.at[s], row_sem.at[s]).wait()

    @pl.when(i == 0)
    def _():
        idx_copy(0, 0).start()
        idx_copy(0, 0).wait()
        start_rows(0)

        @pl.when(n_tiles > 1)
        def _():
            idx_copy(1, 1).start()

    def step(slot):
        other = 1 - slot

        @pl.when(i + 1 < n_tiles)
        def _():
            idx_copy(i + 1, other).wait()
            start_rows(other)

        @pl.when(i + 2 < n_tiles)
        def _():
            idx_copy(i + 2, slot).start()

        wait_rows(slot)
        wt = wt_ref[...]
        ytiles = ybuf_ref.at[slot]
        y = (wt[:, 0:1] * _load_token_tiles(ytiles, 0, tile)
             + wt[:, 1:2] * _load_token_tiles(ytiles, tile, tile))
        out = x_ref[...] + _per_batch(y, lambda y3: y3 * g2_ref[...])
        if final_norm:
            ms = jnp.mean(out * out, axis=-1, keepdims=True)
            out = out * lax.rsqrt(ms + NORM_EPS) * fg_ref[...]
        o_ref[...] = out

    for slot in range(2):
        pl.when(lax.rem(i, 2) == slot)(functools.partial(step, slot))


def _combine(l, dest_tiles, ys, x, mod, wt, final_g, final_norm):
    t, d = x.shape
    any_spec = pl.BlockSpec(memory_space=pl.ANY)
    return pl.pallas_call(
        functools.partial(_combine_kernel, final_norm),
        out_shape=jax.ShapeDtypeStruct((t, d), F32),
        grid=(t // COMBINE_ROWS,),
        in_specs=[
            any_spec,
            any_spec,
            pl.BlockSpec((COMBINE_ROWS, d), lambda i: (i, 0)),
            _layer_spec(mod, l, MOD_GATE2),
            pl.BlockSpec((COMBINE_ROWS, TOP_K), lambda i: (i, 0)),
            pl.BlockSpec((1, d), lambda i: (0, 0)),
        ],
        out_specs=pl.BlockSpec((COMBINE_ROWS, d), lambda i: (i, 0)),
        scratch_shapes=[
            pltpu.SMEM((TOP_K * COMBINE_ROWS,), jnp.int32),
            pltpu.SMEM((TOP_K * COMBINE_ROWS,), jnp.int32),
            pltpu.VMEM((2, TOP_K * COMBINE_ROWS * SUBLANES, LANES), F32),
            pltpu.SemaphoreType.DMA((2,)),
            pltpu.SemaphoreType.DMA((2,)),
        ],
        compiler_params=pltpu.CompilerParams(
            dimension_semantics=("arbitrary",), vmem_limit_bytes=VMEM_LIMIT_BYTES),
        name="moe_combine",
    )(dest_tiles, ys, x, mod, wt, final_g)


def _ssm_params(lam_re, lam_im, log_step, b_re, b_im, c_re, c_im):
    step = jnp.exp(log_step)[:, None]
    mag = jnp.exp(lam_re * step)
    a_re, a_im = mag * jnp.cos(lam_im * step), mag * jnp.sin(lam_im * step)
    den = lam_re * lam_re + lam_im * lam_im
    n_re, n_im = a_re - 1.0, a_im
    k_re = (n_re * lam_re + n_im * lam_im) / den
    k_im = (n_im * lam_re - n_re * lam_im) / den
    bb_re = k_re[..., None] * b_re - k_im[..., None] * b_im
    bb_im = k_re[..., None] * b_im + k_im[..., None] * b_re
    gpc = SSM_GROUPS // SSM_CHUNKS
    eye = jnp.eye(gpc, dtype=F32)

    def in_blocks(bb):
        bb = bb.reshape(SSM_CHUNKS, gpc, SSM_STATE, SSM_GROUP)
        return jnp.einsum("cgph,gk->cghkp", bb, eye).reshape(SSM_CHUNKS, SSM_CHUNK_IN, SSM_CHUNK_STATE)

    def out_blocks(cc):
        cc = cc.reshape(SSM_CHUNKS, gpc, SSM_GROUP, SSM_STATE)
        return jnp.einsum("cghp,gk->cgpkh", cc, eye).reshape(SSM_CHUNKS, SSM_CHUNK_STATE, SSM_CHUNK_IN)

    wb = jnp.concatenate([in_blocks(bb_re), in_blocks(bb_im)], axis=2).astype(BF16)
    wc = jnp.concatenate([out_blocks(c_re), out_blocks(-c_im)], axis=1).astype(BF16)
    bcast = lambda a: jnp.broadcast_to(a.reshape(SSM_CHUNKS, 1, SSM_CHUNK_STATE),
                                       (SSM_CHUNKS, BATCH, SSM_CHUNK_STATE))
    return wb, wc, bcast(a_re), bcast(a_im)


def _rows8(w):
    depth, k, c = w.shape
    return jnp.broadcast_to(w[:, :, None, :], (depth, k, SUBLANES, c)).reshape(depth, k * SUBLANES, c)


def _router_weights(w_rg, b_rg, w_re, b_re):
    depth, d, _ = w_rg.shape
    pad = SUBLANES - N_GROUPS
    w_t = jnp.concatenate([w_rg.transpose(0, 2, 1), jnp.zeros((depth, pad, d), F32),
                           w_re.transpose(0, 2, 1)], axis=1)
    b = jnp.concatenate([b_rg, jnp.zeros((depth, pad), F32), b_re], axis=1)
    hi, lo = _split_bf16(w_t)
    return hi, lo, jnp.broadcast_to(b[:, :, None], (depth, ROUTER_ROWS, ROUTE_ROWS))


def _routing_tables(route, counts):
    t = route.shape[1]
    counts = counts.astype(jnp.int32)
    padded = (counts + EXPERT_ROWS - 1) // EXPERT_ROWS * EXPERT_ROWS
    pad_end = jnp.cumsum(padded)
    pad_start = pad_end - padded
    experts = route[0:TOP_K].astype(jnp.int32)
    ranks = route[TOP_K:2 * TOP_K].astype(jnp.int32)
    expert_ids = jnp.arange(N_EXPERTS, dtype=jnp.int32)
    is_expert = experts[:, :, None] == expert_ids
    dest = jnp.sum(jnp.where(is_expert, pad_start, 0), axis=-1) + ranks

    def tiles(rows_per_tile):
        n = t // rows_per_tile
        return dest.reshape(TOP_K, n, rows_per_tile).transpose(1, 0, 2).reshape(-1)

    capacity = TOP_K * t + N_EXPERTS * EXPERT_ROWS
    first_block = pad_start // EXPERT_ROWS
    n_blocks = padded // EXPERT_ROWS
    last_start = pad_end - EXPERT_ROWS
    last_valid = counts - (padded - EXPERT_ROWS)
    piece = jnp.arange(EXPERT_ROWS // ZERO_ROWS, dtype=jnp.int32)
    has_pad = (counts[:, None] > 0) & ((piece[None, :] + 1) * ZERO_ROWS > last_valid[:, None])
    zstart = jnp.where(has_pad, last_start[:, None] + piece[None, :] * ZERO_ROWS, -1).reshape(-1)
    weights = route[2 * TOP_K:3 * TOP_K].T
    return (tiles(DISPATCH_ROWS), tiles(COMBINE_ROWS), first_block.astype(jnp.int32),
            n_blocks.astype(jnp.int32), zstart.astype(jnp.int32), weights, capacity)


def kernel(x, c, norm_mix_g, norm_ffn_g, w_ada, b_ada, w_in, lam_re, lam_im, log_step, ssm_b_re, ssm_b_im, ssm_c_re, ssm_c_im, ssm_d, w_glu, conf_dw_w, conf_dw_b, conf_ln_g, conf_ln_b, sconv_w, w_branch, w_gate, b_gate, w_out, w_router_group, b_router_group, w_router_expert, b_router_expert, w_exp_gate, w_exp_up, w_exp_down, final_norm_g):
    bsz, seq, d = x.shape
    assert bsz == BATCH and d == D_MODEL
    depth = w_in.shape[0]
    t = bsz * seq
    rows = lambda v: v.reshape(depth, 1, -1)

    mix_g, ffn_g = rows(norm_mix_g), rows(norm_ffn_g)
    seq_params = (*jax.vmap(_ssm_params)(lam_re, lam_im, log_step, ssm_b_re, ssm_b_im, ssm_c_re, ssm_c_im),
                  rows(ssm_d), w_glu.astype(BF16), _rows8(conf_dw_w), rows(conf_dw_b), rows(conf_ln_g),
                  rows(conf_ln_b), _rows8(sconv_w))
    w_in_b, w_gate_b, w_branch_b, w_out_b = (w.astype(BF16) for w in (w_in, w_gate, w_branch, w_out))
    r_hi, r_lo, r_b = _router_weights(w_router_group, b_router_group, w_router_expert, b_router_expert)
    tri = jnp.triu(jnp.ones((ROUTE_ROWS, ROUTE_ROWS), BF16))

    mod = _ada(c, w_ada, b_ada)
    xt = jnp.transpose(x, (1, 0, 2)).reshape(t, d)

    for l in range(depth):
        proj = _inproj(l, xt, mix_g, mod, w_in_b)
        ycat = _seq_mixers(l, proj, *seq_params)
        xt = _merge(l, xt, mix_g, mod, ycat, w_gate_b, rows(b_gate), w_branch_b, w_out_b)

        h2, route, counts = _router(l, xt, ffn_g, mod, r_hi, r_lo, r_b, tri)
        dest_d, dest_c, first_block, n_blocks, zstart, weights, cap = _routing_tables(route, counts[:, 0])
        xs = _dispatch(zstart, dest_d, h2, cap)
        ys = _experts(l, first_block, n_blocks, xs, w_exp_gate, w_exp_up, w_exp_down)
        xt = _combine(l, dest_c, ys, xt, mod, weights, final_norm_g.reshape(1, d), l == depth - 1)

    return jnp.transpose(xt.reshape(seq, bsz, d), (1, 0, 2))
```

```python
import functools
import math

import jax
import jax.numpy as jnp
from jax import lax
from jax.experimental import pallas as pl
from jax.experimental.pallas import tpu as pltpu

F32 = jnp.float32
BF16 = jnp.bfloat16

D_MODEL = 1024
BATCH = 8
BRANCH_WIDTH = D_MODEL // 2
N_BRANCH = 3
N_IN_CHUNKS = 6
SSM_GROUP = 16
SSM_GROUPS = BRANCH_WIDTH // SSM_GROUP
SSM_STATE = 64
CONF_KERNEL = 31
SCONV_KERNEL = 3
N_GROUPS = 4
EXPERTS_PER_GROUP = 8
N_EXPERTS = N_GROUPS * EXPERTS_PER_GROUP
TOP_K = 2
D_FF_EXPERT = D_MODEL // 2
NORM_EPS = 1e-6

SUBLANES = 8
LANES = 128
VMEM_LIMIT_BYTES = 56 * 1024 * 1024

SSM_CHUNKS = 2
SSM_CHUNK_IN = BRANCH_WIDTH // SSM_CHUNKS
SSM_CHUNK_STATE = SSM_GROUPS * SSM_STATE // SSM_CHUNKS
CONF_HIST = (CONF_KERNEL - 1) * BATCH
SCONV_HIST = (SCONV_KERNEL - 1) * BATCH
ROUTER_ROWS = 40
MOD_SHIFT1, MOD_SCALE1, MOD_GATE1, MOD_SHIFT2, MOD_SCALE2, MOD_GATE2 = range(6)

ADA_COLS = 1536
INPROJ_ROWS = 1024
SEQ_ROWS = 512
CONV_ROWS = 64
EW_ROWS = 32
MERGE_ROWS = 512
ROUTE_ROWS = 512
DISPATCH_ROWS = 512
EXPERT_ROWS = 512
ZERO_ROWS = 64
COMBINE_ROWS = 256


def _dot(a, b):
    return jnp.dot(a, b, preferred_element_type=F32)


def _split_bf16(v):
    hi = v.astype(BF16)
    lo = (v - hi.astype(F32)).astype(BF16)
    return hi, lo


def _sigmoid(v):
    return 1.0 / (1.0 + jnp.exp(-v))


def _gelu_tanh(v):
    return 0.5 * v * (1.0 + jnp.tanh(math.sqrt(2.0 / math.pi) * (v + 0.044715 * (v * v * v))))


def _per_batch(v, fn):
    rows, d = v.shape
    return fn(v.reshape(rows // BATCH, BATCH, d)).reshape(rows, d)


def _load_token_tiles(ref, first_token, n_tokens):
    base = first_token * SUBLANES
    chunks = [ref[pl.ds(base + s, n_tokens, stride=SUBLANES), :] for s in range(SUBLANES)]
    return jnp.concatenate(chunks, axis=1)


def _store_token_tiles(ref, value):
    for s in range(SUBLANES):
        ref[pl.ds(s, value.shape[0], stride=SUBLANES), :] = value[:, s * LANES:(s + 1) * LANES]


def _token_tile(ref, token):
    return ref.at[pl.ds(pl.multiple_of(token * SUBLANES, SUBLANES), SUBLANES)]


def _layer_spec(arr, l, col_block=None):
    tail = arr.shape[1:]
    if col_block is None:
        zeros = (0,) * len(tail)
        return pl.BlockSpec((None,) + tail, lambda *_: (l,) + zeros)
    return pl.BlockSpec((None, tail[0], D_MODEL), lambda *_: (l, 0, col_block))


def _norm_modulate(x, g, scale, shift):
    ms = jnp.mean(x * x, axis=-1, keepdims=True)
    h = x * lax.rsqrt(ms + NORM_EPS) * g
    return _per_batch(h, lambda h3: h3 * (1.0 + scale) + shift)


def _ada_kernel(c_ref, w_ref, b_ref, o_ref):
    c = c_ref[...]
    cond = c * _sigmoid(c)
    chi, clo = _split_bf16(cond)
    whi, wlo = _split_bf16(w_ref[...])
    o_ref[...] = _dot(chi, whi) + _dot(chi, wlo) + _dot(clo, whi) + b_ref[...]


def _ada(c, w_ada, b_ada):
    depth, d, n = w_ada.shape
    return pl.pallas_call(
        _ada_kernel,
        out_shape=jax.ShapeDtypeStruct((depth, BATCH, n), F32),
        grid=(depth, n // ADA_COLS),
        in_specs=[
            pl.BlockSpec((BATCH, d), lambda l, j: (0, 0)),
            pl.BlockSpec((None, d, ADA_COLS), lambda l, j: (l, 0, j)),
            pl.BlockSpec((None, 1, ADA_COLS), lambda l, j: (l, 0, j)),
        ],
        out_specs=pl.BlockSpec((None, BATCH, ADA_COLS), lambda l, j: (l, 0, j)),
        compiler_params=pltpu.CompilerParams(
            dimension_semantics=("arbitrary", "arbitrary"), vmem_limit_bytes=VMEM_LIMIT_BYTES),
        name="ada_modulation",
    )(c, w_ada, b_ada.reshape(depth, 1, n))


def _inproj_kernel(x_ref, g_ref, sc_ref, sh_ref, w_ref, o_ref):
    h = _norm_modulate(x_ref[...], g_ref[...], sc_ref[...], sh_ref[...])
    o_ref[...] = _dot(h.astype(BF16), w_ref[...]).astype(o_ref.dtype)


def _inproj(l, x, g, mod, w_in):
    t, d = x.shape
    n = w_in.shape[2]
    return pl.pallas_call(
        _inproj_kernel,
        out_shape=jax.ShapeDtypeStruct((t, n), BF16),
        grid=(t // INPROJ_ROWS,),
        in_specs=[
            pl.BlockSpec((INPROJ_ROWS, d), lambda i: (i, 0)),
            _layer_spec(g, l),
            _layer_spec(mod, l, MOD_SCALE1),
            _layer_spec(mod, l, MOD_SHIFT1),
            _layer_spec(w_in, l),
        ],
        out_specs=pl.BlockSpec((INPROJ_ROWS, n), lambda i: (i, 0)),
        compiler_params=pltpu.CompilerParams(
            dimension_semantics=("arbitrary",), vmem_limit_bytes=VMEM_LIMIT_BYTES),
        name="mixer_inproj",
    )(x, g, mod, mod, w_in)


def _seq_kernel(p_ref, wb_ref, wc_ref, are_ref, aim_ref, d_ref, wglu_ref, cw_ref, cb_ref,
                lng_ref, lnb_ref, sw_ref, o_ref, bu_ref, st_ref, zbuf_ref, cbuf_ref, qbuf_ref):
    rows = p_ref.shape[0]
    steps = rows // BATCH
    w = BRANCH_WIDTH

    @pl.when(pl.program_id(0) == 0)
    def _():
        st_ref[...] = jnp.zeros_like(st_ref)
        zbuf_ref[0:CONF_HIST, :] = jnp.zeros((CONF_HIST, w), F32)
        qbuf_ref[0:SCONV_HIST, :] = jnp.zeros((SCONV_HIST, w), F32)

    chunk_groups = EW_ROWS // SUBLANES

    def pointwise_chunk(j, carry):
        r0 = pl.multiple_of(j * EW_ROWS, EW_ROWS)
        rs = pl.ds(r0, EW_ROWS)
        gate_c = p_ref[rs, 4 * w:5 * w].astype(F32)
        hv = p_ref[rs, 5 * w:6 * w].astype(F32)
        qbuf_ref[pl.ds(r0 + SCONV_HIST, EW_ROWS), :] = gate_c * hv
        q3 = qbuf_ref[pl.ds(r0, EW_ROWS + SCONV_HIST), :].reshape(
            chunk_groups + SCONV_KERNEL - 1, SUBLANES, w)
        acc = q3[0:chunk_groups] * sw_ref[0:SUBLANES, :]
        for k in range(1, SCONV_KERNEL):
            acc = acc + q3[k:k + chunk_groups] * sw_ref[k * SUBLANES:(k + 1) * SUBLANES, :]
        gate_b = p_ref[rs, 3 * w:4 * w].astype(F32)
        o_ref[rs, 2 * w:3 * w] = (gate_b * acc.reshape(EW_ROWS, w)).astype(o_ref.dtype)
        v = p_ref[rs, w:2 * w].astype(F32)
        g = p_ref[rs, 2 * w:3 * w].astype(F32)
        zbuf_ref[pl.ds(r0 + CONF_HIST, EW_ROWS), :] = v * _sigmoid(g)
        return carry

    lax.fori_loop(0, rows // EW_ROWS, pointwise_chunk, 0)
    qbuf_ref[0:SCONV_HIST, :] = qbuf_ref[rows:rows + SCONV_HIST, :]

    ys = []
    for c in range(SSM_CHUNKS):
        u_c = p_ref[:, c * SSM_CHUNK_IN:(c + 1) * SSM_CHUNK_IN]
        bu_ref[...] = _dot(u_c, wb_ref[c])
        a_re = are_ref[c]
        a_im = aim_ref[c]

        def step(t, carry):
            x_re, x_im = carry
            r0 = pl.multiple_of(t * BATCH, BATCH)
            n_re = a_re * x_re - a_im * x_im + bu_ref[pl.ds(r0, BATCH), 0:SSM_CHUNK_STATE]
            n_im = a_re * x_im + a_im * x_re + bu_ref[pl.ds(r0, BATCH), SSM_CHUNK_STATE:]
            bu_ref[pl.ds(r0, BATCH), 0:SSM_CHUNK_STATE] = n_re
            bu_ref[pl.ds(r0, BATCH), SSM_CHUNK_STATE:] = n_im
            return n_re, n_im

        x_re, x_im = lax.fori_loop(0, steps, step, (st_ref[c, 0], st_ref[c, 1]), unroll=4)
        st_ref[c, 0] = x_re
        st_ref[c, 1] = x_im
        ys.append(_dot(bu_ref[...].astype(BF16), wc_ref[c]))
    u = p_ref[:, 0:w].astype(F32)
    y = _gelu_tanh(jnp.concatenate(ys, axis=1) + d_ref[...] * u)
    y_ssm = y * _sigmoid(_dot(y.astype(BF16), wglu_ref[...]))
    o_ref[:, 0:w] = y_ssm.astype(o_ref.dtype)

    win_groups = (CONV_ROWS + CONF_HIST) // SUBLANES
    out_groups = CONV_ROWS // SUBLANES

    def conv_chunk(j, carry):
        r0 = pl.multiple_of(j * CONV_ROWS, CONV_ROWS)
        for s in range(w // LANES):
            ls = slice(s * LANES, (s + 1) * LANES)
            win = zbuf_ref[pl.ds(r0, CONV_ROWS + CONF_HIST), ls].reshape(win_groups, SUBLANES, LANES)
            acc = win[0:out_groups] * cw_ref[0:SUBLANES, ls]
            for k in range(1, CONF_KERNEL):
                acc = acc + win[k:k + out_groups] * cw_ref[k * SUBLANES:(k + 1) * SUBLANES, ls]
            cbuf_ref[pl.ds(r0, CONV_ROWS), ls] = acc.reshape(CONV_ROWS, LANES)
        return carry

    lax.fori_loop(0, rows // CONV_ROWS, conv_chunk, 0)
    zbuf_ref[0:CONF_HIST, :] = zbuf_ref[rows:rows + CONF_HIST, :]
    cv = cbuf_ref[...] + cb_ref[...]
    cc = cv - jnp.mean(cv, axis=-1, keepdims=True)
    ln = cc * lax.rsqrt(jnp.mean(cc * cc, axis=-1, keepdims=True) + NORM_EPS) * lng_ref[...] + lnb_ref[...]
    o_ref[:, w:2 * w] = (ln * _sigmoid(ln)).astype(o_ref.dtype)


def _seq_mixers(l, proj, *layer_params):
    t = proj.shape[0]
    w = BRANCH_WIDTH
    return pl.pallas_call(
        _seq_kernel,
        out_shape=jax.ShapeDtypeStruct((t, N_BRANCH * w), BF16),
        grid=(t // SEQ_ROWS,),
        in_specs=[pl.BlockSpec((SEQ_ROWS, N_IN_CHUNKS * w), lambda i: (i, 0))]
                 + [_layer_spec(p, l) for p in layer_params],
        out_specs=pl.BlockSpec((SEQ_ROWS, N_BRANCH * w), lambda i: (i, 0)),
        scratch_shapes=[
            pltpu.VMEM((SEQ_ROWS, 2 * SSM_CHUNK_STATE), F32),
            pltpu.VMEM((SSM_CHUNKS, 2, BATCH, SSM_CHUNK_STATE), F32),
            pltpu.VMEM((SEQ_ROWS + CONF_HIST, w), F32),
            pltpu.VMEM((SEQ_ROWS, w), F32),
            pltpu.VMEM((SEQ_ROWS + SCONV_HIST, w), F32),
        ],
        compiler_params=pltpu.CompilerParams(
            dimension_semantics=("arbitrary",), vmem_limit_bytes=VMEM_LIMIT_BYTES),
        name="seq_mixers",
    )(proj, *layer_params)


def _merge_kernel(x_ref, g_ref, sc_ref, sh_ref, g1_ref, y_ref, wg_ref, bg_ref, wbr_ref, wo_ref, o_ref):
    x = x_ref[...]
    d = x.shape[1]
    w = BRANCH_WIDTH
    hb = _norm_modulate(x, g_ref[...], sc_ref[...], sh_ref[...]).astype(BF16)
    merged = None
    for n in range(N_BRANCH):
        gate = _sigmoid(_dot(hb, wg_ref[:, n * d:(n + 1) * d]) + bg_ref[:, n * d:(n + 1) * d])
        term = gate * _dot(y_ref[:, n * w:(n + 1) * w], wbr_ref[n])
        merged = term if merged is None else merged + term
    out = _dot(merged.astype(BF16), wo_ref[...])
    o_ref[...] = x + _per_batch(out, lambda o3: o3 * g1_ref[...])


def _merge(l, x, g, mod, ycat, w_gate, b_gate, w_branch, w_out):
    t, d = x.shape
    w = BRANCH_WIDTH
    return pl.pallas_call(
        _merge_kernel,
        out_shape=jax.ShapeDtypeStruct((t, d), F32),
        grid=(t // MERGE_ROWS,),
        in_specs=[
            pl.BlockSpec((MERGE_ROWS, d), lambda i: (i, 0)),
            _layer_spec(g, l),
            _layer_spec(mod, l, MOD_SCALE1),
            _layer_spec(mod, l, MOD_SHIFT1),
            _layer_spec(mod, l, MOD_GATE1),
            pl.BlockSpec((MERGE_ROWS, N_BRANCH * w), lambda i: (i, 0)),
            _layer_spec(w_gate, l),
            _layer_spec(b_gate, l),
            _layer_spec(w_branch, l),
            _layer_spec(w_out, l),
        ],
        out_specs=pl.BlockSpec((MERGE_ROWS, d), lambda i: (i, 0)),
        compiler_params=pltpu.CompilerParams(
            dimension_semantics=("arbitrary",), vmem_limit_bytes=VMEM_LIMIT_BYTES),
        name="mixer_merge",
    )(x, g, mod, mod, mod, ycat, w_gate, b_gate, w_branch, w_out)


def _router_kernel(x_ref, g_ref, sc_ref, sh_ref, whi_ref, wlo_ref, b_ref, tri_ref,
                   h_ref, route_ref, cnt_ref, run_ref):
    @pl.when(pl.program_id(0) == 0)
    def _():
        run_ref[...] = jnp.zeros_like(run_ref)

    h = _norm_modulate(x_ref[...], g_ref[...], sc_ref[...], sh_ref[...])
    _store_token_tiles(h_ref, h)
    rows = h.shape[0]
    hhi, hlo = _split_bf16(h)
    contract_last = (((1,), (1,)), ((), ()))
    dott = lambda a, b: lax.dot_general(a, b, contract_last, preferred_element_type=F32)
    logits = dott(whi_ref[...], hhi) + dott(wlo_ref[...], hhi) + dott(whi_ref[...], hlo) + b_ref[...]
    lg = logits[0:N_GROUPS]
    le = logits[SUBLANES:SUBLANES + N_EXPERTS]

    g_row = lax.broadcasted_iota(jnp.int32, (N_GROUPS, rows), 0).astype(F32)
    g_max = jnp.max(lg, axis=0, keepdims=True)
    g_sel = jnp.min(jnp.where(lg == g_max, g_row, float(N_GROUPS)), axis=0, keepdims=True)
    g_w = 1.0 / jnp.sum(jnp.exp(lg - g_max), axis=0, keepdims=True)

    e_row = lax.broadcasted_iota(jnp.int32, (N_EXPERTS, rows), 0).astype(F32)
    e_grp = jnp.floor(e_row * (1.0 / EXPERTS_PER_GROUP))
    neg = float("-inf")
    lm = jnp.where(e_grp == g_sel, le, neg)
    v1 = jnp.max(lm, axis=0, keepdims=True)
    i1 = jnp.min(jnp.where(lm == v1, e_row, float(N_EXPERTS)), axis=0, keepdims=True)
    lm2 = jnp.where(e_row == i1, neg, lm)
    v2 = jnp.max(lm2, axis=0, keepdims=True)
    i2 = jnp.min(jnp.where(lm2 == v2, e_row, float(N_EXPERTS)), axis=0, keepdims=True)
    e21 = jnp.exp(v2 - v1)
    p1 = 1.0 / (1.0 + e21)
    w1 = p1 * g_w
    w2 = e21 * p1 * g_w

    hit1 = e_row == i1
    hit2 = e_row == i2
    onehot = jnp.where(hit1 | hit2, 1.0, 0.0)
    cum = _dot(onehot.astype(BF16), tri_ref[...])
    before = cum - onehot + run_ref[...]
    r1 = jnp.sum(jnp.where(hit1, before, 0.0), axis=0, keepdims=True)
    r2 = jnp.sum(jnp.where(hit2, before, 0.0), axis=0, keepdims=True)
    run_ref[...] = run_ref[...] + jnp.broadcast_to(cum[:, rows - 1:rows], run_ref.shape)
    cnt_ref[...] = run_ref[...]
    zero = jnp.zeros_like(w1)
    route_ref[...] = jnp.concatenate([i1, i2, r1, r2, w1, w2, zero, zero], axis=0)


def _router(l, x, g, mod, w_hi, w_lo, bias, tri):
    t, d = x.shape
    return pl.pallas_call(
        _router_kernel,
        out_shape=(
            jax.ShapeDtypeStruct((t * SUBLANES, LANES), F32),
            jax.ShapeDtypeStruct((SUBLANES, t), F32),
            jax.ShapeDtypeStruct((N_EXPERTS, ROUTE_ROWS), F32),
        ),
        grid=(t // ROUTE_ROWS,),
        in_specs=[
            pl.BlockSpec((ROUTE_ROWS, d), lambda i: (i, 0)),
            _layer_spec(g, l),
            _layer_spec(mod, l, MOD_SCALE2),
            _layer_spec(mod, l, MOD_SHIFT2),
            _layer_spec(w_hi, l),
            _layer_spec(w_lo, l),
            _layer_spec(bias, l),
            pl.BlockSpec((ROUTE_ROWS, ROUTE_ROWS), lambda i: (0, 0)),
        ],
        out_specs=(
            pl.BlockSpec((ROUTE_ROWS * SUBLANES, LANES), lambda i: (i, 0)),
            pl.BlockSpec((SUBLANES, ROUTE_ROWS), lambda i: (0, i)),
            pl.BlockSpec((N_EXPERTS, ROUTE_ROWS), lambda i: (0, 0)),
        ),
        scratch_shapes=[pltpu.VMEM((N_EXPERTS, ROUTE_ROWS), F32)],
        compiler_params=pltpu.CompilerParams(
            dimension_semantics=("arbitrary",), vmem_limit_bytes=VMEM_LIMIT_BYTES),
        name="moe_router",
    )(x, g, mod, mod, w_hi, w_lo, bias, tri)


def _row_copy(src_hbm, src_token, dst_hbm, dst_token, sem):
    return pltpu.make_async_copy(_token_tile(src_hbm, src_token), _token_tile(dst_hbm, dst_token), sem)


def _dispatch_kernel(zstart_ref, dest_hbm, h_ref, xs_hbm, idx0_smem, idx1_smem, zeros_ref,
                     idx_sem, zero_sem, row_sem):
    i = pl.program_id(0)
    n_tiles = pl.num_programs(0)
    idx_smem = (idx0_smem, idx1_smem)
    n_idx = idx0_smem.shape[0]
    tile = n_idx // TOP_K

    def idx_copy(tile_index, s):
        src = dest_hbm.at[pl.ds(pl.multiple_of(tile_index * n_idx, n_idx), n_idx)]
        return pltpu.make_async_copy(src, idx_smem[s], idx_sem.at[s])

    @pl.when(i == 0)
    def _():
        idx_copy(0, 0).start()

    def zero_copy(j):
        piece_rows = ZERO_ROWS * SUBLANES
        start = pl.multiple_of(zstart_ref[j] * SUBLANES, piece_rows)
        return pltpu.make_async_copy(zeros_ref, xs_hbm.at[pl.ds(start, piece_rows)], zero_sem)

    @pl.when(i == 0)
    def _():
        zeros_ref[...] = jnp.zeros_like(zeros_ref)

        def start(j, carry):
            @pl.when(zstart_ref[j] >= 0)
            def _():
                zero_copy(j).start()
            return carry

        def wait(j, carry):
            @pl.when(zstart_ref[j] >= 0)
            def _():
                zero_copy(j).wait()
            return carry

        lax.fori_loop(0, zstart_ref.shape[0], start, 0)
        lax.fori_loop(0, zstart_ref.shape[0], wait, 0)

    def step(slot):
        @pl.when(i + 1 < n_tiles)
        def _():
            idx_copy(i + 1, 1 - slot).start()

        idx_copy(i, slot).wait()

        def start_rows(t, carry):
            for k in range(TOP_K):
                _row_copy(h_ref, t, xs_hbm, idx_smem[slot][k * tile + t], row_sem).start(priority=k)
            return carry

        lax.fori_loop(0, tile, start_rows, 0, unroll=8)
        for k in range(TOP_K):
            pltpu.make_async_copy(h_ref, xs_hbm.at[pl.ds(0, tile * SUBLANES)], row_sem).wait()

    for slot in range(2):
        pl.when(lax.rem(i, 2) == slot)(functools.partial(step, slot))


def _dispatch(zstart, dest_tiles, h, cap):
    t = h.shape[0] // SUBLANES
    any_spec = pl.BlockSpec(memory_space=pl.ANY)
    return pl.pallas_call(
        _dispatch_kernel,
        out_shape=jax.ShapeDtypeStruct((cap * SUBLANES, LANES), h.dtype),
        grid_spec=pltpu.PrefetchScalarGridSpec(
            num_scalar_prefetch=1,
            grid=(t // DISPATCH_ROWS,),
            in_specs=[any_spec,
                      pl.BlockSpec((DISPATCH_ROWS * SUBLANES, LANES), lambda i, zs: (i, 0))],
            out_specs=any_spec,
            scratch_shapes=[
                pltpu.SMEM((TOP_K * DISPATCH_ROWS,), jnp.int32),
                pltpu.SMEM((TOP_K * DISPATCH_ROWS,), jnp.int32),
                pltpu.VMEM((ZERO_ROWS * SUBLANES, LANES), h.dtype),
                pltpu.SemaphoreType.DMA((2,)),
                pltpu.SemaphoreType.DMA,
                pltpu.SemaphoreType.DMA,
            ],
        ),
        compiler_params=pltpu.CompilerParams(
            dimension_semantics=("arbitrary",), has_side_effects=True),
        name="moe_dispatch",
    )(zstart, dest_tiles, h)


def _expert_kernel(first_ref, nblk_ref, total_ref, xs_hbm, w1_ref, w3_ref, w2_ref, ys_hbm,
                   w1b_ref, w3b_ref, w2b_ref, xbuf_ref, ybuf_ref, in_sem, out_sem):
    e = pl.program_id(0)
    n_blocks = nblk_ref[e]
    first = first_ref[e]
    total = total_ref[0]
    block_rows = EXPERT_ROWS * SUBLANES

    def rows_of(block):
        return pl.ds(pl.multiple_of(block * block_rows, block_rows), block_rows)

    def in_copy(block, slot):
        return pltpu.make_async_copy(xs_hbm.at[rows_of(block)], xbuf_ref.at[slot], in_sem.at[slot])

    def out_copy(block, slot):
        return pltpu.make_async_copy(ybuf_ref.at[slot], ys_hbm.at[rows_of(block)], out_sem.at[slot])

    @pl.when(n_blocks > 0)
    def _():
        @pl.when(first == 0)
        def _():
            in_copy(0, 0).start()

        w1b_ref[...] = w1_ref[...].astype(BF16)
        w3b_ref[...] = w3_ref[...].astype(BF16)
        w2b_ref[...] = w2_ref[...].astype(BF16)

        def block_step(j, carry):
            g = first + j
            slot = lax.rem(g, 2)

            @pl.when(g + 1 < total)
            def _():
                in_copy(g + 1, 1 - slot).start(priority=1)

            in_copy(g, slot).wait()

            @pl.when(g >= 2)
            def _():
                out_copy(g - 2, slot).wait()

            xb = _load_token_tiles(xbuf_ref.at[slot], 0, EXPERT_ROWS).astype(BF16)
            a = _dot(xb, w1b_ref[...])
            hid = a * _sigmoid(a) * _dot(xb, w3b_ref[...])
            _store_token_tiles(ybuf_ref.at[slot], _dot(hid.astype(BF16), w2b_ref[...]))
            out_copy(g, slot).start(priority=1)
            return carry

        lax.fori_loop(0, n_blocks, block_step, 0)

        @pl.when(first + n_blocks == total)
        def _():
            @pl.when(total >= 2)
            def _():
                out_copy(total - 2, lax.rem(total, 2)).wait()

            out_copy(total - 1, lax.rem(total - 1, 2)).wait()


def _experts(l, first_block, n_blocks, total_blocks, xs, w1, w3, w2):
    _, n_experts, d, f = w1.shape
    tile = (EXPERT_ROWS * SUBLANES, LANES)
    any_spec = pl.BlockSpec(memory_space=pl.ANY)
    weight_spec = lambda rows, cols: pl.BlockSpec((None, None, rows, cols), lambda e, *_: (l, e, 0, 0))
    return pl.pallas_call(
        _expert_kernel,
        out_shape=jax.ShapeDtypeStruct(xs.shape, F32),
        grid_spec=pltpu.PrefetchScalarGridSpec(
            num_scalar_prefetch=3,
            grid=(n_experts,),
            in_specs=[any_spec, weight_spec(d, f), weight_spec(d, f), weight_spec(f, d)],
            out_specs=any_spec,
            scratch_shapes=[
                pltpu.VMEM((d, f), BF16), pltpu.VMEM((d, f), BF16), pltpu.VMEM((f, d), BF16),
                pltpu.VMEM((2,) + tile, F32), pltpu.VMEM((2,) + tile, F32),
                pltpu.SemaphoreType.DMA((2,)), pltpu.SemaphoreType.DMA((2,)),
            ],
        ),
        compiler_params=pltpu.CompilerParams(
            dimension_semantics=("arbitrary",), vmem_limit_bytes=VMEM_LIMIT_BYTES),
        name="moe_experts",
    )(first_block, n_blocks, total_blocks, xs, w1, w3, w2)


def _combine_kernel(final_norm, dest_hbm, ys_hbm, x_ref, g2_ref, wt_ref, fg_ref, o_ref,
                    idx0_smem, idx1_smem, ybuf_ref, idx_sem, row_sem):
    i = pl.program_id(0)
    n_tiles = pl.num_programs(0)
    idx_smem = (idx0_smem, idx1_smem)
    n_idx = idx0_smem.shape[0]
    tile = n_idx // TOP_K

    def idx_copy(tile_index, s):
        src = dest_hbm.at[pl.ds(pl.multiple_of(tile_index * n_idx, n_idx), n_idx)]
        return pltpu.make_async_copy(src, idx_smem[s], idx_sem.at[s])

    def start_rows(s):
        def body(t, carry):
            for k in range(TOP_K):
                row = k * tile + t
                pltpu.make_async_copy(_token_tile(ys_hbm, idx_smem[s][row]),
                                      _token_tile(ybuf_ref.at[s], row), row_sem.at[s]).start(priority=k)
            return carry
        lax.fori_loop(0, tile, body, 0, unroll=8)

    def wait_rows(s):
        pltpu.make_async_copy(ybuf_ref.at[s], ybuf_ref.at[s], row_sem.at[s]).wait()

    @pl.when(i == 0)
    def _():
        idx_copy(0, 0).start()
        idx_copy(0, 0).wait()
        start_rows(0)

        @pl.when(n_tiles > 1)
        def _():
            idx_copy(1, 1).start()

    def step(slot):
        other = 1 - slot

        @pl.when(i + 1 < n_tiles)
        def _():
            idx_copy(i + 1, other).wait()
            start_rows(other)

        @pl.when(i + 2 < n_tiles)
        def _():
            idx_copy(i + 2, slot).start()

        wait_rows(slot)
        wt = wt_ref[...]
        ytiles = ybuf_ref.at[slot]
        y = (wt[:, 0:1] * _load_token_tiles(ytiles, 0, tile)
             + wt[:, 1:2] * _load_token_tiles(ytiles, tile, tile))
        out = x_ref[...] + _per_batch(y, lambda y3: y3 * g2_ref[...])
        if final_norm:
            ms = jnp.mean(out * out, axis=-1, keepdims=True)
            out = out * lax.rsqrt(ms + NORM_EPS) * fg_ref[...]
        o_ref[...] = out

    for slot in range(2):
        pl.when(lax.rem(i, 2) == slot)(functools.partial(step, slot))


def _combine(l, dest_tiles, ys, x, mod, wt, final_g, final_norm):
    t, d = x.shape
    any_spec = pl.BlockSpec(memory_space=pl.ANY)
    return pl.pallas_call(
        functools.partial(_combine_kernel, final_norm),
        out_shape=jax.ShapeDtypeStruct((t, d), F32),
        grid=(t // COMBINE_ROWS,),
        in_specs=[
            any_spec,
            any_spec,
            pl.BlockSpec((COMBINE_ROWS, d), lambda i: (i, 0)),
            _layer_spec(mod, l, MOD_GATE2),
            pl.BlockSpec((COMBINE_ROWS, TOP_K), lambda i: (i, 0)),
            pl.BlockSpec((1, d), lambda i: (0, 0)),
        ],
        out_specs=pl.BlockSpec((COMBINE_ROWS, d), lambda i: (i, 0)),
        scratch_shapes=[
            pltpu.SMEM((TOP_K * COMBINE_ROWS,), jnp.int32),
            pltpu.SMEM((TOP_K * COMBINE_ROWS,), jnp.int32),
            pltpu.VMEM((2, TOP_K * COMBINE_ROWS * SUBLANES, LANES), F32),
            pltpu.SemaphoreType.DMA((2,)),
            pltpu.SemaphoreType.DMA((2,)),
        ],
        compiler_params=pltpu.CompilerParams(
            dimension_semantics=("arbitrary",), vmem_limit_bytes=VMEM_LIMIT_BYTES),
        name="moe_combine",
    )(dest_tiles, ys, x, mod, wt, final_g)


def _ssm_params(lam_re, lam_im, log_step, b_re, b_im, c_re, c_im):
    step = jnp.exp(log_step)[:, None]
    mag = jnp.exp(lam_re * step)
    a_re, a_im = mag * jnp.cos(lam_im * step), mag * jnp.sin(lam_im * step)
    den = lam_re * lam_re + lam_im * lam_im
    n_re, n_im = a_re - 1.0, a_im
    k_re = (n_re * lam_re + n_im * lam_im) / den
    k_im = (n_im * lam_re - n_re * lam_im) / den
    bb_re = k_re[..., None] * b_re - k_im[..., None] * b_im
    bb_im = k_re[..., None] * b_im + k_im[..., None] * b_re
    gpc = SSM_GROUPS // SSM_CHUNKS
    eye = jnp.eye(gpc, dtype=F32)

    def in_blocks(bb):
        bb = bb.reshape(SSM_CHUNKS, gpc, SSM_STATE, SSM_GROUP)
        return jnp.einsum("cgph,gk->cghkp", bb, eye).reshape(SSM_CHUNKS, SSM_CHUNK_IN, SSM_CHUNK_STATE)

    def out_blocks(cc):
        cc = cc.reshape(SSM_CHUNKS, gpc, SSM_GROUP, SSM_STATE)
        return jnp.einsum("cghp,gk->cgpkh", cc, eye).reshape(SSM_CHUNKS, SSM_CHUNK_STATE, SSM_CHUNK_IN)

    wb = jnp.concatenate([in_blocks(bb_re), in_blocks(bb_im)], axis=2).astype(BF16)
    wc = jnp.concatenate([out_blocks(c_re), out_blocks(-c_im)], axis=1).astype(BF16)
    bcast = lambda a: jnp.broadcast_to(a.reshape(SSM_CHUNKS, 1, SSM_CHUNK_STATE),
                                       (SSM_CHUNKS, BATCH, SSM_CHUNK_STATE))
    return wb, wc, bcast(a_re), bcast(a_im)


def _rows8(w):
    depth, k, c = w.shape
    return jnp.broadcast_to(w[:, :, None, :], (depth, k, SUBLANES, c)).reshape(depth, k * SUBLANES, c)


def _router_weights(w_rg, b_rg, w_re, b_re):
    depth, d, _ = w_rg.shape
    pad = SUBLANES - N_GROUPS
    w_t = jnp.concatenate([w_rg.transpose(0, 2, 1), jnp.zeros((depth, pad, d), F32),
                           w_re.transpose(0, 2, 1)], axis=1)
    b = jnp.concatenate([b_rg, jnp.zeros((depth, pad), F32), b_re], axis=1)
    hi, lo = _split_bf16(w_t)
    return hi, lo, jnp.broadcast_to(b[:, :, None], (depth, ROUTER_ROWS, ROUTE_ROWS))


def _routing_tables(route, counts):
    t = route.shape[1]
    counts = counts.astype(jnp.int32)
    padded = (counts + EXPERT_ROWS - 1) // EXPERT_ROWS * EXPERT_ROWS
    pad_end = jnp.cumsum(padded)
    pad_start = pad_end - padded
    experts = route[0:TOP_K].astype(jnp.int32)
    ranks = route[TOP_K:2 * TOP_K].astype(jnp.int32)
    expert_ids = jnp.arange(N_EXPERTS, dtype=jnp.int32)
    is_expert = experts[:, :, None] == expert_ids
    dest = jnp.sum(jnp.where(is_expert, pad_start, 0), axis=-1) + ranks

    def tiles(rows_per_tile):
        n = t // rows_per_tile
        return dest.reshape(TOP_K, n, rows_per_tile).transpose(1, 0, 2).reshape(-1)

    capacity = TOP_K * t + N_EXPERTS * EXPERT_ROWS
    first_block = pad_start // EXPERT_ROWS
    n_blocks = padded // EXPERT_ROWS
    last_start = pad_end - EXPERT_ROWS
    last_valid = counts - (padded - EXPERT_ROWS)
    piece = jnp.arange(EXPERT_ROWS // ZERO_ROWS, dtype=jnp.int32)
    has_pad = (counts[:, None] > 0) & ((piece[None, :] + 1) * ZERO_ROWS > last_valid[:, None])
    zstart = jnp.where(has_pad, last_start[:, None] + piece[None, :] * ZERO_ROWS, -1).reshape(-1)
    weights = route[2 * TOP_K:3 * TOP_K].T
    total_blocks = (pad_end[-1] // EXPERT_ROWS).reshape(1)
    return (tiles(DISPATCH_ROWS), tiles(COMBINE_ROWS), first_block.astype(jnp.int32), n_blocks.astype(jnp.int32),
            total_blocks.astype(jnp.int32), zstart.astype(jnp.int32), weights, capacity)


def kernel(x, c, norm_mix_g, norm_ffn_g, w_ada, b_ada, w_in, lam_re, lam_im, log_step, ssm_b_re, ssm_b_im, ssm_c_re, ssm_c_im, ssm_d, w_glu, conf_dw_w, conf_dw_b, conf_ln_g, conf_ln_b, sconv_w, w_branch, w_gate, b_gate, w_out, w_router_group, b_router_group, w_router_expert, b_router_expert, w_exp_gate, w_exp_up, w_exp_down, final_norm_g):
    bsz, seq, d = x.shape
    assert bsz == BATCH and d == D_MODEL
    depth = w_in.shape[0]
    t = bsz * seq
    rows = lambda v: v.reshape(depth, 1, -1)

    mix_g, ffn_g = rows(norm_mix_g), rows(norm_ffn_g)
    seq_params = (*jax.vmap(_ssm_params)(lam_re, lam_im, log_step, ssm_b_re, ssm_b_im, ssm_c_re, ssm_c_im),
                  rows(ssm_d), w_glu.astype(BF16), _rows8(conf_dw_w), rows(conf_dw_b), rows(conf_ln_g),
                  rows(conf_ln_b), _rows8(sconv_w))
    w_in_b, w_gate_b, w_branch_b, w_out_b = (w.astype(BF16) for w in (w_in, w_gate, w_branch, w_out))
    r_hi, r_lo, r_b = _router_weights(w_router_group, b_router_group, w_router_expert, b_router_expert)
    tri = jnp.triu(jnp.ones((ROUTE_ROWS, ROUTE_ROWS), BF16))

    mod = _ada(c, w_ada, b_ada)
    xt = jnp.transpose(x, (1, 0, 2)).reshape(t, d)

    for l in range(depth):
        proj = _inproj(l, xt, mix_g, mod, w_in_b)
        ycat = _seq_mixers(l, proj, *seq_params)
        xt = _merge(l, xt, mix_g, mod, ycat, w_gate_b, rows(b_gate), w_branch_b, w_out_b)

        h2, route, counts = _router(l, xt, ffn_g, mod, r_hi, r_lo, r_b, tri)
        (dest_d, dest_c, first_block, n_blocks, total_blocks, zstart, weights,
         cap) = _routing_tables(route, counts[:, 0])
        xs = _dispatch(zstart, dest_d, h2, cap)
        ys = _experts(l, first_block, n_blocks, total_blocks, xs, w_exp_gate, w_exp_up, w_exp_down)
        xt = _combine(l, dest_c, ys, xt, mod, weights, final_norm_g.reshape(1, d), l == depth - 1)

    return jnp.transpose(xt.reshape(seq, bsz, d), (1, 0, 2))
```

```python
import functools
import math

import jax
import jax.numpy as jnp
from jax import lax
from jax.experimental import pallas as pl
from jax.experimental.pallas import tpu as pltpu

F32 = jnp.float32
BF16 = jnp.bfloat16

D_MODEL = 1024
BATCH = 8
BRANCH_WIDTH = D_MODEL // 2
N_BRANCH = 3
N_IN_CHUNKS = 6
SSM_GROUP = 16
SSM_GROUPS = BRANCH_WIDTH // SSM_GROUP
SSM_STATE = 64
CONF_KERNEL = 31
SCONV_KERNEL = 3
N_GROUPS = 4
EXPERTS_PER_GROUP = 8
N_EXPERTS = N_GROUPS * EXPERTS_PER_GROUP
TOP_K = 2
D_FF_EXPERT = D_MODEL // 2
NORM_EPS = 1e-6

SUBLANES = 8
LANES = 128
VMEM_LIMIT_BYTES = 56 * 1024 * 1024

SSM_CHUNKS = 2
SSM_CHUNK_IN = BRANCH_WIDTH // SSM_CHUNKS
SSM_CHUNK_STATE = SSM_GROUPS * SSM_STATE // SSM_CHUNKS
CONF_HIST = (CONF_KERNEL - 1) * BATCH
SCONV_HIST = (SCONV_KERNEL - 1) * BATCH
ROUTER_ROWS = 40
MOD_SHIFT1, MOD_SCALE1, MOD_GATE1, MOD_SHIFT2, MOD_SCALE2, MOD_GATE2 = range(6)

ADA_COLS = 1536
INPROJ_ROWS = 1024
SEQ_ROWS = 512
CONV_ROWS = 64
EW_ROWS = 32
MERGE_ROWS = 512
ROUTE_ROWS = MERGE_ROWS
DISPATCH_ROWS = 512
EXPERT_ROWS = 512
ZERO_ROWS = 64
COMBINE_ROWS = 512


def _dot(a, b):
    return jnp.dot(a, b, preferred_element_type=F32)


def _split_bf16(v):
    hi = v.astype(BF16)
    lo = (v - hi.astype(F32)).astype(BF16)
    return hi, lo


def _sigmoid(v):
    return 1.0 / (1.0 + jnp.exp(-v))


def _gelu_tanh(v):
    return 0.5 * v * (1.0 + jnp.tanh(math.sqrt(2.0 / math.pi) * (v + 0.044715 * (v * v * v))))


def _per_batch(v, fn):
    rows, d = v.shape
    return fn(v.reshape(rows // BATCH, BATCH, d)).reshape(rows, d)


def _load_token_tiles(ref, first_token, n_tokens):
    base = first_token * SUBLANES
    chunks = [ref[pl.ds(base + s, n_tokens, stride=SUBLANES), :] for s in range(SUBLANES)]
    return jnp.concatenate(chunks, axis=1)


def _store_token_tiles(ref, value):
    for s in range(SUBLANES):
        ref[pl.ds(s, value.shape[0], stride=SUBLANES), :] = value[:, s * LANES:(s + 1) * LANES]


def _token_tile(ref, token):
    return ref.at[pl.ds(pl.multiple_of(token * SUBLANES, SUBLANES), SUBLANES)]


def _layer_spec(arr, l, col_block=None):
    tail = arr.shape[1:]
    if col_block is None:
        zeros = (0,) * len(tail)
        return pl.BlockSpec((None,) + tail, lambda *_: (l,) + zeros)
    return pl.BlockSpec((None, tail[0], D_MODEL), lambda *_: (l, 0, col_block))


def _norm_modulate(x, g, scale, shift):
    ms = jnp.mean(x * x, axis=-1, keepdims=True)
    h = x * lax.rsqrt(ms + NORM_EPS) * g
    return _per_batch(h, lambda h3: h3 * (1.0 + scale) + shift)


def _ada_kernel(c_ref, w_ref, b_ref, o_ref):
    c = c_ref[...]
    cond = c * _sigmoid(c)
    chi, clo = _split_bf16(cond)
    whi, wlo = _split_bf16(w_ref[...])
    o_ref[...] = _dot(chi, whi) + _dot(chi, wlo) + _dot(clo, whi) + b_ref[...]


def _ada(c, w_ada, b_ada):
    depth, d, n = w_ada.shape
    return pl.pallas_call(
        _ada_kernel,
        out_shape=jax.ShapeDtypeStruct((depth, BATCH, n), F32),
        grid=(depth, n // ADA_COLS),
        in_specs=[
            pl.BlockSpec((BATCH, d), lambda l, j: (0, 0)),
            pl.BlockSpec((None, d, ADA_COLS), lambda l, j: (l, 0, j)),
            pl.BlockSpec((None, 1, ADA_COLS), lambda l, j: (l, 0, j)),
        ],
        out_specs=pl.BlockSpec((None, BATCH, ADA_COLS), lambda l, j: (l, 0, j)),
        compiler_params=pltpu.CompilerParams(
            dimension_semantics=("arbitrary", "arbitrary"), vmem_limit_bytes=VMEM_LIMIT_BYTES),
        name="ada_modulation",
    )(c, w_ada, b_ada.reshape(depth, 1, n))


def _inproj_kernel(x_ref, g_ref, sc_ref, sh_ref, w_ref, o_ref):
    h = _norm_modulate(x_ref[...], g_ref[...], sc_ref[...], sh_ref[...])
    o_ref[...] = _dot(h.astype(BF16), w_ref[...]).astype(o_ref.dtype)


def _inproj(l, x, g, mod, w_in):
    t, d = x.shape
    n = w_in.shape[2]
    return pl.pallas_call(
        _inproj_kernel,
        out_shape=jax.ShapeDtypeStruct((t, n), BF16),
        grid=(t // INPROJ_ROWS,),
        in_specs=[
            pl.BlockSpec((INPROJ_ROWS, d), lambda i: (i, 0)),
            _layer_spec(g, l),
            _layer_spec(mod, l, MOD_SCALE1),
            _layer_spec(mod, l, MOD_SHIFT1),
            _layer_spec(w_in, l),
        ],
        out_specs=pl.BlockSpec((INPROJ_ROWS, n), lambda i: (i, 0)),
        compiler_params=pltpu.CompilerParams(
            dimension_semantics=("arbitrary",), vmem_limit_bytes=VMEM_LIMIT_BYTES),
        name="mixer_inproj",
    )(x, g, mod, mod, w_in)


def _seq_kernel(p_ref, wb_ref, wc_ref, are_ref, aim_ref, d_ref, wglu_ref, cw_ref, cb_ref,
                lng_ref, lnb_ref, sw_ref, o_ref, bu_ref, st_ref, zbuf_ref, cbuf_ref, qbuf_ref):
    rows = p_ref.shape[0]
    steps = rows // BATCH
    w = BRANCH_WIDTH

    @pl.when(pl.program_id(0) == 0)
    def _():
        st_ref[...] = jnp.zeros_like(st_ref)
        zbuf_ref[0:CONF_HIST, :] = jnp.zeros((CONF_HIST, w), F32)
        qbuf_ref[0:SCONV_HIST, :] = jnp.zeros((SCONV_HIST, w), F32)

    chunk_groups = EW_ROWS // SUBLANES

    def pointwise_chunk(j, carry):
        r0 = pl.multiple_of(j * EW_ROWS, EW_ROWS)
        rs = pl.ds(r0, EW_ROWS)
        gate_c = p_ref[rs, 4 * w:5 * w].astype(F32)
        hv = p_ref[rs, 5 * w:6 * w].astype(F32)
        qbuf_ref[pl.ds(r0 + SCONV_HIST, EW_ROWS), :] = gate_c * hv
        q3 = qbuf_ref[pl.ds(r0, EW_ROWS + SCONV_HIST), :].reshape(
            chunk_groups + SCONV_KERNEL - 1, SUBLANES, w)
        acc = q3[0:chunk_groups] * sw_ref[0:SUBLANES, :]
        for k in range(1, SCONV_KERNEL):
            acc = acc + q3[k:k + chunk_groups] * sw_ref[k * SUBLANES:(k + 1) * SUBLANES, :]
        gate_b = p_ref[rs, 3 * w:4 * w].astype(F32)
        o_ref[rs, 2 * w:3 * w] = (gate_b * acc.reshape(EW_ROWS, w)).astype(o_ref.dtype)
        v = p_ref[rs, w:2 * w].astype(F32)
        g = p_ref[rs, 2 * w:3 * w].astype(F32)
        zbuf_ref[pl.ds(r0 + CONF_HIST, EW_ROWS), :] = v * _sigmoid(g)
        return carry

    lax.fori_loop(0, rows // EW_ROWS, pointwise_chunk, 0)
    qbuf_ref[0:SCONV_HIST, :] = qbuf_ref[rows:rows + SCONV_HIST, :]

    ys = []
    for c in range(SSM_CHUNKS):
        u_c = p_ref[:, c * SSM_CHUNK_IN:(c + 1) * SSM_CHUNK_IN]
        bu_ref[...] = _dot(u_c, wb_ref[c])
        a_re = are_ref[c]
        a_im = aim_ref[c]

        def step(t, carry):
            x_re, x_im = carry
            r0 = pl.multiple_of(t * BATCH, BATCH)
            n_re = a_re * x_re - a_im * x_im + bu_ref[pl.ds(r0, BATCH), 0:SSM_CHUNK_STATE]
            n_im = a_re * x_im + a_im * x_re + bu_ref[pl.ds(r0, BATCH), SSM_CHUNK_STATE:]
            bu_ref[pl.ds(r0, BATCH), 0:SSM_CHUNK_STATE] = n_re
            bu_ref[pl.ds(r0, BATCH), SSM_CHUNK_STATE:] = n_im
            return n_re, n_im

        x_re, x_im = lax.fori_loop(0, steps, step, (st_ref[c, 0], st_ref[c, 1]), unroll=4)
        st_ref[c, 0] = x_re
        st_ref[c, 1] = x_im
        ys.append(_dot(bu_ref[...].astype(BF16), wc_ref[c]))
    u = p_ref[:, 0:w].astype(F32)
    y = _gelu_tanh(jnp.concatenate(ys, axis=1) + d_ref[...] * u)
    y_ssm = y * _sigmoid(_dot(y.astype(BF16), wglu_ref[...]))
    o_ref[:, 0:w] = y_ssm.astype(o_ref.dtype)

    win_groups = (CONV_ROWS + CONF_HIST) // SUBLANES
    out_groups = CONV_ROWS // SUBLANES

    def conv_chunk(j, carry):
        r0 = pl.multiple_of(j * CONV_ROWS, CONV_ROWS)
        for s in range(w // LANES):
            ls = slice(s * LANES, (s + 1) * LANES)
            win = zbuf_ref[pl.ds(r0, CONV_ROWS + CONF_HIST), ls].reshape(win_groups, SUBLANES, LANES)
            acc = win[0:out_groups] * cw_ref[0:SUBLANES, ls]
            for k in range(1, CONF_KERNEL):
                acc = acc + win[k:k + out_groups] * cw_ref[k * SUBLANES:(k + 1) * SUBLANES, ls]
            cbuf_ref[pl.ds(r0, CONV_ROWS), ls] = acc.reshape(CONV_ROWS, LANES)
        return carry

    lax.fori_loop(0, rows // CONV_ROWS, conv_chunk, 0)
    zbuf_ref[0:CONF_HIST, :] = zbuf_ref[rows:rows + CONF_HIST, :]
    cv = cbuf_ref[...] + cb_ref[...]
    cc = cv - jnp.mean(cv, axis=-1, keepdims=True)
    ln = cc * lax.rsqrt(jnp.mean(cc * cc, axis=-1, keepdims=True) + NORM_EPS) * lng_ref[...] + lnb_ref[...]
    o_ref[:, w:2 * w] = (ln * _sigmoid(ln)).astype(o_ref.dtype)


def _seq_mixers(l, proj, *layer_params):
    t = proj.shape[0]
    w = BRANCH_WIDTH
    return pl.pallas_call(
        _seq_kernel,
        out_shape=jax.ShapeDtypeStruct((t, N_BRANCH * w), BF16),
        grid=(t // SEQ_ROWS,),
        in_specs=[pl.BlockSpec((SEQ_ROWS, N_IN_CHUNKS * w), lambda i: (i, 0))]
                 + [_layer_spec(p, l) for p in layer_params],
        out_specs=pl.BlockSpec((SEQ_ROWS, N_BRANCH * w), lambda i: (i, 0)),
        scratch_shapes=[
            pltpu.VMEM((SEQ_ROWS, 2 * SSM_CHUNK_STATE), F32),
            pltpu.VMEM((SSM_CHUNKS, 2, BATCH, SSM_CHUNK_STATE), F32),
            pltpu.VMEM((SEQ_ROWS + CONF_HIST, w), F32),
            pltpu.VMEM((SEQ_ROWS, w), F32),
            pltpu.VMEM((SEQ_ROWS + SCONV_HIST, w), F32),
        ],
        compiler_params=pltpu.CompilerParams(
            dimension_semantics=("arbitrary",), vmem_limit_bytes=VMEM_LIMIT_BYTES),
        name="seq_mixers",
    )(proj, *layer_params)


def _merge_router_kernel(x_ref, g_ref, sc_ref, sh_ref, g1_ref, y_ref, wg_ref, bg_ref, wbr_ref, wo_ref,
                         g2_ref, sc2_ref, sh2_ref, whi_ref, wlo_ref, b_ref, tri_ref,
                         o_ref, h_ref, route_ref, cnt_ref, run_ref):
    x = x_ref[...]
    d = x.shape[1]
    w = BRANCH_WIDTH
    hb = _norm_modulate(x, g_ref[...], sc_ref[...], sh_ref[...]).astype(BF16)
    merged = None
    for n in range(N_BRANCH):
        gate = _sigmoid(_dot(hb, wg_ref[:, n * d:(n + 1) * d]) + bg_ref[:, n * d:(n + 1) * d])
        term = gate * _dot(y_ref[:, n * w:(n + 1) * w], wbr_ref[n])
        merged = term if merged is None else merged + term
    out = _dot(merged.astype(BF16), wo_ref[...])
    x_new = x + _per_batch(out, lambda o3: o3 * g1_ref[...])
    o_ref[...] = x_new
    _route_tile(x_new, g2_ref, sc2_ref, sh2_ref, whi_ref, wlo_ref, b_ref, tri_ref,
                h_ref, route_ref, cnt_ref, run_ref)


def _merge_router(l, x, mix_g, ffn_g, mod, ycat, w_gate, b_gate, w_branch, w_out, r_hi, r_lo, r_bias, tri):
    t, d = x.shape
    w = BRANCH_WIDTH
    row_tile = lambda cols: pl.BlockSpec((MERGE_ROWS, cols), lambda i: (i, 0))
    return pl.pallas_call(
        _merge_router_kernel,
        out_shape=(
            jax.ShapeDtypeStruct((t, d), F32),
            jax.ShapeDtypeStruct((t * SUBLANES, LANES), F32),
            jax.ShapeDtypeStruct((SUBLANES, t), F32),
            jax.ShapeDtypeStruct((N_EXPERTS, MERGE_ROWS), F32),
        ),
        grid=(t // MERGE_ROWS,),
        in_specs=[
            row_tile(d),
            _layer_spec(mix_g, l),
            _layer_spec(mod, l, MOD_SCALE1),
            _layer_spec(mod, l, MOD_SHIFT1),
            _layer_spec(mod, l, MOD_GATE1),
            row_tile(N_BRANCH * w),
            _layer_spec(w_gate, l),
            _layer_spec(b_gate, l),
            _layer_spec(w_branch, l),
            _layer_spec(w_out, l),
            _layer_spec(ffn_g, l),
            _layer_spec(mod, l, MOD_SCALE2),
            _layer_spec(mod, l, MOD_SHIFT2),
            _layer_spec(r_hi, l),
            _layer_spec(r_lo, l),
            _layer_spec(r_bias, l),
            pl.BlockSpec((MERGE_ROWS, MERGE_ROWS), lambda i: (0, 0)),
        ],
        out_specs=(
            row_tile(d),
            pl.BlockSpec((MERGE_ROWS * SUBLANES, LANES), lambda i: (i, 0)),
            pl.BlockSpec((SUBLANES, MERGE_ROWS), lambda i: (0, i)),
            pl.BlockSpec((N_EXPERTS, MERGE_ROWS), lambda i: (0, 0)),
        ),
        scratch_shapes=[pltpu.VMEM((N_EXPERTS, MERGE_ROWS), F32)],
        compiler_params=pltpu.CompilerParams(
            dimension_semantics=("arbitrary",), vmem_limit_bytes=VMEM_LIMIT_BYTES),
        name="mixer_merge_router",
    )(x, mix_g, mod, mod, mod, ycat, w_gate, b_gate, w_branch, w_out, ffn_g, mod, mod, r_hi, r_lo, r_bias, tri)


def _route_tile(x, g_ref, sc_ref, sh_ref, whi_ref, wlo_ref, b_ref, tri_ref,
                h_ref, route_ref, cnt_ref, run_ref):
    @pl.when(pl.program_id(0) == 0)
    def _():
        run_ref[...] = jnp.zeros_like(run_ref)

    h = _norm_modulate(x, g_ref[...], sc_ref[...], sh_ref[...])
    _store_token_tiles(h_ref, h)
    rows = h.shape[0]
    hhi, hlo = _split_bf16(h)
    contract_last = (((1,), (1,)), ((), ()))
    dott = lambda a, b: lax.dot_general(a, b, contract_last, preferred_element_type=F32)
    logits = dott(whi_ref[...], hhi) + dott(wlo_ref[...], hhi) + dott(whi_ref[...], hlo) + b_ref[...]
    lg = logits[0:N_GROUPS]
    le = logits[SUBLANES:SUBLANES + N_EXPERTS]

    g_row = lax.broadcasted_iota(jnp.int32, (N_GROUPS, rows), 0).astype(F32)
    g_max = jnp.max(lg, axis=0, keepdims=True)
    g_sel = jnp.min(jnp.where(lg == g_max, g_row, float(N_GROUPS)), axis=0, keepdims=True)
    g_w = 1.0 / jnp.sum(jnp.exp(lg - g_max), axis=0, keepdims=True)

    e_row = lax.broadcasted_iota(jnp.int32, (N_EXPERTS, rows), 0).astype(F32)
    e_grp = jnp.floor(e_row * (1.0 / EXPERTS_PER_GROUP))
    neg = float("-inf")
    lm = jnp.where(e_grp == g_sel, le, neg)
    v1 = jnp.max(lm, axis=0, keepdims=True)
    i1 = jnp.min(jnp.where(lm == v1, e_row, float(N_EXPERTS)), axis=0, keepdims=True)
    lm2 = jnp.where(e_row == i1, neg, lm)
    v2 = jnp.max(lm2, axis=0, keepdims=True)
    i2 = jnp.min(jnp.where(lm2 == v2, e_row, float(N_EXPERTS)), axis=0, keepdims=True)
    e21 = jnp.exp(v2 - v1)
    p1 = 1.0 / (1.0 + e21)
    w1 = p1 * g_w
    w2 = e21 * p1 * g_w

    hit1 = e_row == i1
    hit2 = e_row == i2
    onehot = jnp.where(hit1 | hit2, 1.0, 0.0)
    cum = _dot(onehot.astype(BF16), tri_ref[...])
    before = cum - onehot + run_ref[...]
    r1 = jnp.sum(jnp.where(hit1, before, 0.0), axis=0, keepdims=True)
    r2 = jnp.sum(jnp.where(hit2, before, 0.0), axis=0, keepdims=True)
    run_ref[...] = run_ref[...] + jnp.broadcast_to(cum[:, rows - 1:rows], run_ref.shape)
    cnt_ref[...] = run_ref[...]
    zero = jnp.zeros_like(w1)
    route_ref[...] = jnp.concatenate([i1, i2, r1, r2, w1, w2, zero, zero], axis=0)


def _row_copy(src_hbm, src_token, dst_hbm, dst_token, sem):
    return pltpu.make_async_copy(_token_tile(src_hbm, src_token), _token_tile(dst_hbm, dst_token), sem)


def _dispatch_kernel(zstart_ref, dest_hbm, h_ref, xs_hbm, idx0_smem, idx1_smem, zeros_ref,
                     idx_sem, zero_sem, row_sem):
    i = pl.program_id(0)
    n_tiles = pl.num_programs(0)
    idx_smem = (idx0_smem, idx1_smem)
    n_idx = idx0_smem.shape[0]
    tile = n_idx // TOP_K

    def idx_copy(tile_index, s):
        src = dest_hbm.at[pl.ds(pl.multiple_of(tile_index * n_idx, n_idx), n_idx)]
        return pltpu.make_async_copy(src, idx_smem[s], idx_sem.at[s])

    @pl.when(i == 0)
    def _():
        idx_copy(0, 0).start()

    def zero_copy(j):
        piece_rows = ZERO_ROWS * SUBLANES
        start = pl.multiple_of(zstart_ref[j] * SUBLANES, piece_rows)
        return pltpu.make_async_copy(zeros_ref, xs_hbm.at[pl.ds(start, piece_rows)], zero_sem)

    @pl.when(i == 0)
    def _():
        zeros_ref[...] = jnp.zeros_like(zeros_ref)

        def start(j, carry):
            @pl.when(zstart_ref[j] >= 0)
            def _():
                zero_copy(j).start()
            return carry

        def wait(j, carry):
            @pl.when(zstart_ref[j] >= 0)
            def _():
                zero_copy(j).wait()
            return carry

        lax.fori_loop(0, zstart_ref.shape[0], start, 0)
        lax.fori_loop(0, zstart_ref.shape[0], wait, 0)

    def step(slot):
        @pl.when(i + 1 < n_tiles)
        def _():
            idx_copy(i + 1, 1 - slot).start()

        idx_copy(i, slot).wait()

        def start_rows(t, carry):
            for k in range(TOP_K):
                _row_copy(h_ref, t, xs_hbm, idx_smem[slot][k * tile + t], row_sem).start(priority=k)
            return carry

        lax.fori_loop(0, tile, start_rows, 0, unroll=8)
        for k in range(TOP_K):
            pltpu.make_async_copy(h_ref, xs_hbm.at[pl.ds(0, tile * SUBLANES)], row_sem).wait()

    for slot in range(2):
        pl.when(lax.rem(i, 2) == slot)(functools.partial(step, slot))


def _dispatch(zstart, dest_tiles, h, cap):
    t = h.shape[0] // SUBLANES
    any_spec = pl.BlockSpec(memory_space=pl.ANY)
    return pl.pallas_call(
        _dispatch_kernel,
        out_shape=jax.ShapeDtypeStruct((cap * SUBLANES, LANES), h.dtype),
        grid_spec=pltpu.PrefetchScalarGridSpec(
            num_scalar_prefetch=1,
            grid=(t // DISPATCH_ROWS,),
            in_specs=[any_spec,
                      pl.BlockSpec((DISPATCH_ROWS * SUBLANES, LANES), lambda i, zs: (i, 0))],
            out_specs=any_spec,
            scratch_shapes=[
                pltpu.SMEM((TOP_K * DISPATCH_ROWS,), jnp.int32),
                pltpu.SMEM((TOP_K * DISPATCH_ROWS,), jnp.int32),
                pltpu.VMEM((ZERO_ROWS * SUBLANES, LANES), h.dtype),
                pltpu.SemaphoreType.DMA((2,)),
                pltpu.SemaphoreType.DMA,
                pltpu.SemaphoreType.DMA,
            ],
        ),
        compiler_params=pltpu.CompilerParams(
            dimension_semantics=("arbitrary",), has_side_effects=True),
        name="moe_dispatch",
    )(zstart, dest_tiles, h)


def _expert_kernel(first_ref, nblk_ref, total_ref, xs_hbm, w1_ref, w3_ref, w2_ref, ys_hbm,
                   w1b_ref, w3b_ref, w2b_ref, xbuf_ref, ybuf_ref, in_sem, out_sem):
    e = pl.program_id(0)
    n_blocks = nblk_ref[e]
    first = first_ref[e]
    total = total_ref[0]
    block_rows = EXPERT_ROWS * SUBLANES

    def rows_of(block):
        return pl.ds(pl.multiple_of(block * block_rows, block_rows), block_rows)

    def in_copy(block, slot):
        return pltpu.make_async_copy(xs_hbm.at[rows_of(block)], xbuf_ref.at[slot], in_sem.at[slot])

    def out_copy(block, slot):
        return pltpu.make_async_copy(ybuf_ref.at[slot], ys_hbm.at[rows_of(block)], out_sem.at[slot])

    @pl.when(n_blocks > 0)
    def _():
        @pl.when(first == 0)
        def _():
            in_copy(0, 0).start()

        w1b_ref[...] = w1_ref[...].astype(BF16)
        w3b_ref[...] = w3_ref[...].astype(BF16)
        w2b_ref[...] = w2_ref[...].astype(BF16)

        def block_step(j, carry):
            g = first + j
            slot = lax.rem(g, 2)

            @pl.when(g + 1 < total)
            def _():
                in_copy(g + 1, 1 - slot).start(priority=1)

            in_copy(g, slot).wait()

            @pl.when(g >= 2)
            def _():
                out_copy(g - 2, slot).wait()

            xb = _load_token_tiles(xbuf_ref.at[slot], 0, EXPERT_ROWS).astype(BF16)
            a = _dot(xb, w1b_ref[...])
            hid = a * _sigmoid(a) * _dot(xb, w3b_ref[...])
            _store_token_tiles(ybuf_ref.at[slot], _dot(hid.astype(BF16), w2b_ref[...]))
            out_copy(g, slot).start(priority=1)
            return carry

        lax.fori_loop(0, n_blocks, block_step, 0)

        @pl.when(first + n_blocks == total)
        def _():
            @pl.when(total >= 2)
            def _():
                out_copy(total - 2, lax.rem(total, 2)).wait()

            out_copy(total - 1, lax.rem(total - 1, 2)).wait()


def _experts(l, first_block, n_blocks, total_blocks, xs, w1, w3, w2):
    _, n_experts, d, f = w1.shape
    tile = (EXPERT_ROWS * SUBLANES, LANES)
    any_spec = pl.BlockSpec(memory_space=pl.ANY)
    weight_spec = lambda rows, cols: pl.BlockSpec((None, None, rows, cols), lambda e, *_: (l, e, 0, 0))
    return pl.pallas_call(
        _expert_kernel,
        out_shape=jax.ShapeDtypeStruct(xs.shape, F32),
        grid_spec=pltpu.PrefetchScalarGridSpec(
            num_scalar_prefetch=3,
            grid=(n_experts,),
            in_specs=[any_spec, weight_spec(d, f), weight_spec(d, f), weight_spec(f, d)],
            out_specs=any_spec,
            scratch_shapes=[
                pltpu.VMEM((d, f), BF16), pltpu.VMEM((d, f), BF16), pltpu.VMEM((f, d), BF16),
                pltpu.VMEM((2,) + tile, F32), pltpu.VMEM((2,) + tile, F32),
                pltpu.SemaphoreType.DMA((2,)), pltpu.SemaphoreType.DMA((2,)),
            ],
        ),
        compiler_params=pltpu.CompilerParams(
            dimension_semantics=("arbitrary",), vmem_limit_bytes=VMEM_LIMIT_BYTES),
        name="moe_experts",
    )(first_block, n_blocks, total_blocks, xs, w1, w3, w2)


def _gather_expert_rows(dest_hbm, ys_hbm, idx_smem, ybuf_ref, idx_sem, row_sem):
    i = pl.program_id(0)
    n_tiles = pl.num_programs(0)
    n_idx = idx_smem[0].shape[0]
    tile = n_idx // TOP_K

    def idx_copy(tile_index, s):
        src = dest_hbm.at[pl.ds(pl.multiple_of(tile_index * n_idx, n_idx), n_idx)]
        return pltpu.make_async_copy(src, idx_smem[s], idx_sem.at[s])

    def start_rows(s):
        def body(t, carry):
            for k in range(TOP_K):
                row = k * tile + t
                pltpu.make_async_copy(_token_tile(ys_hbm, idx_smem[s][row]),
                                      _token_tile(ybuf_ref.at[s], row), row_sem.at[s]).start(priority=k)
            return carry
        lax.fori_loop(0, tile, body, 0, unroll=8)

    def wait_rows(s):
        pltpu.make_async_copy(ybuf_ref.at[s], ybuf_ref.at[s], row_sem.at[s]).wait()

    @pl.when(i == 0)
    def _():
        idx_copy(0, 0).start()
        idx_copy(0, 0).wait()
        start_rows(0)

        @pl.when(n_tiles > 1)
        def _():
            idx_copy(1, 1).start()

    def step(slot):
        other = 1 - slot

        @pl.when(i + 1 < n_tiles)
        def _():
            idx_copy(i + 1, other).wait()
            start_rows(other)

        @pl.when(i + 2 < n_tiles)
        def _():
            idx_copy(i + 2, slot).start()

        wait_rows(slot)

    for slot in range(2):
        pl.when(lax.rem(i, 2) == slot)(functools.partial(step, slot))
    return lax.rem(i, 2)


def _combined_residual(slot, ybuf_ref, x_ref, g2_ref, wt_ref):
    tile = x_ref.shape[0]
    wt = wt_ref[...]
    ytiles = ybuf_ref.at[slot]
    y = (wt[:, 0:1] * _load_token_tiles(ytiles, 0, tile)
         + wt[:, 1:2] * _load_token_tiles(ytiles, tile, tile))
    return x_ref[...] + _per_batch(y, lambda y3: y3 * g2_ref[...])


def _combine_final_kernel(dest_hbm, ys_hbm, x_ref, g2_ref, wt_ref, fg_ref, o_ref,
                          idx0_smem, idx1_smem, ybuf_ref, idx_sem, row_sem):
    slot = _gather_expert_rows(dest_hbm, ys_hbm, (idx0_smem, idx1_smem), ybuf_ref, idx_sem, row_sem)
    out = _combined_residual(slot, ybuf_ref, x_ref, g2_ref, wt_ref)
    ms = jnp.mean(out * out, axis=-1, keepdims=True)
    o_ref[...] = out * lax.rsqrt(ms + NORM_EPS) * fg_ref[...]


def _combine_inproj_kernel(dest_hbm, ys_hbm, x_ref, g2_ref, wt_ref, g_ref, sc_ref, sh_ref, w_ref,
                           xo_ref, p_ref, idx0_smem, idx1_smem, ybuf_ref, idx_sem, row_sem):
    slot = _gather_expert_rows(dest_hbm, ys_hbm, (idx0_smem, idx1_smem), ybuf_ref, idx_sem, row_sem)
    out = _combined_residual(slot, ybuf_ref, x_ref, g2_ref, wt_ref)
    xo_ref[...] = out
    h = _norm_modulate(out, g_ref[...], sc_ref[...], sh_ref[...])
    p_ref[...] = _dot(h.astype(BF16), w_ref[...]).astype(p_ref.dtype)


def _combine(l, dest_tiles, ys, x, mod, wt, final_g=None, next_layer=None):
    t, d = x.shape
    any_spec = pl.BlockSpec(memory_space=pl.ANY)
    row_tile = lambda cols: pl.BlockSpec((COMBINE_ROWS, cols), lambda i: (i, 0))
    in_specs = [any_spec, any_spec, row_tile(d), _layer_spec(mod, l, MOD_GATE2), row_tile(TOP_K)]
    if next_layer is None:
        body, operands = _combine_final_kernel, (final_g,)
        in_specs += [pl.BlockSpec((1, d), lambda i: (0, 0))]
        out_shape = jax.ShapeDtypeStruct((t, d), F32)
        out_specs = row_tile(d)
    else:
        mix_g, w_in = next_layer
        n = w_in.shape[2]
        body, operands = _combine_inproj_kernel, (mix_g, mod, mod, w_in)
        in_specs += [_layer_spec(mix_g, l + 1), _layer_spec(mod, l + 1, MOD_SCALE1),
                     _layer_spec(mod, l + 1, MOD_SHIFT1), _layer_spec(w_in, l + 1)]
        out_shape = (jax.ShapeDtypeStruct((t, d), F32), jax.ShapeDtypeStruct((t, n), BF16))
        out_specs = (row_tile(d), row_tile(n))
    return pl.pallas_call(
        body,
        out_shape=out_shape,
        grid=(t // COMBINE_ROWS,),
        in_specs=in_specs,
        out_specs=out_specs,
        scratch_shapes=[
            pltpu.SMEM((TOP_K * COMBINE_ROWS,), jnp.int32),
            pltpu.SMEM((TOP_K * COMBINE_ROWS,), jnp.int32),
            pltpu.VMEM((2, TOP_K * COMBINE_ROWS * SUBLANES, LANES), F32),
            pltpu.SemaphoreType.DMA((2,)),
            pltpu.SemaphoreType.DMA((2,)),
        ],
        compiler_params=pltpu.CompilerParams(
            dimension_semantics=("arbitrary",), vmem_limit_bytes=VMEM_LIMIT_BYTES),
        name="moe_combine",
    )(dest_tiles, ys, x, mod, wt, *operands)


def _ssm_params(lam_re, lam_im, log_step, b_re, b_im, c_re, c_im):
    step = jnp.exp(log_step)[:, None]
    mag = jnp.exp(lam_re * step)
    a_re, a_im = mag * jnp.cos(lam_im * step), mag * jnp.sin(lam_im * step)
    den = lam_re * lam_re + lam_im * lam_im
    n_re, n_im = a_re - 1.0, a_im
    k_re = (n_re * lam_re + n_im * lam_im) / den
    k_im = (n_im * lam_re - n_re * lam_im) / den
    bb_re = k_re[..., None] * b_re - k_im[..., None] * b_im
    bb_im = k_re[..., None] * b_im + k_im[..., None] * b_re
    gpc = SSM_GROUPS // SSM_CHUNKS
    eye = jnp.eye(gpc, dtype=F32)

    def in_blocks(bb):
        bb = bb.reshape(SSM_CHUNKS, gpc, SSM_STATE, SSM_GROUP)
        return jnp.einsum("cgph,gk->cghkp", bb, eye).reshape(SSM_CHUNKS, SSM_CHUNK_IN, SSM_CHUNK_STATE)

    def out_blocks(cc):
        cc = cc.reshape(SSM_CHUNKS, gpc, SSM_GROUP, SSM_STATE)
        return jnp.einsum("cghp,gk->cgpkh", cc, eye).reshape(SSM_CHUNKS, SSM_CHUNK_STATE, SSM_CHUNK_IN)

    wb = jnp.concatenate([in_blocks(bb_re), in_blocks(bb_im)], axis=2).astype(BF16)
    wc = jnp.concatenate([out_blocks(c_re), out_blocks(-c_im)], axis=1).astype(BF16)
    bcast = lambda a: jnp.broadcast_to(a.reshape(SSM_CHUNKS, 1, SSM_CHUNK_STATE),
                                       (SSM_CHUNKS, BATCH, SSM_CHUNK_STATE))
    return wb, wc, bcast(a_re), bcast(a_im)


def _rows8(w):
    depth, k, c = w.shape
    return jnp.broadcast_to(w[:, :, None, :], (depth, k, SUBLANES, c)).reshape(depth, k * SUBLANES, c)


def _router_weights(w_rg, b_rg, w_re, b_re):
    depth, d, _ = w_rg.shape
    pad = SUBLANES - N_GROUPS
    w_t = jnp.concatenate([w_rg.transpose(0, 2, 1), jnp.zeros((depth, pad, d), F32),
                           w_re.transpose(0, 2, 1)], axis=1)
    b = jnp.concatenate([b_rg, jnp.zeros((depth, pad), F32), b_re], axis=1)
    hi, lo = _split_bf16(w_t)
    return hi, lo, jnp.broadcast_to(b[:, :, None], (depth, ROUTER_ROWS, ROUTE_ROWS))


def _routing_tables(route, counts):
    t = route.shape[1]
    counts = counts.astype(jnp.int32)
    padded = (counts + EXPERT_ROWS - 1) // EXPERT_ROWS * EXPERT_ROWS
    pad_end = jnp.cumsum(padded)
    pad_start = pad_end - padded
    experts = route[0:TOP_K].astype(jnp.int32)
    ranks = route[TOP_K:2 * TOP_K].astype(jnp.int32)
    expert_ids = jnp.arange(N_EXPERTS, dtype=jnp.int32)
    is_expert = experts[:, :, None] == expert_ids
    dest = jnp.sum(jnp.where(is_expert, pad_start, 0), axis=-1) + ranks

    def tiles(rows_per_tile):
        n = t // rows_per_tile
        return dest.reshape(TOP_K, n, rows_per_tile).transpose(1, 0, 2).reshape(-1)

    capacity = TOP_K * t + N_EXPERTS * EXPERT_ROWS
    first_block = pad_start // EXPERT_ROWS
    n_blocks = padded // EXPERT_ROWS
    last_start = pad_end - EXPERT_ROWS
    last_valid = counts - (padded - EXPERT_ROWS)
    piece = jnp.arange(EXPERT_ROWS // ZERO_ROWS, dtype=jnp.int32)
    has_pad = (counts[:, None] > 0) & ((piece[None, :] + 1) * ZERO_ROWS > last_valid[:, None])
    zstart = jnp.where(has_pad, last_start[:, None] + piece[None, :] * ZERO_ROWS, -1).reshape(-1)
    weights = route[2 * TOP_K:3 * TOP_K].T
    total_blocks = (pad_end[-1] // EXPERT_ROWS).reshape(1)
    return (tiles(DISPATCH_ROWS), tiles(COMBINE_ROWS), first_block.astype(jnp.int32), n_blocks.astype(jnp.int32),
            total_blocks.astype(jnp.int32), zstart.astype(jnp.int32), weights, capacity)


def kernel(x, c, norm_mix_g, norm_ffn_g, w_ada, b_ada, w_in, lam_re, lam_im, log_step, ssm_b_re, ssm_b_im, ssm_c_re, ssm_c_im, ssm_d, w_glu, conf_dw_w, conf_dw_b, conf_ln_g, conf_ln_b, sconv_w, w_branch, w_gate, b_gate, w_out, w_router_group, b_router_group, w_router_expert, b_router_expert, w_exp_gate, w_exp_up, w_exp_down, final_norm_g):
    bsz, seq, d = x.shape
    assert bsz == BATCH and d == D_MODEL
    depth = w_in.shape[0]
    t = bsz * seq
    rows = lambda v: v.reshape(depth, 1, -1)

    mix_g, ffn_g = rows(norm_mix_g), rows(norm_ffn_g)
    seq_params = (*jax.vmap(_ssm_params)(lam_re, lam_im, log_step, ssm_b_re, ssm_b_im, ssm_c_re, ssm_c_im),
                  rows(ssm_d), w_glu.astype(BF16), _rows8(conf_dw_w), rows(conf_dw_b), rows(conf_ln_g),
                  rows(conf_ln_b), _rows8(sconv_w))
    w_in_b, w_gate_b, w_branch_b, w_out_b = (w.astype(BF16) for w in (w_in, w_gate, w_branch, w_out))
    r_hi, r_lo, r_b = _router_weights(w_router_group, b_router_group, w_router_expert, b_router_expert)
    tri = jnp.triu(jnp.ones((ROUTE_ROWS, ROUTE_ROWS), BF16))

    mod = _ada(c, w_ada, b_ada)
    xt = jnp.transpose(x, (1, 0, 2)).reshape(t, d)

    proj = _inproj(0, xt, mix_g, mod, w_in_b)
    for l in range(depth):
        ycat = _seq_mixers(l, proj, *seq_params)
        xt, h2, route, counts = _merge_router(l, xt, mix_g, ffn_g, mod, ycat, w_gate_b, rows(b_gate),
                                              w_branch_b, w_out_b, r_hi, r_lo, r_b, tri)
        (dest_d, dest_c, first_block, n_blocks, total_blocks, zstart, weights,
         cap) = _routing_tables(route, counts[:, 0])
        xs = _dispatch(zstart, dest_d, h2, cap)
        ys = _experts(l, first_block, n_blocks, total_blocks, xs, w_exp_gate, w_exp_up, w_exp_down)
        if l + 1 < depth:
            xt, proj = _combine(l, dest_c, ys, xt, mod, weights, next_layer=(mix_g, w_in_b))
        else:
            xt = _combine(l, dest_c, ys, xt, mod, weights, final_g=final_norm_g.reshape(1, d))

    return jnp.transpose(xt.reshape(seq, bsz, d), (1, 0, 2))
```

```python
import functools
import math

import jax
import jax.numpy as jnp
from jax import lax
from jax.experimental import pallas as pl
from jax.experimental.pallas import tpu as pltpu

F32 = jnp.float32
BF16 = jnp.bfloat16

D_MODEL = 1024
BATCH = 8
BRANCH_WIDTH = D_MODEL // 2
N_BRANCH = 3
N_IN_CHUNKS = 6
SSM_GROUP = 16
SSM_GROUPS = BRANCH_WIDTH // SSM_GROUP
SSM_STATE = 64
CONF_KERNEL = 31
SCONV_KERNEL = 3
N_GROUPS = 4
EXPERTS_PER_GROUP = 8
N_EXPERTS = N_GROUPS * EXPERTS_PER_GROUP
TOP_K = 2
D_FF_EXPERT = D_MODEL // 2
NORM_EPS = 1e-6

SUBLANES = 8
LANES = 128
VMEM_LIMIT_BYTES = 56 * 1024 * 1024

SSM_CHUNKS = 2
SSM_CHUNK_IN = BRANCH_WIDTH // SSM_CHUNKS
SSM_CHUNK_STATE = SSM_GROUPS * SSM_STATE // SSM_CHUNKS
CONF_HIST = (CONF_KERNEL - 1) * BATCH
SCONV_HIST = (SCONV_KERNEL - 1) * BATCH
ROUTER_ROWS = 40
MOD_SHIFT1, MOD_SCALE1, MOD_GATE1, MOD_SHIFT2, MOD_SCALE2, MOD_GATE2 = range(6)

ADA_COLS = 1536
INPROJ_ROWS = 1024
SEQ_ROWS = 512
CONV_ROWS = 64
EW_ROWS = 32
MERGE_ROWS = 512
ROUTE_ROWS = MERGE_ROWS
DISPATCH_ROWS = 512
EXPERT_ROWS = 256
ZERO_ROWS = 64
COMBINE_ROWS = 512


def _dot(a, b):
    return jnp.dot(a, b, preferred_element_type=F32)


def _split_bf16(v):
    hi = v.astype(BF16)
    lo = (v - hi.astype(F32)).astype(BF16)
    return hi, lo


def _sigmoid(v):
    return 1.0 / (1.0 + jnp.exp(-v))


def _gelu_tanh(v):
    return 0.5 * v * (1.0 + jnp.tanh(math.sqrt(2.0 / math.pi) * (v + 0.044715 * (v * v * v))))


def _per_batch(v, fn):
    rows, d = v.shape
    return fn(v.reshape(rows // BATCH, BATCH, d)).reshape(rows, d)


def _load_token_tiles(ref, first_token, n_tokens):
    base = first_token * SUBLANES
    chunks = [ref[pl.ds(base + s, n_tokens, stride=SUBLANES), :] for s in range(SUBLANES)]
    return jnp.concatenate(chunks, axis=1)


def _store_token_tiles(ref, value):
    for s in range(SUBLANES):
        ref[pl.ds(s, value.shape[0], stride=SUBLANES), :] = value[:, s * LANES:(s + 1) * LANES]


def _token_tile(ref, token):
    return ref.at[pl.ds(pl.multiple_of(token * SUBLANES, SUBLANES), SUBLANES)]


def _layer_spec(arr, l, col_block=None):
    tail = arr.shape[1:]
    if col_block is None:
        zeros = (0,) * len(tail)
        return pl.BlockSpec((None,) + tail, lambda *_: (l,) + zeros)
    return pl.BlockSpec((None, tail[0], D_MODEL), lambda *_: (l, 0, col_block))


def _norm_modulate(x, g, scale, shift):
    ms = jnp.mean(x * x, axis=-1, keepdims=True)
    h = x * lax.rsqrt(ms + NORM_EPS) * g
    return _per_batch(h, lambda h3: h3 * (1.0 + scale) + shift)


def _ada_kernel(c_ref, w_ref, b_ref, o_ref):
    c = c_ref[...]
    cond = c * _sigmoid(c)
    chi, clo = _split_bf16(cond)
    whi, wlo = _split_bf16(w_ref[...])
    o_ref[...] = _dot(chi, whi) + _dot(chi, wlo) + _dot(clo, whi) + b_ref[...]


def _ada(c, w_ada, b_ada):
    depth, d, n = w_ada.shape
    return pl.pallas_call(
        _ada_kernel,
        out_shape=jax.ShapeDtypeStruct((depth, BATCH, n), F32),
        grid=(depth, n // ADA_COLS),
        in_specs=[
            pl.BlockSpec((BATCH, d), lambda l, j: (0, 0)),
            pl.BlockSpec((None, d, ADA_COLS), lambda l, j: (l, 0, j)),
            pl.BlockSpec((None, 1, ADA_COLS), lambda l, j: (l, 0, j)),
        ],
        out_specs=pl.BlockSpec((None, BATCH, ADA_COLS), lambda l, j: (l, 0, j)),
        compiler_params=pltpu.CompilerParams(
            dimension_semantics=("arbitrary", "arbitrary"), vmem_limit_bytes=VMEM_LIMIT_BYTES),
        name="ada_modulation",
    )(c, w_ada, b_ada.reshape(depth, 1, n))


def _inproj_kernel(x_ref, g_ref, sc_ref, sh_ref, w_ref, o_ref):
    h = _norm_modulate(x_ref[...], g_ref[...], sc_ref[...], sh_ref[...])
    o_ref[...] = _dot(h.astype(BF16), w_ref[...]).astype(o_ref.dtype)


def _inproj(l, x, g, mod, w_in):
    t, d = x.shape
    n = w_in.shape[2]
    return pl.pallas_call(
        _inproj_kernel,
        out_shape=jax.ShapeDtypeStruct((t, n), BF16),
        grid=(t // INPROJ_ROWS,),
        in_specs=[
            pl.BlockSpec((INPROJ_ROWS, d), lambda i: (i, 0)),
            _layer_spec(g, l),
            _layer_spec(mod, l, MOD_SCALE1),
            _layer_spec(mod, l, MOD_SHIFT1),
            _layer_spec(w_in, l),
        ],
        out_specs=pl.BlockSpec((INPROJ_ROWS, n), lambda i: (i, 0)),
        compiler_params=pltpu.CompilerParams(
            dimension_semantics=("arbitrary",), vmem_limit_bytes=VMEM_LIMIT_BYTES),
        name="mixer_inproj",
    )(x, g, mod, mod, w_in)


def _seq_kernel(p_ref, wb_ref, wc_ref, are_ref, aim_ref, d_ref, wglu_ref, cw_ref, cb_ref,
                lng_ref, lnb_ref, sw_ref, o_ref, bu_ref, st_ref, zbuf_ref, cbuf_ref, qbuf_ref):
    rows = p_ref.shape[0]
    steps = rows // BATCH
    w = BRANCH_WIDTH

    @pl.when(pl.program_id(0) == 0)
    def _():
        st_ref[...] = jnp.zeros_like(st_ref)
        zbuf_ref[0:CONF_HIST, :] = jnp.zeros((CONF_HIST, w), F32)
        qbuf_ref[0:SCONV_HIST, :] = jnp.zeros((SCONV_HIST, w), F32)

    chunk_groups = EW_ROWS // SUBLANES

    def pointwise_chunk(j, carry):
        r0 = pl.multiple_of(j * EW_ROWS, EW_ROWS)
        rs = pl.ds(r0, EW_ROWS)
        gate_c = p_ref[rs, 4 * w:5 * w].astype(F32)
        hv = p_ref[rs, 5 * w:6 * w].astype(F32)
        qbuf_ref[pl.ds(r0 + SCONV_HIST, EW_ROWS), :] = gate_c * hv
        q3 = qbuf_ref[pl.ds(r0, EW_ROWS + SCONV_HIST), :].reshape(
            chunk_groups + SCONV_KERNEL - 1, SUBLANES, w)
        acc = q3[0:chunk_groups] * sw_ref[0:SUBLANES, :]
        for k in range(1, SCONV_KERNEL):
            acc = acc + q3[k:k + chunk_groups] * sw_ref[k * SUBLANES:(k + 1) * SUBLANES, :]
        gate_b = p_ref[rs, 3 * w:4 * w].astype(F32)
        o_ref[rs, 2 * w:3 * w] = (gate_b * acc.reshape(EW_ROWS, w)).astype(o_ref.dtype)
        v = p_ref[rs, w:2 * w].astype(F32)
        g = p_ref[rs, 2 * w:3 * w].astype(F32)
        zbuf_ref[pl.ds(r0 + CONF_HIST, EW_ROWS), :] = v * _sigmoid(g)
        return carry

    lax.fori_loop(0, rows // EW_ROWS, pointwise_chunk, 0)
    qbuf_ref[0:SCONV_HIST, :] = qbuf_ref[rows:rows + SCONV_HIST, :]

    ys = []
    for c in range(SSM_CHUNKS):
        u_c = p_ref[:, c * SSM_CHUNK_IN:(c + 1) * SSM_CHUNK_IN]
        bu_ref[...] = _dot(u_c, wb_ref[c])
        a_re = are_ref[c]
        a_im = aim_ref[c]

        def step(t, carry):
            x_re, x_im = carry
            r0 = pl.multiple_of(t * BATCH, BATCH)
            n_re = a_re * x_re - a_im * x_im + bu_ref[pl.ds(r0, BATCH), 0:SSM_CHUNK_STATE]
            n_im = a_re * x_im + a_im * x_re + bu_ref[pl.ds(r0, BATCH), SSM_CHUNK_STATE:]
            bu_ref[pl.ds(r0, BATCH), 0:SSM_CHUNK_STATE] = n_re
            bu_ref[pl.ds(r0, BATCH), SSM_CHUNK_STATE:] = n_im
            return n_re, n_im

        x_re, x_im = lax.fori_loop(0, steps, step, (st_ref[c, 0], st_ref[c, 1]), unroll=4)
        st_ref[c, 0] = x_re
        st_ref[c, 1] = x_im
        ys.append(_dot(bu_ref[...].astype(BF16), wc_ref[c]))
    u = p_ref[:, 0:w].astype(F32)
    y = _gelu_tanh(jnp.concatenate(ys, axis=1) + d_ref[...] * u)
    y_ssm = y * _sigmoid(_dot(y.astype(BF16), wglu_ref[...]))
    o_ref[:, 0:w] = y_ssm.astype(o_ref.dtype)

    win_groups = (CONV_ROWS + CONF_HIST) // SUBLANES
    out_groups = CONV_ROWS // SUBLANES

    def conv_chunk(j, carry):
        r0 = pl.multiple_of(j * CONV_ROWS, CONV_ROWS)
        for s in range(w // LANES):
            ls = slice(s * LANES, (s + 1) * LANES)
            win = zbuf_ref[pl.ds(r0, CONV_ROWS + CONF_HIST), ls].reshape(win_groups, SUBLANES, LANES)
            acc = win[0:out_groups] * cw_ref[0:SUBLANES, ls]
            for k in range(1, CONF_KERNEL):
                acc = acc + win[k:k + out_groups] * cw_ref[k * SUBLANES:(k + 1) * SUBLANES, ls]
            cbuf_ref[pl.ds(r0, CONV_ROWS), ls] = acc.reshape(CONV_ROWS, LANES)
        return carry

    lax.fori_loop(0, rows // CONV_ROWS, conv_chunk, 0)
    zbuf_ref[0:CONF_HIST, :] = zbuf_ref[rows:rows + CONF_HIST, :]
    cv = cbuf_ref[...] + cb_ref[...]
    cc = cv - jnp.mean(cv, axis=-1, keepdims=True)
    ln = cc * lax.rsqrt(jnp.mean(cc * cc, axis=-1, keepdims=True) + NORM_EPS) * lng_ref[...] + lnb_ref[...]
    o_ref[:, w:2 * w] = (ln * _sigmoid(ln)).astype(o_ref.dtype)


def _seq_mixers(l, proj, *layer_params):
    t = proj.shape[0]
    w = BRANCH_WIDTH
    return pl.pallas_call(
        _seq_kernel,
        out_shape=jax.ShapeDtypeStruct((t, N_BRANCH * w), BF16),
        grid=(t // SEQ_ROWS,),
        in_specs=[pl.BlockSpec((SEQ_ROWS, N_IN_CHUNKS * w), lambda i: (i, 0))]
                 + [_layer_spec(p, l) for p in layer_params],
        out_specs=pl.BlockSpec((SEQ_ROWS, N_BRANCH * w), lambda i: (i, 0)),
        scratch_shapes=[
            pltpu.VMEM((SEQ_ROWS, 2 * SSM_CHUNK_STATE), F32),
            pltpu.VMEM((SSM_CHUNKS, 2, BATCH, SSM_CHUNK_STATE), F32),
            pltpu.VMEM((SEQ_ROWS + CONF_HIST, w), F32),
            pltpu.VMEM((SEQ_ROWS, w), F32),
            pltpu.VMEM((SEQ_ROWS + SCONV_HIST, w), F32),
        ],
        compiler_params=pltpu.CompilerParams(
            dimension_semantics=("arbitrary",), vmem_limit_bytes=VMEM_LIMIT_BYTES),
        name="seq_mixers",
    )(proj, *layer_params)


def _merge_router_kernel(x_ref, g_ref, sc_ref, sh_ref, g1_ref, y_ref, wg_ref, bg_ref, wbr_ref, wo_ref,
                         g2_ref, sc2_ref, sh2_ref, whi_ref, wlo_ref, b_ref, tri_ref,
                         o_ref, h_ref, route_ref, route_t_ref, cnt_ref, run_ref):
    x = x_ref[...]
    d = x.shape[1]
    w = BRANCH_WIDTH
    hb = _norm_modulate(x, g_ref[...], sc_ref[...], sh_ref[...]).astype(BF16)
    merged = None
    for n in range(N_BRANCH):
        gate = _sigmoid(_dot(hb, wg_ref[:, n * d:(n + 1) * d]) + bg_ref[:, n * d:(n + 1) * d])
        term = gate * _dot(y_ref[:, n * w:(n + 1) * w], wbr_ref[n])
        merged = term if merged is None else merged + term
    out = _dot(merged.astype(BF16), wo_ref[...])
    x_new = x + _per_batch(out, lambda o3: o3 * g1_ref[...])
    o_ref[...] = x_new
    _route_tile(x_new, g2_ref, sc2_ref, sh2_ref, whi_ref, wlo_ref, b_ref, tri_ref,
                h_ref, route_ref, route_t_ref, cnt_ref, run_ref)


def _merge_router(l, x, mix_g, ffn_g, mod, ycat, w_gate, b_gate, w_branch, w_out, r_hi, r_lo, r_bias, tri):
    t, d = x.shape
    w = BRANCH_WIDTH
    row_tile = lambda cols: pl.BlockSpec((MERGE_ROWS, cols), lambda i: (i, 0))
    return pl.pallas_call(
        _merge_router_kernel,
        out_shape=(
            jax.ShapeDtypeStruct((t, d), F32),
            jax.ShapeDtypeStruct((t * SUBLANES, LANES), F32),
            jax.ShapeDtypeStruct((SUBLANES, t), F32),
            jax.ShapeDtypeStruct((t, SUBLANES), F32),
            jax.ShapeDtypeStruct((N_EXPERTS, MERGE_ROWS), F32),
        ),
        grid=(t // MERGE_ROWS,),
        in_specs=[
            row_tile(d),
            _layer_spec(mix_g, l),
            _layer_spec(mod, l, MOD_SCALE1),
            _layer_spec(mod, l, MOD_SHIFT1),
            _layer_spec(mod, l, MOD_GATE1),
            row_tile(N_BRANCH * w),
            _layer_spec(w_gate, l),
            _layer_spec(b_gate, l),
            _layer_spec(w_branch, l),
            _layer_spec(w_out, l),
            _layer_spec(ffn_g, l),
            _layer_spec(mod, l, MOD_SCALE2),
            _layer_spec(mod, l, MOD_SHIFT2),
            _layer_spec(r_hi, l),
            _layer_spec(r_lo, l),
            _layer_spec(r_bias, l),
            pl.BlockSpec((MERGE_ROWS, MERGE_ROWS), lambda i: (0, 0)),
        ],
        out_specs=(
            row_tile(d),
            pl.BlockSpec((MERGE_ROWS * SUBLANES, LANES), lambda i: (i, 0)),
            pl.BlockSpec((SUBLANES, MERGE_ROWS), lambda i: (0, i)),
            row_tile(SUBLANES),
            pl.BlockSpec((N_EXPERTS, MERGE_ROWS), lambda i: (0, 0)),
        ),
        scratch_shapes=[pltpu.VMEM((N_EXPERTS, MERGE_ROWS), F32)],
        compiler_params=pltpu.CompilerParams(
            dimension_semantics=("arbitrary",), vmem_limit_bytes=VMEM_LIMIT_BYTES),
        name="mixer_merge_router",
    )(x, mix_g, mod, mod, mod, ycat, w_gate, b_gate, w_branch, w_out, ffn_g, mod, mod, r_hi, r_lo, r_bias, tri)


def _route_tile(x, g_ref, sc_ref, sh_ref, whi_ref, wlo_ref, b_ref, tri_ref,
                h_ref, route_ref, route_t_ref, cnt_ref, run_ref):
    @pl.when(pl.program_id(0) == 0)
    def _():
        run_ref[...] = jnp.zeros_like(run_ref)

    h = _norm_modulate(x, g_ref[...], sc_ref[...], sh_ref[...])
    _store_token_tiles(h_ref, h)
    rows = h.shape[0]
    hhi, hlo = _split_bf16(h)
    logits = (_dot(hhi, whi_ref[...]) + _dot(hhi, wlo_ref[...]) + _dot(hlo, whi_ref[...]) + b_ref[...]).T
    lg = logits[0:N_GROUPS]
    le = logits[SUBLANES:SUBLANES + N_EXPERTS]

    g_row = lax.broadcasted_iota(jnp.int32, (N_GROUPS, rows), 0).astype(F32)
    g_max = jnp.max(lg, axis=0, keepdims=True)
    g_sel = jnp.min(jnp.where(lg == g_max, g_row, float(N_GROUPS)), axis=0, keepdims=True)
    g_w = 1.0 / jnp.sum(jnp.exp(lg - g_max), axis=0, keepdims=True)

    e_row = lax.broadcasted_iota(jnp.int32, (N_EXPERTS, rows), 0).astype(F32)
    e_grp = jnp.floor(e_row * (1.0 / EXPERTS_PER_GROUP))
    neg = float("-inf")
    lm = jnp.where(e_grp == g_sel, le, neg)
    v1 = jnp.max(lm, axis=0, keepdims=True)
    i1 = jnp.min(jnp.where(lm == v1, e_row, float(N_EXPERTS)), axis=0, keepdims=True)
    lm2 = jnp.where(e_row == i1, neg, lm)
    v2 = jnp.max(lm2, axis=0, keepdims=True)
    i2 = jnp.min(jnp.where(lm2 == v2, e_row, float(N_EXPERTS)), axis=0, keepdims=True)
    e21 = jnp.exp(v2 - v1)
    p1 = 1.0 / (1.0 + e21)
    w1 = p1 * g_w
    w2 = e21 * p1 * g_w

    hit1 = e_row == i1
    hit2 = e_row == i2
    onehot = jnp.where(hit1 | hit2, 1.0, 0.0)
    cum = _dot(onehot.astype(BF16), tri_ref[...])
    before = cum - onehot + run_ref[...]
    r1 = jnp.sum(jnp.where(hit1, before, 0.0), axis=0, keepdims=True)
    r2 = jnp.sum(jnp.where(hit2, before, 0.0), axis=0, keepdims=True)
    run_ref[...] = run_ref[...] + jnp.broadcast_to(cum[:, rows - 1:rows], run_ref.shape)
    cnt_ref[...] = run_ref[...]
    zero = jnp.zeros_like(w1)
    route = jnp.concatenate([i1, i2, r1, r2, w1, w2, zero, zero], axis=0)
    route_ref[...] = route
    route_t_ref[...] = route.T


def _row_copy(src_hbm, src_token, dst_hbm, dst_token, sem):
    return pltpu.make_async_copy(_token_tile(src_hbm, src_token), _token_tile(dst_hbm, dst_token), sem)


def _dispatch_kernel(zstart_ref, dest_hbm, h_ref, xs_hbm, idx0_smem, idx1_smem, zeros_ref,
                     idx_sem, zero_sem, row_sem):
    i = pl.program_id(0)
    n_tiles = pl.num_programs(0)
    idx_smem = (idx0_smem, idx1_smem)
    n_idx = idx0_smem.shape[0]
    tile = n_idx // TOP_K

    def idx_copy(tile_index, s):
        src = dest_hbm.at[pl.ds(pl.multiple_of(tile_index * n_idx, n_idx), n_idx)]
        return pltpu.make_async_copy(src, idx_smem[s], idx_sem.at[s])

    @pl.when(i == 0)
    def _():
        idx_copy(0, 0).start()

    def zero_copy(j):
        piece_rows = ZERO_ROWS * SUBLANES
        start = pl.multiple_of(zstart_ref[j] * SUBLANES, piece_rows)
        return pltpu.make_async_copy(zeros_ref, xs_hbm.at[pl.ds(start, piece_rows)], zero_sem)

    @pl.when(i == 0)
    def _():
        zeros_ref[...] = jnp.zeros_like(zeros_ref)

        def start(j, carry):
            @pl.when(zstart_ref[j] >= 0)
            def _():
                zero_copy(j).start()
            return carry

        def wait(j, carry):
            @pl.when(zstart_ref[j] >= 0)
            def _():
                zero_copy(j).wait()
            return carry

        lax.fori_loop(0, zstart_ref.shape[0], start, 0)
        lax.fori_loop(0, zstart_ref.shape[0], wait, 0)

    def step(slot):
        @pl.when(i + 1 < n_tiles)
        def _():
            idx_copy(i + 1, 1 - slot).start()

        idx_copy(i, slot).wait()

        def start_rows(t, carry):
            for k in range(TOP_K):
                _row_copy(h_ref, t, xs_hbm, idx_smem[slot][k * tile + t], row_sem).start(priority=1)
            return carry

        lax.fori_loop(0, tile, start_rows, 0, unroll=8)
        for k in range(TOP_K):
            pltpu.make_async_copy(h_ref, xs_hbm.at[pl.ds(0, tile * SUBLANES)], row_sem).wait()

    for slot in range(2):
        pl.when(lax.rem(i, 2) == slot)(functools.partial(step, slot))


def _dispatch(zstart, dest_tiles, h, cap):
    t = h.shape[0] // SUBLANES
    any_spec = pl.BlockSpec(memory_space=pl.ANY)
    return pl.pallas_call(
        _dispatch_kernel,
        out_shape=jax.ShapeDtypeStruct((cap * SUBLANES, LANES), h.dtype),
        grid_spec=pltpu.PrefetchScalarGridSpec(
            num_scalar_prefetch=1,
            grid=(t // DISPATCH_ROWS,),
            in_specs=[any_spec,
                      pl.BlockSpec((DISPATCH_ROWS * SUBLANES, LANES), lambda i, zs: (i, 0))],
            out_specs=any_spec,
            scratch_shapes=[
                pltpu.SMEM((TOP_K * DISPATCH_ROWS,), jnp.int32),
                pltpu.SMEM((TOP_K * DISPATCH_ROWS,), jnp.int32),
                pltpu.VMEM((ZERO_ROWS * SUBLANES, LANES), h.dtype),
                pltpu.SemaphoreType.DMA((2,)),
                pltpu.SemaphoreType.DMA,
                pltpu.SemaphoreType.DMA,
            ],
        ),
        compiler_params=pltpu.CompilerParams(
            dimension_semantics=("arbitrary",), has_side_effects=True),
        name="moe_dispatch",
    )(zstart, dest_tiles, h)


def _expert_kernel(first_ref, nblk_ref, total_ref, xs_hbm, w1_ref, w3_ref, w2_ref, ys_hbm,
                   w1b_ref, w3b_ref, w2b_ref, xbuf_ref, ybuf_ref, in_sem, out_sem):
    e = pl.program_id(0)
    n_blocks = nblk_ref[e]
    first = first_ref[e]
    total = total_ref[0]
    block_rows = EXPERT_ROWS * SUBLANES

    def rows_of(block):
        return pl.ds(pl.multiple_of(block * block_rows, block_rows), block_rows)

    def in_copy(block, slot):
        return pltpu.make_async_copy(xs_hbm.at[rows_of(block)], xbuf_ref.at[slot], in_sem.at[slot])

    def out_copy(block, slot):
        return pltpu.make_async_copy(ybuf_ref.at[slot], ys_hbm.at[rows_of(block)], out_sem.at[slot])

    @pl.when(n_blocks > 0)
    def _():
        @pl.when(first == 0)
        def _():
            in_copy(0, 0).start()

        w1b_ref[...] = w1_ref[...].astype(BF16)
        w3b_ref[...] = w3_ref[...].astype(BF16)
        w2b_ref[...] = w2_ref[...].astype(BF16)

        def block_step(j, carry):
            g = first + j
            slot = lax.rem(g, 2)

            @pl.when(g + 1 < total)
            def _():
                in_copy(g + 1, 1 - slot).start(priority=1)

            in_copy(g, slot).wait()

            @pl.when(g >= 2)
            def _():
                out_copy(g - 2, slot).wait()

            xb = _load_token_tiles(xbuf_ref.at[slot], 0, EXPERT_ROWS).astype(BF16)
            a = _dot(xb, w1b_ref[...])
            hid = a * _sigmoid(a) * _dot(xb, w3b_ref[...])
            _store_token_tiles(ybuf_ref.at[slot], _dot(hid.astype(BF16), w2b_ref[...]))
            out_copy(g, slot).start(priority=1)
            return carry

        lax.fori_loop(0, n_blocks, block_step, 0)

        @pl.when(first + n_blocks == total)
        def _():
            @pl.when(total >= 2)
            def _():
                out_copy(total - 2, lax.rem(total, 2)).wait()

            out_copy(total - 1, lax.rem(total - 1, 2)).wait()


def _experts(l, first_block, n_blocks, total_blocks, xs, w1, w3, w2):
    _, n_experts, d, f = w1.shape
    tile = (EXPERT_ROWS * SUBLANES, LANES)
    any_spec = pl.BlockSpec(memory_space=pl.ANY)
    weight_spec = lambda rows, cols: pl.BlockSpec((None, None, rows, cols), lambda e, *_: (l, e, 0, 0))
    return pl.pallas_call(
        _expert_kernel,
        out_shape=jax.ShapeDtypeStruct(xs.shape, F32),
        grid_spec=pltpu.PrefetchScalarGridSpec(
            num_scalar_prefetch=3,
            grid=(n_experts,),
            in_specs=[any_spec, weight_spec(d, f), weight_spec(d, f), weight_spec(f, d)],
            out_specs=any_spec,
            scratch_shapes=[
                pltpu.VMEM((d, f), BF16), pltpu.VMEM((d, f), BF16), pltpu.VMEM((f, d), BF16),
                pltpu.VMEM((2,) + tile, F32), pltpu.VMEM((2,) + tile, F32),
                pltpu.SemaphoreType.DMA((2,)), pltpu.SemaphoreType.DMA((2,)),
            ],
        ),
        compiler_params=pltpu.CompilerParams(
            dimension_semantics=("arbitrary",), vmem_limit_bytes=VMEM_LIMIT_BYTES),
        name="moe_experts",
    )(first_block, n_blocks, total_blocks, xs, w1, w3, w2)


def _gather_expert_rows(dest_hbm, ys_hbm, idx_smem, ybuf_ref, idx_sem, row_sem):
    i = pl.program_id(0)
    n_tiles = pl.num_programs(0)
    n_idx = idx_smem[0].shape[0]
    tile = n_idx // TOP_K

    def idx_copy(tile_index, s):
        src = dest_hbm.at[pl.ds(pl.multiple_of(tile_index * n_idx, n_idx), n_idx)]
        return pltpu.make_async_copy(src, idx_smem[s], idx_sem.at[s])

    def start_rows(s):
        def body(t, carry):
            for k in range(TOP_K):
                row = k * tile + t
                pltpu.make_async_copy(_token_tile(ys_hbm, idx_smem[s][row]),
                                      _token_tile(ybuf_ref.at[s], row), row_sem.at[s]).start(priority=1)
            return carry
        lax.fori_loop(0, tile, body, 0, unroll=8)

    def wait_rows(s):
        pltpu.make_async_copy(ybuf_ref.at[s], ybuf_ref.at[s], row_sem.at[s]).wait()

    @pl.when(i == 0)
    def _():
        idx_copy(0, 0).start()
        idx_copy(0, 0).wait()
        start_rows(0)

        @pl.when(n_tiles > 1)
        def _():
            idx_copy(1, 1).start()

    def step(slot):
        other = 1 - slot

        @pl.when(i + 1 < n_tiles)
        def _():
            idx_copy(i + 1, other).wait()
            start_rows(other)

        @pl.when(i + 2 < n_tiles)
        def _():
            idx_copy(i + 2, slot).start()

        wait_rows(slot)

    for slot in range(2):
        pl.when(lax.rem(i, 2) == slot)(functools.partial(step, slot))
    return lax.rem(i, 2)


def _combined_residual(slot, ybuf_ref, x_ref, g2_ref, wt_ref):
    tile = x_ref.shape[0]
    wt = wt_ref[...]
    ytiles = ybuf_ref.at[slot]
    y = (wt[:, 2 * TOP_K:2 * TOP_K + 1] * _load_token_tiles(ytiles, 0, tile)
         + wt[:, 2 * TOP_K + 1:2 * TOP_K + 2] * _load_token_tiles(ytiles, tile, tile))
    return x_ref[...] + _per_batch(y, lambda y3: y3 * g2_ref[...])


def _combine_final_kernel(dest_hbm, ys_hbm, x_ref, g2_ref, wt_ref, fg_ref, o_ref,
                          idx0_smem, idx1_smem, ybuf_ref, idx_sem, row_sem):
    slot = _gather_expert_rows(dest_hbm, ys_hbm, (idx0_smem, idx1_smem), ybuf_ref, idx_sem, row_sem)
    out = _combined_residual(slot, ybuf_ref, x_ref, g2_ref, wt_ref)
    ms = jnp.mean(out * out, axis=-1, keepdims=True)
    o_ref[...] = out * lax.rsqrt(ms + NORM_EPS) * fg_ref[...]


def _combine_inproj_kernel(dest_hbm, ys_hbm, x_ref, g2_ref, wt_ref, g_ref, sc_ref, sh_ref, w_ref,
                           xo_ref, p_ref, idx0_smem, idx1_smem, ybuf_ref, idx_sem, row_sem):
    slot = _gather_expert_rows(dest_hbm, ys_hbm, (idx0_smem, idx1_smem), ybuf_ref, idx_sem, row_sem)
    out = _combined_residual(slot, ybuf_ref, x_ref, g2_ref, wt_ref)
    xo_ref[...] = out
    h = _norm_modulate(out, g_ref[...], sc_ref[...], sh_ref[...])
    p_ref[...] = _dot(h.astype(BF16), w_ref[...]).astype(p_ref.dtype)


def _combine(l, dest_tiles, ys, x, mod, wt, final_g=None, next_layer=None):
    t, d = x.shape
    any_spec = pl.BlockSpec(memory_space=pl.ANY)
    row_tile = lambda cols: pl.BlockSpec((COMBINE_ROWS, cols), lambda i: (i, 0))
    in_specs = [any_spec, any_spec, row_tile(d), _layer_spec(mod, l, MOD_GATE2), row_tile(SUBLANES)]
    if next_layer is None:
        body, operands = _combine_final_kernel, (final_g,)
        in_specs += [pl.BlockSpec((1, d), lambda i: (0, 0))]
        out_shape = jax.ShapeDtypeStruct((t, d), F32)
        out_specs = row_tile(d)
    else:
        mix_g, w_in = next_layer
        n = w_in.shape[2]
        body, operands = _combine_inproj_kernel, (mix_g, mod, mod, w_in)
        in_specs += [_layer_spec(mix_g, l + 1), _layer_spec(mod, l + 1, MOD_SCALE1),
                     _layer_spec(mod, l + 1, MOD_SHIFT1), _layer_spec(w_in, l + 1)]
        out_shape = (jax.ShapeDtypeStruct((t, d), F32), jax.ShapeDtypeStruct((t, n), BF16))
        out_specs = (row_tile(d), row_tile(n))
    return pl.pallas_call(
        body,
        out_shape=out_shape,
        grid=(t // COMBINE_ROWS,),
        in_specs=in_specs,
        out_specs=out_specs,
        scratch_shapes=[
            pltpu.SMEM((TOP_K * COMBINE_ROWS,), jnp.int32),
            pltpu.SMEM((TOP_K * COMBINE_ROWS,), jnp.int32),
            pltpu.VMEM((2, TOP_K * COMBINE_ROWS * SUBLANES, LANES), F32),
            pltpu.SemaphoreType.DMA((2,)),
            pltpu.SemaphoreType.DMA((2,)),
        ],
        compiler_params=pltpu.CompilerParams(
            dimension_semantics=("arbitrary",), vmem_limit_bytes=VMEM_LIMIT_BYTES),
        name="moe_combine",
    )(dest_tiles, ys, x, mod, wt, *operands)


def _ssm_params(lam_re, lam_im, log_step, b_re, b_im, c_re, c_im):
    step = jnp.exp(log_step)[:, None]
    mag = jnp.exp(lam_re * step)
    a_re, a_im = mag * jnp.cos(lam_im * step), mag * jnp.sin(lam_im * step)
    den = lam_re * lam_re + lam_im * lam_im
    n_re, n_im = a_re - 1.0, a_im
    k_re = (n_re * lam_re + n_im * lam_im) / den
    k_im = (n_im * lam_re - n_re * lam_im) / den
    bb_re = k_re[..., None] * b_re - k_im[..., None] * b_im
    bb_im = k_re[..., None] * b_im + k_im[..., None] * b_re
    gpc = SSM_GROUPS // SSM_CHUNKS
    eye = jnp.eye(gpc, dtype=F32)

    def in_blocks(bb):
        bb = bb.reshape(SSM_CHUNKS, gpc, SSM_STATE, SSM_GROUP)
        return jnp.einsum("cgph,gk->cghkp", bb, eye).reshape(SSM_CHUNKS, SSM_CHUNK_IN, SSM_CHUNK_STATE)

    def out_blocks(cc):
        cc = cc.reshape(SSM_CHUNKS, gpc, SSM_GROUP, SSM_STATE)
        return jnp.einsum("cghp,gk->cgpkh", cc, eye).reshape(SSM_CHUNKS, SSM_CHUNK_STATE, SSM_CHUNK_IN)

    wb = jnp.concatenate([in_blocks(bb_re), in_blocks(bb_im)], axis=2).astype(BF16)
    wc = jnp.concatenate([out_blocks(c_re), out_blocks(-c_im)], axis=1).astype(BF16)
    bcast = lambda a: jnp.broadcast_to(a.reshape(SSM_CHUNKS, 1, SSM_CHUNK_STATE),
                                       (SSM_CHUNKS, BATCH, SSM_CHUNK_STATE))
    return wb, wc, bcast(a_re), bcast(a_im)


def _rows8(w):
    depth, k, c = w.shape
    return jnp.broadcast_to(w[:, :, None, :], (depth, k, SUBLANES, c)).reshape(depth, k * SUBLANES, c)


def _router_weights(w_rg, b_rg, w_re, b_re):
    depth, d, _ = w_rg.shape
    pad = SUBLANES - N_GROUPS
    tail = LANES - ROUTER_ROWS
    w_cols = jnp.concatenate([w_rg, jnp.zeros((depth, d, pad), F32), w_re,
                              jnp.zeros((depth, d, tail), F32)], axis=2)
    b = jnp.concatenate([b_rg, jnp.zeros((depth, pad), F32), b_re, jnp.zeros((depth, tail), F32)], axis=1)
    hi, lo = _split_bf16(w_cols)
    return hi, lo, b.reshape(depth, 1, LANES)


def _routing_tables(route, counts):
    t = route.shape[1]
    counts = counts.astype(jnp.int32)
    padded = (counts + EXPERT_ROWS - 1) // EXPERT_ROWS * EXPERT_ROWS
    pad_end = jnp.cumsum(padded)
    pad_start = pad_end - padded
    experts = route[0:TOP_K].astype(jnp.int32)
    ranks = route[TOP_K:2 * TOP_K].astype(jnp.int32)
    expert_ids = jnp.arange(N_EXPERTS, dtype=jnp.int32)
    is_expert = experts[:, :, None] == expert_ids
    dest = jnp.sum(jnp.where(is_expert, pad_start, 0), axis=-1) + ranks

    def tiles(rows_per_tile):
        n = t // rows_per_tile
        return dest.reshape(TOP_K, n, rows_per_tile).transpose(1, 0, 2).reshape(-1)

    capacity = TOP_K * t + N_EXPERTS * EXPERT_ROWS
    first_block = pad_start // EXPERT_ROWS
    n_blocks = padded // EXPERT_ROWS
    last_start = pad_end - EXPERT_ROWS
    last_valid = counts - (padded - EXPERT_ROWS)
    piece = jnp.arange(EXPERT_ROWS // ZERO_ROWS, dtype=jnp.int32)
    has_pad = (counts[:, None] > 0) & ((piece[None, :] + 1) * ZERO_ROWS > last_valid[:, None])
    zstart = jnp.where(has_pad, last_start[:, None] + piece[None, :] * ZERO_ROWS, -1).reshape(-1)
    total_blocks = (pad_end[-1] // EXPERT_ROWS).reshape(1)
    return (tiles(DISPATCH_ROWS), tiles(COMBINE_ROWS), first_block.astype(jnp.int32), n_blocks.astype(jnp.int32),
            total_blocks.astype(jnp.int32), zstart.astype(jnp.int32), capacity)


def kernel(x, c, norm_mix_g, norm_ffn_g, w_ada, b_ada, w_in, lam_re, lam_im, log_step, ssm_b_re, ssm_b_im, ssm_c_re, ssm_c_im, ssm_d, w_glu, conf_dw_w, conf_dw_b, conf_ln_g, conf_ln_b, sconv_w, w_branch, w_gate, b_gate, w_out, w_router_group, b_router_group, w_router_expert, b_router_expert, w_exp_gate, w_exp_up, w_exp_down, final_norm_g):
    bsz, seq, d = x.shape
    assert bsz == BATCH and d == D_MODEL
    depth = w_in.shape[0]
    t = bsz * seq
    rows = lambda v: v.reshape(depth, 1, -1)

    mix_g, ffn_g = rows(norm_mix_g), rows(norm_ffn_g)
    seq_params = (*jax.vmap(_ssm_params)(lam_re, lam_im, log_step, ssm_b_re, ssm_b_im, ssm_c_re, ssm_c_im),
                  rows(ssm_d), w_glu.astype(BF16), _rows8(conf_dw_w), rows(conf_dw_b), rows(conf_ln_g),
                  rows(conf_ln_b), _rows8(sconv_w))
    w_in_b, w_gate_b, w_branch_b, w_out_b = (w.astype(BF16) for w in (w_in, w_gate, w_branch, w_out))
    r_hi, r_lo, r_b = _router_weights(w_router_group, b_router_group, w_router_expert, b_router_expert)
    tri = jnp.triu(jnp.ones((ROUTE_ROWS, ROUTE_ROWS), BF16))

    mod = _ada(c, w_ada, b_ada)
    xt = jnp.transpose(x, (1, 0, 2)).reshape(t, d)

    proj = _inproj(0, xt, mix_g, mod, w_in_b)
    for l in range(depth):
        ycat = _seq_mixers(l, proj, *seq_params)
        xt, h2, route, route_t, counts = _merge_router(l, xt, mix_g, ffn_g, mod, ycat, w_gate_b, rows(b_gate),
                                                       w_branch_b, w_out_b, r_hi, r_lo, r_b, tri)
        (dest_d, dest_c, first_block, n_blocks, total_blocks, zstart,
         cap) = _routing_tables(route, counts[:, 0])
        xs = _dispatch(zstart, dest_d, h2, cap)
        ys = _experts(l, first_block, n_blocks, total_blocks, xs, w_exp_gate, w_exp_up, w_exp_down)
        if l + 1 < depth:
            xt, proj = _combine(l, dest_c, ys, xt, mod, route_t, next_layer=(mix_g, w_in_b))
        else:
            xt = _combine(l, dest_c, ys, xt, mod, route_t, final_g=final_norm_g.reshape(1, d))

    return jnp.transpose(xt.reshape(seq, bsz, d), (1, 0, 2))
```

```python
import functools
import math

import jax
import jax.numpy as jnp
from jax import lax
from jax.experimental import pallas as pl
from jax.experimental.pallas import tpu as pltpu

F32 = jnp.float32
BF16 = jnp.bfloat16

D_MODEL = 1024
BATCH = 8
BRANCH_WIDTH = D_MODEL // 2
N_BRANCH = 3
N_IN_CHUNKS = 6
SSM_GROUP = 16
SSM_GROUPS = BRANCH_WIDTH // SSM_GROUP
SSM_STATE = 64
CONF_KERNEL = 31
SCONV_KERNEL = 3
N_GROUPS = 4
EXPERTS_PER_GROUP = 8
N_EXPERTS = N_GROUPS * EXPERTS_PER_GROUP
TOP_K = 2
D_FF_EXPERT = D_MODEL // 2
NORM_EPS = 1e-6

SUBLANES = 8
LANES = 128
VMEM_LIMIT_BYTES = 56 * 1024 * 1024

SSM_CHUNKS = 2
SSM_CHUNK_IN = BRANCH_WIDTH // SSM_CHUNKS
SSM_CHUNK_STATE = SSM_GROUPS * SSM_STATE // SSM_CHUNKS
CONF_HIST = (CONF_KERNEL - 1) * BATCH
SCONV_HIST = (SCONV_KERNEL - 1) * BATCH
ROUTER_ROWS = 40
MOD_SHIFT1, MOD_SCALE1, MOD_GATE1, MOD_SHIFT2, MOD_SCALE2, MOD_GATE2 = range(6)

ADA_COLS = 1536
INPROJ_ROWS = 1024
SEQ_ROWS = 512
CONV_ROWS = 64
EW_ROWS = 32
MERGE_ROWS = 512
ROUTE_ROWS = MERGE_ROWS
DISPATCH_ROWS = 512
EXPERT_ROWS = 512
ZERO_ROWS = 64
COMBINE_ROWS = 512


def _dot(a, b):
    return jnp.dot(a, b, preferred_element_type=F32)


def _split_bf16(v):
    hi = v.astype(BF16)
    lo = (v - hi.astype(F32)).astype(BF16)
    return hi, lo


def _sigmoid(v):
    return 1.0 / (1.0 + jnp.exp(-v))


def _gelu_tanh(v):
    return 0.5 * v * (1.0 + jnp.tanh(math.sqrt(2.0 / math.pi) * (v + 0.044715 * (v * v * v))))


def _per_batch(v, fn):
    rows, d = v.shape
    return fn(v.reshape(rows // BATCH, BATCH, d)).reshape(rows, d)


def _load_token_tiles(ref, first_token, n_tokens):
    base = first_token * SUBLANES
    chunks = [ref[pl.ds(base + s, n_tokens, stride=SUBLANES), :] for s in range(SUBLANES)]
    return jnp.concatenate(chunks, axis=1)


def _store_token_tiles(ref, value):
    for s in range(SUBLANES):
        ref[pl.ds(s, value.shape[0], stride=SUBLANES), :] = value[:, s * LANES:(s + 1) * LANES]


def _token_tile(ref, token):
    return ref.at[pl.ds(pl.multiple_of(token * SUBLANES, SUBLANES), SUBLANES)]


def _layer_spec(arr, l, col_block=None):
    tail = arr.shape[1:]
    if col_block is None:
        zeros = (0,) * len(tail)
        return pl.BlockSpec((None,) + tail, lambda *_: (l,) + zeros)
    return pl.BlockSpec((None, tail[0], D_MODEL), lambda *_: (l, 0, col_block))


def _norm_modulate(x, g, scale, shift):
    ms = jnp.mean(x * x, axis=-1, keepdims=True)
    h = x * lax.rsqrt(ms + NORM_EPS) * g
    return _per_batch(h, lambda h3: h3 * (1.0 + scale) + shift)


def _ada_kernel(c_ref, w_ref, b_ref, o_ref):
    c = c_ref[...]
    cond = c * _sigmoid(c)
    chi, clo = _split_bf16(cond)
    whi, wlo = _split_bf16(w_ref[...])
    o_ref[...] = _dot(chi, whi) + _dot(chi, wlo) + _dot(clo, whi) + b_ref[...]


def _ada(c, w_ada, b_ada):
    depth, d, n = w_ada.shape
    return pl.pallas_call(
        _ada_kernel,
        out_shape=jax.ShapeDtypeStruct((depth, BATCH, n), F32),
        grid=(depth, n // ADA_COLS),
        in_specs=[
            pl.BlockSpec((BATCH, d), lambda l, j: (0, 0)),
            pl.BlockSpec((None, d, ADA_COLS), lambda l, j: (l, 0, j)),
            pl.BlockSpec((None, 1, ADA_COLS), lambda l, j: (l, 0, j)),
        ],
        out_specs=pl.BlockSpec((None, BATCH, ADA_COLS), lambda l, j: (l, 0, j)),
        compiler_params=pltpu.CompilerParams(
            dimension_semantics=("arbitrary", "arbitrary"), vmem_limit_bytes=VMEM_LIMIT_BYTES),
        name="ada_modulation",
    )(c, w_ada, b_ada.reshape(depth, 1, n))


def _inproj_kernel(x_ref, g_ref, sc_ref, sh_ref, w_ref, o_ref):
    h = _norm_modulate(x_ref[...], g_ref[...], sc_ref[...], sh_ref[...])
    o_ref[...] = _dot(h.astype(BF16), w_ref[...]).astype(o_ref.dtype)


def _inproj(l, x, g, mod, w_in):
    t, d = x.shape
    n = w_in.shape[2]
    return pl.pallas_call(
        _inproj_kernel,
        out_shape=jax.ShapeDtypeStruct((t, n), BF16),
        grid=(t // INPROJ_ROWS,),
        in_specs=[
            pl.BlockSpec((INPROJ_ROWS, d), lambda i: (i, 0)),
            _layer_spec(g, l),
            _layer_spec(mod, l, MOD_SCALE1),
            _layer_spec(mod, l, MOD_SHIFT1),
            _layer_spec(w_in, l),
        ],
        out_specs=pl.BlockSpec((INPROJ_ROWS, n), lambda i: (i, 0)),
        compiler_params=pltpu.CompilerParams(
            dimension_semantics=("arbitrary",), vmem_limit_bytes=VMEM_LIMIT_BYTES),
        name="mixer_inproj",
    )(x, g, mod, mod, w_in)


def _seq_kernel(p_ref, wb_ref, wc_ref, are_ref, aim_ref, d_ref, wglu_ref, cw_ref, cb_ref,
                lng_ref, lnb_ref, sw_ref, o_ref, bu_ref, st_ref, zbuf_ref, cbuf_ref, qbuf_ref):
    rows = p_ref.shape[0]
    steps = rows // BATCH
    w = BRANCH_WIDTH

    @pl.when(pl.program_id(0) == 0)
    def _():
        st_ref[...] = jnp.zeros_like(st_ref)
        zbuf_ref[0:CONF_HIST, :] = jnp.zeros((CONF_HIST, w), F32)
        qbuf_ref[0:SCONV_HIST, :] = jnp.zeros((SCONV_HIST, w), F32)

    chunk_groups = EW_ROWS // SUBLANES

    def pointwise_chunk(j, carry):
        r0 = pl.multiple_of(j * EW_ROWS, EW_ROWS)
        rs = pl.ds(r0, EW_ROWS)
        gate_c = p_ref[rs, 4 * w:5 * w].astype(F32)
        hv = p_ref[rs, 5 * w:6 * w].astype(F32)
        qbuf_ref[pl.ds(r0 + SCONV_HIST, EW_ROWS), :] = gate_c * hv
        q3 = qbuf_ref[pl.ds(r0, EW_ROWS + SCONV_HIST), :].reshape(
            chunk_groups + SCONV_KERNEL - 1, SUBLANES, w)
        acc = q3[0:chunk_groups] * sw_ref[0:SUBLANES, :]
        for k in range(1, SCONV_KERNEL):
            acc = acc + q3[k:k + chunk_groups] * sw_ref[k * SUBLANES:(k + 1) * SUBLANES, :]
        gate_b = p_ref[rs, 3 * w:4 * w].astype(F32)
        o_ref[rs, 2 * w:3 * w] = (gate_b * acc.reshape(EW_ROWS, w)).astype(o_ref.dtype)
        v = p_ref[rs, w:2 * w].astype(F32)
        g = p_ref[rs, 2 * w:3 * w].astype(F32)
        zbuf_ref[pl.ds(r0 + CONF_HIST, EW_ROWS), :] = v * _sigmoid(g)
        return carry

    lax.fori_loop(0, rows // EW_ROWS, pointwise_chunk, 0)
    qbuf_ref[0:SCONV_HIST, :] = qbuf_ref[rows:rows + SCONV_HIST, :]

    for c in range(SSM_CHUNKS):
        u_c = p_ref[:, c * SSM_CHUNK_IN:(c + 1) * SSM_CHUNK_IN]
        bu_ref[c] = _dot(u_c, wb_ref[c])
    for c in range(SSM_CHUNKS):
        a_re = are_ref[c]
        a_im = aim_ref[c]
        bu_c = bu_ref.at[c]

        def step(t, carry, a_re=a_re, a_im=a_im, bu_c=bu_c):
            x_re, x_im = carry
            r0 = pl.multiple_of(t * BATCH, BATCH)
            n_re = a_re * x_re - a_im * x_im + bu_c[pl.ds(r0, BATCH), 0:SSM_CHUNK_STATE]
            n_im = a_re * x_im + a_im * x_re + bu_c[pl.ds(r0, BATCH), SSM_CHUNK_STATE:]
            bu_c[pl.ds(r0, BATCH), 0:SSM_CHUNK_STATE] = n_re
            bu_c[pl.ds(r0, BATCH), SSM_CHUNK_STATE:] = n_im
            return n_re, n_im

        x_re, x_im = lax.fori_loop(0, steps, step, (st_ref[c, 0], st_ref[c, 1]), unroll=4)
        st_ref[c, 0] = x_re
        st_ref[c, 1] = x_im
    ys = [_dot(bu_ref[c].astype(BF16), wc_ref[c]) for c in range(SSM_CHUNKS)]
    u = p_ref[:, 0:w].astype(F32)
    y = _gelu_tanh(jnp.concatenate(ys, axis=1) + d_ref[...] * u)
    y_ssm = y * _sigmoid(_dot(y.astype(BF16), wglu_ref[...]))
    o_ref[:, 0:w] = y_ssm.astype(o_ref.dtype)

    win_groups = (CONV_ROWS + CONF_HIST) // SUBLANES
    out_groups = CONV_ROWS // SUBLANES

    def conv_chunk(j, carry):
        r0 = pl.multiple_of(j * CONV_ROWS, CONV_ROWS)
        for s in range(w // LANES):
            ls = slice(s * LANES, (s + 1) * LANES)
            win = zbuf_ref[pl.ds(r0, CONV_ROWS + CONF_HIST), ls].reshape(win_groups, SUBLANES, LANES)
            acc = win[0:out_groups] * cw_ref[0:SUBLANES, ls]
            for k in range(1, CONF_KERNEL):
                acc = acc + win[k:k + out_groups] * cw_ref[k * SUBLANES:(k + 1) * SUBLANES, ls]
            cbuf_ref[pl.ds(r0, CONV_ROWS), ls] = acc.reshape(CONV_ROWS, LANES)
        return carry

    lax.fori_loop(0, rows // CONV_ROWS, conv_chunk, 0)
    zbuf_ref[0:CONF_HIST, :] = zbuf_ref[rows:rows + CONF_HIST, :]
    cv = cbuf_ref[...] + cb_ref[...]
    cc = cv - jnp.mean(cv, axis=-1, keepdims=True)
    ln = cc * lax.rsqrt(jnp.mean(cc * cc, axis=-1, keepdims=True) + NORM_EPS) * lng_ref[...] + lnb_ref[...]
    o_ref[:, w:2 * w] = (ln * _sigmoid(ln)).astype(o_ref.dtype)


def _seq_mixers(l, proj, *layer_params):
    t = proj.shape[0]
    w = BRANCH_WIDTH
    return pl.pallas_call(
        _seq_kernel,
        out_shape=jax.ShapeDtypeStruct((t, N_BRANCH * w), BF16),
        grid=(t // SEQ_ROWS,),
        in_specs=[pl.BlockSpec((SEQ_ROWS, N_IN_CHUNKS * w), lambda i: (i, 0))]
                 + [_layer_spec(p, l) for p in layer_params],
        out_specs=pl.BlockSpec((SEQ_ROWS, N_BRANCH * w), lambda i: (i, 0)),
        scratch_shapes=[
            pltpu.VMEM((SSM_CHUNKS, SEQ_ROWS, 2 * SSM_CHUNK_STATE), F32),
            pltpu.VMEM((SSM_CHUNKS, 2, BATCH, SSM_CHUNK_STATE), F32),
            pltpu.VMEM((SEQ_ROWS + CONF_HIST, w), F32),
            pltpu.VMEM((SEQ_ROWS, w), F32),
            pltpu.VMEM((SEQ_ROWS + SCONV_HIST, w), F32),
        ],
        compiler_params=pltpu.CompilerParams(
            dimension_semantics=("arbitrary",), vmem_limit_bytes=VMEM_LIMIT_BYTES),
        name="seq_mixers",
    )(proj, *layer_params)


def _merge_router_kernel(x_ref, g_ref, sc_ref, sh_ref, g1_ref, y_ref, wg_ref, bg_ref, wbr_ref, wo_ref,
                         g2_ref, sc2_ref, sh2_ref, whi_ref, wlo_ref, b_ref, tri_ref,
                         o_ref, h_ref, route_ref, route_t_ref, cnt_ref, run_ref):
    x = x_ref[...]
    d = x.shape[1]
    w = BRANCH_WIDTH
    hb = _norm_modulate(x, g_ref[...], sc_ref[...], sh_ref[...]).astype(BF16)
    merged = None
    for n in range(N_BRANCH):
        gate = _sigmoid(_dot(hb, wg_ref[:, n * d:(n + 1) * d]) + bg_ref[:, n * d:(n + 1) * d])
        term = gate * _dot(y_ref[:, n * w:(n + 1) * w], wbr_ref[n])
        merged = term if merged is None else merged + term
    out = _dot(merged.astype(BF16), wo_ref[...])
    x_new = x + _per_batch(out, lambda o3: o3 * g1_ref[...])
    o_ref[...] = x_new
    _route_tile(x_new, g2_ref, sc2_ref, sh2_ref, whi_ref, wlo_ref, b_ref, tri_ref,
                h_ref, route_ref, route_t_ref, cnt_ref, run_ref)


def _merge_router(l, x, mix_g, ffn_g, mod, ycat, w_gate, b_gate, w_branch, w_out, r_hi, r_lo, r_bias, tri):
    t, d = x.shape
    w = BRANCH_WIDTH
    row_tile = lambda cols: pl.BlockSpec((MERGE_ROWS, cols), lambda i: (i, 0))
    return pl.pallas_call(
        _merge_router_kernel,
        out_shape=(
            jax.ShapeDtypeStruct((t, d), F32),
            jax.ShapeDtypeStruct((t * SUBLANES, LANES), F32),
            jax.ShapeDtypeStruct((SUBLANES, t), F32),
            jax.ShapeDtypeStruct((t, SUBLANES), F32),
            jax.ShapeDtypeStruct((N_EXPERTS, MERGE_ROWS), F32),
        ),
        grid=(t // MERGE_ROWS,),
        in_specs=[
            row_tile(d),
            _layer_spec(mix_g, l),
            _layer_spec(mod, l, MOD_SCALE1),
            _layer_spec(mod, l, MOD_SHIFT1),
            _layer_spec(mod, l, MOD_GATE1),
            row_tile(N_BRANCH * w),
            _layer_spec(w_gate, l),
            _layer_spec(b_gate, l),
            _layer_spec(w_branch, l),
            _layer_spec(w_out, l),
            _layer_spec(ffn_g, l),
            _layer_spec(mod, l, MOD_SCALE2),
            _layer_spec(mod, l, MOD_SHIFT2),
            _layer_spec(r_hi, l),
            _layer_spec(r_lo, l),
            _layer_spec(r_bias, l),
            pl.BlockSpec((MERGE_ROWS, MERGE_ROWS), lambda i: (0, 0)),
        ],
        out_specs=(
            row_tile(d),
            pl.BlockSpec((MERGE_ROWS * SUBLANES, LANES), lambda i: (i, 0)),
            pl.BlockSpec((SUBLANES, MERGE_ROWS), lambda i: (0, i)),
            row_tile(SUBLANES),
            pl.BlockSpec((N_EXPERTS, MERGE_ROWS), lambda i: (0, 0)),
        ),
        scratch_shapes=[pltpu.VMEM((N_EXPERTS, MERGE_ROWS), F32)],
        compiler_params=pltpu.CompilerParams(
            dimension_semantics=("arbitrary",), vmem_limit_bytes=VMEM_LIMIT_BYTES),
        name="mixer_merge_router",
    )(x, mix_g, mod, mod, mod, ycat, w_gate, b_gate, w_branch, w_out, ffn_g, mod, mod, r_hi, r_lo, r_bias, tri)


def _route_tile(x, g_ref, sc_ref, sh_ref, whi_ref, wlo_ref, b_ref, tri_ref,
                h_ref, route_ref, route_t_ref, cnt_ref, run_ref):
    @pl.when(pl.program_id(0) == 0)
    def _():
        run_ref[...] = jnp.zeros_like(run_ref)

    h = _norm_modulate(x, g_ref[...], sc_ref[...], sh_ref[...])
    _store_token_tiles(h_ref, h)
    rows = h.shape[0]
    hhi, hlo = _split_bf16(h)
    logits = (_dot(hhi, whi_ref[...]) + _dot(hhi, wlo_ref[...]) + _dot(hlo, whi_ref[...]) + b_ref[...]).T
    lg = logits[0:N_GROUPS]
    le = logits[SUBLANES:SUBLANES + N_EXPERTS]

    g_row = lax.broadcasted_iota(jnp.int32, (N_GROUPS, rows), 0).astype(F32)
    g_max = jnp.max(lg, axis=0, keepdims=True)
    g_sel = jnp.min(jnp.where(lg == g_max, g_row, float(N_GROUPS)), axis=0, keepdims=True)
    g_w = 1.0 / jnp.sum(jnp.exp(lg - g_max), axis=0, keepdims=True)

    e_row = lax.broadcasted_iota(jnp.int32, (N_EXPERTS, rows), 0).astype(F32)
    e_grp = jnp.floor(e_row * (1.0 / EXPERTS_PER_GROUP))
    neg = float("-inf")
    lm = jnp.where(e_grp == g_sel, le, neg)
    v1 = jnp.max(lm, axis=0, keepdims=True)
    i1 = jnp.min(jnp.where(lm == v1, e_row, float(N_EXPERTS)), axis=0, keepdims=True)
    lm2 = jnp.where(e_row == i1, neg, lm)
    v2 = jnp.max(lm2, axis=0, keepdims=True)
    i2 = jnp.min(jnp.where(lm2 == v2, e_row, float(N_EXPERTS)), axis=0, keepdims=True)
    e21 = jnp.exp(v2 - v1)
    p1 = 1.0 / (1.0 + e21)
    w1 = p1 * g_w
    w2 = e21 * p1 * g_w

    hit1 = e_row == i1
    hit2 = e_row == i2
    onehot = jnp.where(hit1 | hit2, 1.0, 0.0)
    cum = _dot(onehot.astype(BF16), tri_ref[...])
    before = cum - onehot + run_ref[...]
    r1 = jnp.sum(jnp.where(hit1, before, 0.0), axis=0, keepdims=True)
    r2 = jnp.sum(jnp.where(hit2, before, 0.0), axis=0, keepdims=True)
    run_ref[...] = run_ref[...] + jnp.broadcast_to(cum[:, rows - 1:rows], run_ref.shape)
    cnt_ref[...] = run_ref[...]
    zero = jnp.zeros_like(w1)
    route = jnp.concatenate([i1, i2, r1, r2, w1, w2, zero, zero], axis=0)
    route_ref[...] = route
    route_t_ref[...] = route.T


def _row_copy(src_hbm, src_token, dst_hbm, dst_token, sem):
    return pltpu.make_async_copy(_token_tile(src_hbm, src_token), _token_tile(dst_hbm, dst_token), sem)


def _dispatch_kernel(zstart_ref, dest_hbm, h_ref, xs_hbm, idx0_smem, idx1_smem, zeros_ref,
                     idx_sem, zero_sem, row_sem):
    i = pl.program_id(0)
    n_tiles = pl.num_programs(0)
    idx_smem = (idx0_smem, idx1_smem)
    n_idx = idx0_smem.shape[0]
    tile = n_idx // TOP_K

    def idx_copy(tile_index, s):
        src = dest_hbm.at[pl.ds(pl.multiple_of(tile_index * n_idx, n_idx), n_idx)]
        return pltpu.make_async_copy(src, idx_smem[s], idx_sem.at[s])

    @pl.when(i == 0)
    def _():
        idx_copy(0, 0).start()

    def zero_copy(j):
        piece_rows = ZERO_ROWS * SUBLANES
        start = pl.multiple_of(zstart_ref[j] * SUBLANES, piece_rows)
        return pltpu.make_async_copy(zeros_ref, xs_hbm.at[pl.ds(start, piece_rows)], zero_sem)

    @pl.when(i == 0)
    def _():
        zeros_ref[...] = jnp.zeros_like(zeros_ref)

        def start(j, carry):
            @pl.when(zstart_ref[j] >= 0)
            def _():
                zero_copy(j).start()
            return carry

        def wait(j, carry):
            @pl.when(zstart_ref[j] >= 0)
            def _():
                zero_copy(j).wait()
            return carry

        lax.fori_loop(0, zstart_ref.shape[0], start, 0)
        lax.fori_loop(0, zstart_ref.shape[0], wait, 0)

    def step(slot):
        @pl.when(i + 1 < n_tiles)
        def _():
            idx_copy(i + 1, 1 - slot).start()

        idx_copy(i, slot).wait()

        def start_rows(t, carry):
            for k in range(TOP_K):
                _row_copy(h_ref, t, xs_hbm, idx_smem[slot][k * tile + t], row_sem).start(priority=k)
            return carry

        lax.fori_loop(0, tile, start_rows, 0, unroll=8)
        for k in range(TOP_K):
            pltpu.make_async_copy(h_ref, xs_hbm.at[pl.ds(0, tile * SUBLANES)], row_sem).wait()

    for slot in range(2):
        pl.when(lax.rem(i, 2) == slot)(functools.partial(step, slot))


def _dispatch(zstart, dest_tiles, h, cap):
    t = h.shape[0] // SUBLANES
    any_spec = pl.BlockSpec(memory_space=pl.ANY)
    return pl.pallas_call(
        _dispatch_kernel,
        out_shape=jax.ShapeDtypeStruct((cap * SUBLANES, LANES), h.dtype),
        grid_spec=pltpu.PrefetchScalarGridSpec(
            num_scalar_prefetch=1,
            grid=(t // DISPATCH_ROWS,),
            in_specs=[any_spec,
                      pl.BlockSpec((DISPATCH_ROWS * SUBLANES, LANES), lambda i, zs: (i, 0))],
            out_specs=any_spec,
            scratch_shapes=[
                pltpu.SMEM((TOP_K * DISPATCH_ROWS,), jnp.int32),
                pltpu.SMEM((TOP_K * DISPATCH_ROWS,), jnp.int32),
                pltpu.VMEM((ZERO_ROWS * SUBLANES, LANES), h.dtype),
                pltpu.SemaphoreType.DMA((2,)),
                pltpu.SemaphoreType.DMA,
                pltpu.SemaphoreType.DMA,
            ],
        ),
        compiler_params=pltpu.CompilerParams(
            dimension_semantics=("arbitrary",), has_side_effects=True),
        name="moe_dispatch",
    )(zstart, dest_tiles, h)


def _expert_kernel(first_ref, nblk_ref, total_ref, xs_hbm, w1_ref, w3_ref, w2_ref, ys_hbm,
                   w1b_ref, w3b_ref, w2b_ref, xbuf_ref, ybuf_ref, in_sem, out_sem):
    e = pl.program_id(0)
    n_blocks = nblk_ref[e]
    first = first_ref[e]
    total = total_ref[0]
    block_rows = EXPERT_ROWS * SUBLANES

    def rows_of(block):
        return pl.ds(pl.multiple_of(block * block_rows, block_rows), block_rows)

    def in_copy(block, slot):
        return pltpu.make_async_copy(xs_hbm.at[rows_of(block)], xbuf_ref.at[slot], in_sem.at[slot])

    def out_copy(block, slot):
        return pltpu.make_async_copy(ybuf_ref.at[slot], ys_hbm.at[rows_of(block)], out_sem.at[slot])

    @pl.when(n_blocks > 0)
    def _():
        @pl.when(first == 0)
        def _():
            in_copy(0, 0).start()

        w1b_ref[...] = w1_ref[...].astype(BF16)
        w3b_ref[...] = w3_ref[...].astype(BF16)
        w2b_ref[...] = w2_ref[...].astype(BF16)

        def block_step(j, carry):
            g = first + j
            slot = lax.rem(g, 2)

            @pl.when(g + 1 < total)
            def _():
                in_copy(g + 1, 1 - slot).start(priority=1)

            in_copy(g, slot).wait()

            @pl.when(g >= 2)
            def _():
                out_copy(g - 2, slot).wait()

            xb = _load_token_tiles(xbuf_ref.at[slot], 0, EXPERT_ROWS).astype(BF16)
            a = _dot(xb, w1b_ref[...])
            hid = a * _sigmoid(a) * _dot(xb, w3b_ref[...])
            _store_token_tiles(ybuf_ref.at[slot], _dot(hid.astype(BF16), w2b_ref[...]))
            out_copy(g, slot).start()
            return carry

        lax.fori_loop(0, n_blocks, block_step, 0)

        @pl.when(first + n_blocks == total)
        def _():
            @pl.when(total >= 2)
            def _():
                out_copy(total - 2, lax.rem(total, 2)).wait()

            out_copy(total - 1, lax.rem(total - 1, 2)).wait()


def _experts(l, first_block, n_blocks, total_blocks, xs, w1, w3, w2):
    _, n_experts, d, f = w1.shape
    tile = (EXPERT_ROWS * SUBLANES, LANES)
    any_spec = pl.BlockSpec(memory_space=pl.ANY)
    weight_spec = lambda rows, cols: pl.BlockSpec((None, None, rows, cols), lambda e, *_: (l, e, 0, 0))
    return pl.pallas_call(
        _expert_kernel,
        out_shape=jax.ShapeDtypeStruct(xs.shape, F32),
        grid_spec=pltpu.PrefetchScalarGridSpec(
            num_scalar_prefetch=3,
            grid=(n_experts,),
            in_specs=[any_spec, weight_spec(d, f), weight_spec(d, f), weight_spec(f, d)],
            out_specs=any_spec,
            scratch_shapes=[
                pltpu.VMEM((d, f), BF16), pltpu.VMEM((d, f), BF16), pltpu.VMEM((f, d), BF16),
                pltpu.VMEM((2,) + tile, F32), pltpu.VMEM((2,) + tile, F32),
                pltpu.SemaphoreType.DMA((2,)), pltpu.SemaphoreType.DMA((2,)),
            ],
        ),
        compiler_params=pltpu.CompilerParams(
            dimension_semantics=("arbitrary",), vmem_limit_bytes=VMEM_LIMIT_BYTES),
        name="moe_experts",
    )(first_block, n_blocks, total_blocks, xs, w1, w3, w2)


def _gather_expert_rows(dest_hbm, ys_hbm, idx_smem, ybuf_ref, idx_sem, row_sem):
    i = pl.program_id(0)
    n_tiles = pl.num_programs(0)
    n_idx = idx_smem[0].shape[0]
    tile = n_idx // TOP_K

    def idx_copy(tile_index, s):
        src = dest_hbm.at[pl.ds(pl.multiple_of(tile_index * n_idx, n_idx), n_idx)]
        return pltpu.make_async_copy(src, idx_smem[s], idx_sem.at[s])

    def start_rows(s):
        def body(t, carry):
            for k in range(TOP_K):
                row = k * tile + t
                pltpu.make_async_copy(_token_tile(ys_hbm, idx_smem[s][row]),
                                      _token_tile(ybuf_ref.at[s], row), row_sem.at[s]).start(priority=k)
            return carry
        lax.fori_loop(0, tile, body, 0, unroll=8)

    def wait_rows(s):
        pltpu.make_async_copy(ybuf_ref.at[s], ybuf_ref.at[s], row_sem.at[s]).wait()

    @pl.when(i == 0)
    def _():
        idx_copy(0, 0).start()
        idx_copy(0, 0).wait()
        start_rows(0)

        @pl.when(n_tiles > 1)
        def _():
            idx_copy(1, 1).start()

    def step(slot):
        other = 1 - slot

        @pl.when(i + 1 < n_tiles)
        def _():
            idx_copy(i + 1, other).wait()
            start_rows(other)

        @pl.when(i + 2 < n_tiles)
        def _():
            idx_copy(i + 2, slot).start()

        wait_rows(slot)

    for slot in range(2):
        pl.when(lax.rem(i, 2) == slot)(functools.partial(step, slot))
    return lax.rem(i, 2)


def _combined_residual(slot, ybuf_ref, x_ref, g2_ref, wt_ref):
    tile = x_ref.shape[0]
    wt = wt_ref[...]
    ytiles = ybuf_ref.at[slot]
    y = (wt[:, 2 * TOP_K:2 * TOP_K + 1] * _load_token_tiles(ytiles, 0, tile)
         + wt[:, 2 * TOP_K + 1:2 * TOP_K + 2] * _load_token_tiles(ytiles, tile, tile))
    return x_ref[...] + _per_batch(y, lambda y3: y3 * g2_ref[...])


def _combine_final_kernel(dest_hbm, ys_hbm, x_ref, g2_ref, wt_ref, fg_ref, o_ref,
                          idx0_smem, idx1_smem, ybuf_ref, idx_sem, row_sem):
    slot = _gather_expert_rows(dest_hbm, ys_hbm, (idx0_smem, idx1_smem), ybuf_ref, idx_sem, row_sem)
    out = _combined_residual(slot, ybuf_ref, x_ref, g2_ref, wt_ref)
    ms = jnp.mean(out * out, axis=-1, keepdims=True)
    o_ref[...] = out * lax.rsqrt(ms + NORM_EPS) * fg_ref[...]


def _combine_inproj_kernel(dest_hbm, ys_hbm, x_ref, g2_ref, wt_ref, g_ref, sc_ref, sh_ref, w_ref,
                           xo_ref, p_ref, idx0_smem, idx1_smem, ybuf_ref, idx_sem, row_sem):
    slot = _gather_expert_rows(dest_hbm, ys_hbm, (idx0_smem, idx1_smem), ybuf_ref, idx_sem, row_sem)
    out = _combined_residual(slot, ybuf_ref, x_ref, g2_ref, wt_ref)
    xo_ref[...] = out
    h = _norm_modulate(out, g_ref[...], sc_ref[...], sh_ref[...])
    p_ref[...] = _dot(h.astype(BF16), w_ref[...]).astype(p_ref.dtype)


def _combine(l, dest_tiles, ys, x, mod, wt, final_g=None, next_layer=None):
    t, d = x.shape
    any_spec = pl.BlockSpec(memory_space=pl.ANY)
    row_tile = lambda cols: pl.BlockSpec((COMBINE_ROWS, cols), lambda i: (i, 0))
    in_specs = [any_spec, any_spec, row_tile(d), _layer_spec(mod, l, MOD_GATE2), row_tile(SUBLANES)]
    if next_layer is None:
        body, operands = _combine_final_kernel, (final_g,)
        in_specs += [pl.BlockSpec((1, d), lambda i: (0, 0))]
        out_shape = jax.ShapeDtypeStruct((t, d), F32)
        out_specs = row_tile(d)
    else:
        mix_g, w_in = next_layer
        n = w_in.shape[2]
        body, operands = _combine_inproj_kernel, (mix_g, mod, mod, w_in)
        in_specs += [_layer_spec(mix_g, l + 1), _layer_spec(mod, l + 1, MOD_SCALE1),
                     _layer_spec(mod, l + 1, MOD_SHIFT1), _layer_spec(w_in, l + 1)]
        out_shape = (jax.ShapeDtypeStruct((t, d), F32), jax.ShapeDtypeStruct((t, n), BF16))
        out_specs = (row_tile(d), row_tile(n))
    return pl.pallas_call(
        body,
        out_shape=out_shape,
        grid=(t // COMBINE_ROWS,),
        in_specs=in_specs,
        out_specs=out_specs,
        scratch_shapes=[
            pltpu.SMEM((TOP_K * COMBINE_ROWS,), jnp.int32),
            pltpu.SMEM((TOP_K * COMBINE_ROWS,), jnp.int32),
            pltpu.VMEM((2, TOP_K * COMBINE_ROWS * SUBLANES, LANES), F32),
            pltpu.SemaphoreType.DMA((2,)),
            pltpu.SemaphoreType.DMA((2,)),
        ],
        compiler_params=pltpu.CompilerParams(
            dimension_semantics=("arbitrary",), vmem_limit_bytes=VMEM_LIMIT_BYTES),
        name="moe_combine",
    )(dest_tiles, ys, x, mod, wt, *operands)


def _ssm_params(lam_re, lam_im, log_step, b_re, b_im, c_re, c_im):
    step = jnp.exp(log_step)[:, None]
    mag = jnp.exp(lam_re * step)
    a_re, a_im = mag * jnp.cos(lam_im * step), mag * jnp.sin(lam_im * step)
    den = lam_re * lam_re + lam_im * lam_im
    n_re, n_im = a_re - 1.0, a_im
    k_re = (n_re * lam_re + n_im * lam_im) / den
    k_im = (n_im * lam_re - n_re * lam_im) / den
    bb_re = k_re[..., None] * b_re - k_im[..., None] * b_im
    bb_im = k_re[..., None] * b_im + k_im[..., None] * b_re
    gpc = SSM_GROUPS // SSM_CHUNKS
    eye = jnp.eye(gpc, dtype=F32)

    def in_blocks(bb):
        bb = bb.reshape(SSM_CHUNKS, gpc, SSM_STATE, SSM_GROUP)
        return jnp.einsum("cgph,gk->cghkp", bb, eye).reshape(SSM_CHUNKS, SSM_CHUNK_IN, SSM_CHUNK_STATE)

    def out_blocks(cc):
        cc = cc.reshape(SSM_CHUNKS, gpc, SSM_GROUP, SSM_STATE)
        return jnp.einsum("cghp,gk->cgpkh", cc, eye).reshape(SSM_CHUNKS, SSM_CHUNK_STATE, SSM_CHUNK_IN)

    wb = jnp.concatenate([in_blocks(bb_re), in_blocks(bb_im)], axis=2).astype(BF16)
    wc = jnp.concatenate([out_blocks(c_re), out_blocks(-c_im)], axis=1).astype(BF16)
    bcast = lambda a: jnp.broadcast_to(a.reshape(SSM_CHUNKS, 1, SSM_CHUNK_STATE),
                                       (SSM_CHUNKS, BATCH, SSM_CHUNK_STATE))
    return wb, wc, bcast(a_re), bcast(a_im)


def _rows8(w):
    depth, k, c = w.shape
    return jnp.broadcast_to(w[:, :, None, :], (depth, k, SUBLANES, c)).reshape(depth, k * SUBLANES, c)


def _router_weights(w_rg, b_rg, w_re, b_re):
    depth, d, _ = w_rg.shape
    pad = SUBLANES - N_GROUPS
    tail = LANES - ROUTER_ROWS
    w_cols = jnp.concatenate([w_rg, jnp.zeros((depth, d, pad), F32), w_re,
                              jnp.zeros((depth, d, tail), F32)], axis=2)
    b = jnp.concatenate([b_rg, jnp.zeros((depth, pad), F32), b_re, jnp.zeros((depth, tail), F32)], axis=1)
    hi, lo = _split_bf16(w_cols)
    return hi, lo, b.reshape(depth, 1, LANES)


def _routing_tables(route, counts):
    t = route.shape[1]
    counts = counts.astype(jnp.int32)
    padded = (counts + EXPERT_ROWS - 1) // EXPERT_ROWS * EXPERT_ROWS
    pad_end = jnp.cumsum(padded)
    pad_start = pad_end - padded
    experts = route[0:TOP_K].astype(jnp.int32)
    ranks = route[TOP_K:2 * TOP_K].astype(jnp.int32)
    expert_ids = jnp.arange(N_EXPERTS, dtype=jnp.int32)
    is_expert = experts[:, :, None] == expert_ids
    dest = jnp.sum(jnp.where(is_expert, pad_start, 0), axis=-1) + ranks

    def tiles(rows_per_tile):
        n = t // rows_per_tile
        return dest.reshape(TOP_K, n, rows_per_tile).transpose(1, 0, 2).reshape(-1)

    capacity = TOP_K * t + N_EXPERTS * EXPERT_ROWS
    first_block = pad_start // EXPERT_ROWS
    n_blocks = padded // EXPERT_ROWS
    last_start = pad_end - EXPERT_ROWS
    last_valid = counts - (padded - EXPERT_ROWS)
    piece = jnp.arange(EXPERT_ROWS // ZERO_ROWS, dtype=jnp.int32)
    has_pad = (counts[:, None] > 0) & ((piece[None, :] + 1) * ZERO_ROWS > last_valid[:, None])
    zstart = jnp.where(has_pad, last_start[:, None] + piece[None, :] * ZERO_ROWS, -1).reshape(-1)
    total_blocks = (pad_end[-1] // EXPERT_ROWS).reshape(1)
    return (tiles(DISPATCH_ROWS), tiles(COMBINE_ROWS), first_block.astype(jnp.int32), n_blocks.astype(jnp.int32),
            total_blocks.astype(jnp.int32), zstart.astype(jnp.int32), capacity)


def kernel(x, c, norm_mix_g, norm_ffn_g, w_ada, b_ada, w_in, lam_re, lam_im, log_step, ssm_b_re, ssm_b_im, ssm_c_re, ssm_c_im, ssm_d, w_glu, conf_dw_w, conf_dw_b, conf_ln_g, conf_ln_b, sconv_w, w_branch, w_gate, b_gate, w_out, w_router_group, b_router_group, w_router_expert, b_router_expert, w_exp_gate, w_exp_up, w_exp_down, final_norm_g):
    bsz, seq, d = x.shape
    assert bsz == BATCH and d == D_MODEL
    depth = w_in.shape[0]
    t = bsz * seq
    rows = lambda v: v.reshape(depth, 1, -1)

    mix_g, ffn_g = rows(norm_mix_g), rows(norm_ffn_g)
    seq_params = (*jax.vmap(_ssm_params)(lam_re, lam_im, log_step, ssm_b_re, ssm_b_im, ssm_c_re, ssm_c_im),
                  rows(ssm_d), w_glu.astype(BF16), _rows8(conf_dw_w), rows(conf_dw_b), rows(conf_ln_g),
                  rows(conf_ln_b), _rows8(sconv_w))
    w_in_b, w_gate_b, w_branch_b, w_out_b = (w.astype(BF16) for w in (w_in, w_gate, w_branch, w_out))
    r_hi, r_lo, r_b = _router_weights(w_router_group, b_router_group, w_router_expert, b_router_expert)
    tri = jnp.triu(jnp.ones((ROUTE_ROWS, ROUTE_ROWS), BF16))

    mod = _ada(c, w_ada, b_ada)
    xt = jnp.transpose(x, (1, 0, 2)).reshape(t, d)

    proj = _inproj(0, xt, mix_g, mod, w_in_b)
    for l in range(depth):
        ycat = _seq_mixers(l, proj, *seq_params)
        xt, h2, route, route_t, counts = _merge_router(l, xt, mix_g, ffn_g, mod, ycat, w_gate_b, rows(b_gate),
                                                       w_branch_b, w_out_b, r_hi, r_lo, r_b, tri)
        (dest_d, dest_c, first_block, n_blocks, total_blocks, zstart,
         cap) = _routing_tables(route, counts[:, 0])
        xs = _dispatch(zstart, dest_d, h2, cap)
        ys = _experts(l, first_block, n_blocks, total_blocks, xs, w_exp_gate, w_exp_up, w_exp_down)
        if l + 1 < depth:
            xt, proj = _combine(l, dest_c, ys, xt, mod, route_t, next_layer=(mix_g, w_in_b))
        else:
            xt = _combine(l, dest_c, ys, xt, mod, route_t, final_g=final_norm_g.reshape(1, d))

    return jnp.transpose(xt.reshape(seq, bsz, d), (1, 0, 2))
```

```python
import functools
import math

import jax
import jax.numpy as jnp
from jax import lax
from jax.experimental import pallas as pl
from jax.experimental.pallas import tpu as pltpu

F32 = jnp.float32
BF16 = jnp.bfloat16

D_MODEL = 1024
BATCH = 8
BRANCH_WIDTH = D_MODEL // 2
N_BRANCH = 3
N_IN_CHUNKS = 6
SSM_GROUP = 16
SSM_GROUPS = BRANCH_WIDTH // SSM_GROUP
SSM_STATE = 64
CONF_KERNEL = 31
SCONV_KERNEL = 3
N_GROUPS = 4
EXPERTS_PER_GROUP = 8
N_EXPERTS = N_GROUPS * EXPERTS_PER_GROUP
TOP_K = 2
D_FF_EXPERT = D_MODEL // 2
NORM_EPS = 1e-6

SUBLANES = 8
LANES = 128
VMEM_LIMIT_BYTES = 56 * 1024 * 1024

SSM_CHUNKS = 2
SSM_CHUNK_IN = BRANCH_WIDTH // SSM_CHUNKS
SSM_CHUNK_STATE = SSM_GROUPS * SSM_STATE // SSM_CHUNKS
CONF_HIST = (CONF_KERNEL - 1) * BATCH
SCONV_HIST = (SCONV_KERNEL - 1) * BATCH
ROUTER_ROWS = 40
MOD_SHIFT1, MOD_SCALE1, MOD_GATE1, MOD_SHIFT2, MOD_SCALE2, MOD_GATE2 = range(6)

ADA_COLS = 1536
INPROJ_ROWS = 1024
SEQ_ROWS = 512
CONV_ROWS = 64
EW_ROWS = 32
MERGE_ROWS = 512
ROUTE_ROWS = MERGE_ROWS
DISPATCH_ROWS = 512
EXPERT_ROWS = 512
ZERO_ROWS = 64
COMBINE_ROWS = 512


def _dot(a, b):
    return jnp.dot(a, b, preferred_element_type=F32)


def _split_bf16(v):
    hi = v.astype(BF16)
    lo = (v - hi.astype(F32)).astype(BF16)
    return hi, lo


def _sigmoid(v):
    return 1.0 / (1.0 + jnp.exp(-v))


def _gelu_tanh(v):
    return 0.5 * v * (1.0 + jnp.tanh(math.sqrt(2.0 / math.pi) * (v + 0.044715 * (v * v * v))))


def _per_batch(v, fn):
    rows, d = v.shape
    return fn(v.reshape(rows // BATCH, BATCH, d)).reshape(rows, d)


def _load_token_tiles(ref, first_token, n_tokens):
    base = first_token * SUBLANES
    chunks = [ref[pl.ds(base + s, n_tokens, stride=SUBLANES), :] for s in range(SUBLANES)]
    return jnp.concatenate(chunks, axis=1)


def _store_token_tiles(ref, value):
    for s in range(SUBLANES):
        ref[pl.ds(s, value.shape[0], stride=SUBLANES), :] = value[:, s * LANES:(s + 1) * LANES]


def _token_tile(ref, token):
    return ref.at[pl.ds(pl.multiple_of(token * SUBLANES, SUBLANES), SUBLANES)]


def _layer_spec(arr, l, col_block=None):
    tail = arr.shape[1:]
    if col_block is None:
        zeros = (0,) * len(tail)
        return pl.BlockSpec((None,) + tail, lambda *_: (l,) + zeros)
    return pl.BlockSpec((None, tail[0], D_MODEL), lambda *_: (l, 0, col_block))


def _norm_modulate(x, g, scale, shift):
    ms = jnp.mean(x * x, axis=-1, keepdims=True)
    h = x * lax.rsqrt(ms + NORM_EPS) * g
    return _per_batch(h, lambda h3: h3 * (1.0 + scale) + shift)


def _ada_kernel(c_ref, w_ref, b_ref, o_ref):
    c = c_ref[...]
    cond = c * _sigmoid(c)
    chi, clo = _split_bf16(cond)
    whi, wlo = _split_bf16(w_ref[...])
    o_ref[...] = _dot(chi, whi) + _dot(chi, wlo) + _dot(clo, whi) + b_ref[...]


def _ada(c, w_ada, b_ada):
    depth, d, n = w_ada.shape
    return pl.pallas_call(
        _ada_kernel,
        out_shape=jax.ShapeDtypeStruct((depth, BATCH, n), F32),
        grid=(depth, n // ADA_COLS),
        in_specs=[
            pl.BlockSpec((BATCH, d), lambda l, j: (0, 0)),
            pl.BlockSpec((None, d, ADA_COLS), lambda l, j: (l, 0, j)),
            pl.BlockSpec((None, 1, ADA_COLS), lambda l, j: (l, 0, j)),
        ],
        out_specs=pl.BlockSpec((None, BATCH, ADA_COLS), lambda l, j: (l, 0, j)),
        compiler_params=pltpu.CompilerParams(
            dimension_semantics=("arbitrary", "arbitrary"), vmem_limit_bytes=VMEM_LIMIT_BYTES),
        name="ada_modulation",
    )(c, w_ada, b_ada.reshape(depth, 1, n))


def _inproj_kernel(x_ref, g_ref, sc_ref, sh_ref, w_ref, xt_ref, o_ref):
    bsz, steps, d = x_ref.shape
    x = pltpu.einshape("bsd->sbd", x_ref[...]).reshape(steps * bsz, d)
    xt_ref[...] = x
    h = _norm_modulate(x, g_ref[...], sc_ref[...], sh_ref[...])
    o_ref[...] = _dot(h.astype(BF16), w_ref[...]).astype(o_ref.dtype)


def _inproj_first(x, g, mod, w_in):
    bsz, seq, d = x.shape
    n = w_in.shape[2]
    steps = INPROJ_ROWS // bsz
    return pl.pallas_call(
        _inproj_kernel,
        out_shape=(jax.ShapeDtypeStruct((bsz * seq, d), F32), jax.ShapeDtypeStruct((bsz * seq, n), BF16)),
        grid=(seq // steps,),
        in_specs=[
            pl.BlockSpec((bsz, steps, d), lambda i: (0, i, 0)),
            _layer_spec(g, 0),
            _layer_spec(mod, 0, MOD_SCALE1),
            _layer_spec(mod, 0, MOD_SHIFT1),
            _layer_spec(w_in, 0),
        ],
        out_specs=(pl.BlockSpec((INPROJ_ROWS, d), lambda i: (i, 0)),
                   pl.BlockSpec((INPROJ_ROWS, n), lambda i: (i, 0))),
        compiler_params=pltpu.CompilerParams(
            dimension_semantics=("arbitrary",), vmem_limit_bytes=VMEM_LIMIT_BYTES),
        name="mixer_inproj",
    )(x, g, mod, mod, w_in)


def _seq_kernel(p_ref, wb_ref, wc_ref, are_ref, aim_ref, d_ref, wglu_ref, cw_ref, cb_ref,
                lng_ref, lnb_ref, sw_ref, o_ref, bu_ref, st_ref, zbuf_ref, cbuf_ref, qbuf_ref):
    rows = p_ref.shape[0]
    steps = rows // BATCH
    w = BRANCH_WIDTH

    @pl.when(pl.program_id(0) == 0)
    def _():
        st_ref[...] = jnp.zeros_like(st_ref)
        zbuf_ref[0:CONF_HIST, :] = jnp.zeros((CONF_HIST, w), F32)
        qbuf_ref[0:SCONV_HIST, :] = jnp.zeros((SCONV_HIST, w), F32)

    chunk_groups = EW_ROWS // SUBLANES

    def pointwise_chunk(j, carry):
        r0 = pl.multiple_of(j * EW_ROWS, EW_ROWS)
        rs = pl.ds(r0, EW_ROWS)
        gate_c = p_ref[rs, 4 * w:5 * w].astype(F32)
        hv = p_ref[rs, 5 * w:6 * w].astype(F32)
        qbuf_ref[pl.ds(r0 + SCONV_HIST, EW_ROWS), :] = gate_c * hv
        q3 = qbuf_ref[pl.ds(r0, EW_ROWS + SCONV_HIST), :].reshape(
            chunk_groups + SCONV_KERNEL - 1, SUBLANES, w)
        acc = q3[0:chunk_groups] * sw_ref[0:SUBLANES, :]
        for k in range(1, SCONV_KERNEL):
            acc = acc + q3[k:k + chunk_groups] * sw_ref[k * SUBLANES:(k + 1) * SUBLANES, :]
        gate_b = p_ref[rs, 3 * w:4 * w].astype(F32)
        o_ref[rs, 2 * w:3 * w] = (gate_b * acc.reshape(EW_ROWS, w)).astype(o_ref.dtype)
        v = p_ref[rs, w:2 * w].astype(F32)
        g = p_ref[rs, 2 * w:3 * w].astype(F32)
        zbuf_ref[pl.ds(r0 + CONF_HIST, EW_ROWS), :] = v * _sigmoid(g)
        return carry

    lax.fori_loop(0, rows // EW_ROWS, pointwise_chunk, 0)
    qbuf_ref[0:SCONV_HIST, :] = qbuf_ref[rows:rows + SCONV_HIST, :]

    for c in range(SSM_CHUNKS):
        u_c = p_ref[:, c * SSM_CHUNK_IN:(c + 1) * SSM_CHUNK_IN]
        bu_ref[c] = _dot(u_c, wb_ref[c])
    for c in range(SSM_CHUNKS):
        a_re = are_ref[c]
        a_im = aim_ref[c]
        bu_c = bu_ref.at[c]

        def step(t, carry, a_re=a_re, a_im=a_im, bu_c=bu_c):
            x_re, x_im = carry
            r0 = pl.multiple_of(t * BATCH, BATCH)
            n_re = a_re * x_re - a_im * x_im + bu_c[pl.ds(r0, BATCH), 0:SSM_CHUNK_STATE]
            n_im = a_re * x_im + a_im * x_re + bu_c[pl.ds(r0, BATCH), SSM_CHUNK_STATE:]
            bu_c[pl.ds(r0, BATCH), 0:SSM_CHUNK_STATE] = n_re
            bu_c[pl.ds(r0, BATCH), SSM_CHUNK_STATE:] = n_im
            return n_re, n_im

        x_re, x_im = lax.fori_loop(0, steps, step, (st_ref[c, 0], st_ref[c, 1]), unroll=4)
        st_ref[c, 0] = x_re
        st_ref[c, 1] = x_im
    ys = [_dot(bu_ref[c].astype(BF16), wc_ref[c]) for c in range(SSM_CHUNKS)]
    u = p_ref[:, 0:w].astype(F32)
    y = _gelu_tanh(jnp.concatenate(ys, axis=1) + d_ref[...] * u)
    y_ssm = y * _sigmoid(_dot(y.astype(BF16), wglu_ref[...]))
    o_ref[:, 0:w] = y_ssm.astype(o_ref.dtype)

    win_groups = (CONV_ROWS + CONF_HIST) // SUBLANES
    out_groups = CONV_ROWS // SUBLANES

    def conv_chunk(j, carry):
        r0 = pl.multiple_of(j * CONV_ROWS, CONV_ROWS)
        for s in range(w // LANES):
            ls = slice(s * LANES, (s + 1) * LANES)
            win = zbuf_ref[pl.ds(r0, CONV_ROWS + CONF_HIST), ls].reshape(win_groups, SUBLANES, LANES)
            acc = win[0:out_groups] * cw_ref[0:SUBLANES, ls]
            for k in range(1, CONF_KERNEL):
                acc = acc + win[k:k + out_groups] * cw_ref[k * SUBLANES:(k + 1) * SUBLANES, ls]
            cbuf_ref[pl.ds(r0, CONV_ROWS), ls] = acc.reshape(CONV_ROWS, LANES)
        return carry

    lax.fori_loop(0, rows // CONV_ROWS, conv_chunk, 0)
    zbuf_ref[0:CONF_HIST, :] = zbuf_ref[rows:rows + CONF_HIST, :]
    cv = cbuf_ref[...] + cb_ref[...]
    cc = cv - jnp.mean(cv, axis=-1, keepdims=True)
    ln = cc * lax.rsqrt(jnp.mean(cc * cc, axis=-1, keepdims=True) + NORM_EPS) * lng_ref[...] + lnb_ref[...]
    o_ref[:, w:2 * w] = (ln * _sigmoid(ln)).astype(o_ref.dtype)


def _seq_mixers(l, proj, *layer_params):
    t = proj.shape[0]
    w = BRANCH_WIDTH
    return pl.pallas_call(
        _seq_kernel,
        out_shape=jax.ShapeDtypeStruct((t, N_BRANCH * w), BF16),
        grid=(t // SEQ_ROWS,),
        in_specs=[pl.BlockSpec((SEQ_ROWS, N_IN_CHUNKS * w), lambda i: (i, 0))]
                 + [_layer_spec(p, l) for p in layer_params],
        out_specs=pl.BlockSpec((SEQ_ROWS, N_BRANCH * w), lambda i: (i, 0)),
        scratch_shapes=[
            pltpu.VMEM((SSM_CHUNKS, SEQ_ROWS, 2 * SSM_CHUNK_STATE), F32),
            pltpu.VMEM((SSM_CHUNKS, 2, BATCH, SSM_CHUNK_STATE), F32),
            pltpu.VMEM((SEQ_ROWS + CONF_HIST, w), F32),
            pltpu.VMEM((SEQ_ROWS, w), F32),
            pltpu.VMEM((SEQ_ROWS + SCONV_HIST, w), F32),
        ],
        compiler_params=pltpu.CompilerParams(
            dimension_semantics=("arbitrary",), vmem_limit_bytes=VMEM_LIMIT_BYTES),
        name="seq_mixers",
    )(proj, *layer_params)


def _merge_router_kernel(x_ref, g_ref, sc_ref, sh_ref, g1_ref, y_ref, wg_ref, bg_ref, wbr_ref, wo_ref,
                         g2_ref, sc2_ref, sh2_ref, whi_ref, wlo_ref, b_ref, tri_ref,
                         o_ref, h_ref, route_ref, route_t_ref, cnt_ref, run_ref):
    x = x_ref[...]
    d = x.shape[1]
    w = BRANCH_WIDTH
    hb = _norm_modulate(x, g_ref[...], sc_ref[...], sh_ref[...]).astype(BF16)
    merged = None
    for n in range(N_BRANCH):
        gate = _sigmoid(_dot(hb, wg_ref[:, n * d:(n + 1) * d]) + bg_ref[:, n * d:(n + 1) * d])
        term = gate * _dot(y_ref[:, n * w:(n + 1) * w], wbr_ref[n])
        merged = term if merged is None else merged + term
    out = _dot(merged.astype(BF16), wo_ref[...])
    x_new = x + _per_batch(out, lambda o3: o3 * g1_ref[...])
    o_ref[...] = x_new
    _route_tile(x_new, g2_ref, sc2_ref, sh2_ref, whi_ref, wlo_ref, b_ref, tri_ref,
                h_ref, route_ref, route_t_ref, cnt_ref, run_ref)


def _merge_router(l, x, mix_g, ffn_g, mod, ycat, w_gate, b_gate, w_branch, w_out, r_hi, r_lo, r_bias, tri):
    t, d = x.shape
    w = BRANCH_WIDTH
    row_tile = lambda cols: pl.BlockSpec((MERGE_ROWS, cols), lambda i: (i, 0))
    return pl.pallas_call(
        _merge_router_kernel,
        out_shape=(
            jax.ShapeDtypeStruct((t, d), F32),
            jax.ShapeDtypeStruct((t * SUBLANES, LANES), F32),
            jax.ShapeDtypeStruct((SUBLANES, t), F32),
            jax.ShapeDtypeStruct((t, SUBLANES), F32),
            jax.ShapeDtypeStruct((N_EXPERTS, MERGE_ROWS), F32),
        ),
        grid=(t // MERGE_ROWS,),
        in_specs=[
            row_tile(d),
            _layer_spec(mix_g, l),
            _layer_spec(mod, l, MOD_SCALE1),
            _layer_spec(mod, l, MOD_SHIFT1),
            _layer_spec(mod, l, MOD_GATE1),
            row_tile(N_BRANCH * w),
            _layer_spec(w_gate, l),
            _layer_spec(b_gate, l),
            _layer_spec(w_branch, l),
            _layer_spec(w_out, l),
            _layer_spec(ffn_g, l),
            _layer_spec(mod, l, MOD_SCALE2),
            _layer_spec(mod, l, MOD_SHIFT2),
            _layer_spec(r_hi, l),
            _layer_spec(r_lo, l),
            _layer_spec(r_bias, l),
            pl.BlockSpec((MERGE_ROWS, MERGE_ROWS), lambda i: (0, 0)),
        ],
        out_specs=(
            row_tile(d),
            pl.BlockSpec((MERGE_ROWS * SUBLANES, LANES), lambda i: (i, 0)),
            pl.BlockSpec((SUBLANES, MERGE_ROWS), lambda i: (0, i)),
            row_tile(SUBLANES),
            pl.BlockSpec((N_EXPERTS, MERGE_ROWS), lambda i: (0, 0)),
        ),
        scratch_shapes=[pltpu.VMEM((N_EXPERTS, MERGE_ROWS), F32)],
        compiler_params=pltpu.CompilerParams(
            dimension_semantics=("arbitrary",), vmem_limit_bytes=VMEM_LIMIT_BYTES),
        name="mixer_merge_router",
    )(x, mix_g, mod, mod, mod, ycat, w_gate, b_gate, w_branch, w_out, ffn_g, mod, mod, r_hi, r_lo, r_bias, tri)


def _route_tile(x, g_ref, sc_ref, sh_ref, whi_ref, wlo_ref, b_ref, tri_ref,
                h_ref, route_ref, route_t_ref, cnt_ref, run_ref):
    @pl.when(pl.program_id(0) == 0)
    def _():
        run_ref[...] = jnp.zeros_like(run_ref)

    h = _norm_modulate(x, g_ref[...], sc_ref[...], sh_ref[...])
    _store_token_tiles(h_ref, h)
    rows = h.shape[0]
    hhi, hlo = _split_bf16(h)
    logits = (_dot(hhi, whi_ref[...]) + _dot(hhi, wlo_ref[...]) + _dot(hlo, whi_ref[...]) + b_ref[...]).T
    lg = logits[0:N_GROUPS]
    le = logits[SUBLANES:SUBLANES + N_EXPERTS]

    g_row = lax.broadcasted_iota(jnp.int32, (N_GROUPS, rows), 0).astype(F32)
    g_max = jnp.max(lg, axis=0, keepdims=True)
    g_sel = jnp.min(jnp.where(lg == g_max, g_row, float(N_GROUPS)), axis=0, keepdims=True)
    g_w = 1.0 / jnp.sum(jnp.exp(lg - g_max), axis=0, keepdims=True)

    e_row = lax.broadcasted_iota(jnp.int32, (N_EXPERTS, rows), 0).astype(F32)
    e_grp = jnp.floor(e_row * (1.0 / EXPERTS_PER_GROUP))
    neg = float("-inf")
    lm = jnp.where(e_grp == g_sel, le, neg)
    v1 = jnp.max(lm, axis=0, keepdims=True)
    i1 = jnp.min(jnp.where(lm == v1, e_row, float(N_EXPERTS)), axis=0, keepdims=True)
    lm2 = jnp.where(e_row == i1, neg, lm)
    v2 = jnp.max(lm2, axis=0, keepdims=True)
    i2 = jnp.min(jnp.where(lm2 == v2, e_row, float(N_EXPERTS)), axis=0, keepdims=True)
    e21 = jnp.exp(v2 - v1)
    p1 = 1.0 / (1.0 + e21)
    w1 = p1 * g_w
    w2 = e21 * p1 * g_w

    hit1 = e_row == i1
    hit2 = e_row == i2
    onehot = jnp.where(hit1 | hit2, 1.0, 0.0)
    cum = _dot(onehot.astype(BF16), tri_ref[...])
    before = cum - onehot + run_ref[...]
    r1 = jnp.sum(jnp.where(hit1, before, 0.0), axis=0, keepdims=True)
    r2 = jnp.sum(jnp.where(hit2, before, 0.0), axis=0, keepdims=True)
    run_ref[...] = run_ref[...] + jnp.broadcast_to(cum[:, rows - 1:rows], run_ref.shape)
    cnt_ref[...] = run_ref[...]
    zero = jnp.zeros_like(w1)
    route = jnp.concatenate([i1, i2, r1, r2, w1, w2, zero, zero], axis=0)
    route_ref[...] = route
    route_t_ref[...] = route.T


def _row_copy(src_hbm, src_token, dst_hbm, dst_token, sem):
    return pltpu.make_async_copy(_token_tile(src_hbm, src_token), _token_tile(dst_hbm, dst_token), sem)


def _dispatch_kernel(zstart_ref, dest_hbm, h_ref, xs_hbm, idx0_smem, idx1_smem, zeros_ref,
                     idx_sem, zero_sem, row_sem):
    i = pl.program_id(0)
    n_tiles = pl.num_programs(0)
    idx_smem = (idx0_smem, idx1_smem)
    n_idx = idx0_smem.shape[0]
    tile = n_idx // TOP_K

    def idx_copy(tile_index, s):
        src = dest_hbm.at[pl.ds(pl.multiple_of(tile_index * n_idx, n_idx), n_idx)]
        return pltpu.make_async_copy(src, idx_smem[s], idx_sem.at[s])

    @pl.when(i == 0)
    def _():
        idx_copy(0, 0).start()

    def zero_copy(j):
        piece_rows = ZERO_ROWS * SUBLANES
        start = pl.multiple_of(zstart_ref[j] * SUBLANES, piece_rows)
        return pltpu.make_async_copy(zeros_ref, xs_hbm.at[pl.ds(start, piece_rows)], zero_sem)

    @pl.when(i == 0)
    def _():
        zeros_ref[...] = jnp.zeros_like(zeros_ref)

        def start(j, carry):
            @pl.when(zstart_ref[j] >= 0)
            def _():
                zero_copy(j).start()
            return carry

        def wait(j, carry):
            @pl.when(zstart_ref[j] >= 0)
            def _():
                zero_copy(j).wait()
            return carry

        lax.fori_loop(0, zstart_ref.shape[0], start, 0)
        lax.fori_loop(0, zstart_ref.shape[0], wait, 0)

    def step(slot):
        @pl.when(i + 1 < n_tiles)
        def _():
            idx_copy(i + 1, 1 - slot).start()

        idx_copy(i, slot).wait()

        def start_rows(t, carry):
            for k in range(TOP_K):
                _row_copy(h_ref, t, xs_hbm, idx_smem[slot][k * tile + t], row_sem).start(priority=k)
            return carry

        lax.fori_loop(0, tile, start_rows, 0, unroll=8)
        for k in range(TOP_K):
            pltpu.make_async_copy(h_ref, xs_hbm.at[pl.ds(0, tile * SUBLANES)], row_sem).wait()

    for slot in range(2):
        pl.when(lax.rem(i, 2) == slot)(functools.partial(step, slot))


def _dispatch(zstart, dest_tiles, h, cap):
    t = h.shape[0] // SUBLANES
    any_spec = pl.BlockSpec(memory_space=pl.ANY)
    return pl.pallas_call(
        _dispatch_kernel,
        out_shape=jax.ShapeDtypeStruct((cap * SUBLANES, LANES), h.dtype),
        grid_spec=pltpu.PrefetchScalarGridSpec(
            num_scalar_prefetch=1,
            grid=(t // DISPATCH_ROWS,),
            in_specs=[any_spec,
                      pl.BlockSpec((DISPATCH_ROWS * SUBLANES, LANES), lambda i, zs: (i, 0))],
            out_specs=any_spec,
            scratch_shapes=[
                pltpu.SMEM((TOP_K * DISPATCH_ROWS,), jnp.int32),
                pltpu.SMEM((TOP_K * DISPATCH_ROWS,), jnp.int32),
                pltpu.VMEM((ZERO_ROWS * SUBLANES, LANES), h.dtype),
                pltpu.SemaphoreType.DMA((2,)),
                pltpu.SemaphoreType.DMA,
                pltpu.SemaphoreType.DMA,
            ],
        ),
        compiler_params=pltpu.CompilerParams(
            dimension_semantics=("arbitrary",), has_side_effects=True),
        name="moe_dispatch",
    )(zstart, dest_tiles, h)


def _expert_kernel(first_ref, nblk_ref, total_ref, xs_hbm, w1_ref, w3_ref, w2_ref, ys_hbm,
                   w1b_ref, w3b_ref, w2b_ref, xbuf_ref, ybuf_ref, in_sem, out_sem):
    e = pl.program_id(0)
    n_blocks = nblk_ref[e]
    first = first_ref[e]
    total = total_ref[0]
    block_rows = EXPERT_ROWS * SUBLANES

    def rows_of(block):
        return pl.ds(pl.multiple_of(block * block_rows, block_rows), block_rows)

    def in_copy(block, slot):
        return pltpu.make_async_copy(xs_hbm.at[rows_of(block)], xbuf_ref.at[slot], in_sem.at[slot])

    def out_copy(block, slot):
        return pltpu.make_async_copy(ybuf_ref.at[slot], ys_hbm.at[rows_of(block)], out_sem.at[slot])

    @pl.when(n_blocks > 0)
    def _():
        @pl.when(first == 0)
        def _():
            in_copy(0, 0).start()

        w1b_ref[...] = w1_ref[...].astype(BF16)
        w3b_ref[...] = w3_ref[...].astype(BF16)
        w2b_ref[...] = w2_ref[...].astype(BF16)

        def block_step(j, carry):
            g = first + j
            slot = lax.rem(g, 2)

            @pl.when(g + 1 < total)
            def _():
                in_copy(g + 1, 1 - slot).start(priority=1)

            in_copy(g, slot).wait()

            @pl.when(g >= 2)
            def _():
                out_copy(g - 2, slot).wait()

            xb = _load_token_tiles(xbuf_ref.at[slot], 0, EXPERT_ROWS).astype(BF16)
            a = _dot(xb, w1b_ref[...])
            hid = a * _sigmoid(a) * _dot(xb, w3b_ref[...])
            _store_token_tiles(ybuf_ref.at[slot], _dot(hid.astype(BF16), w2b_ref[...]))
            out_copy(g, slot).start()
            return carry

        lax.fori_loop(0, n_blocks, block_step, 0)

        @pl.when(first + n_blocks == total)
        def _():
            @pl.when(total >= 2)
            def _():
                out_copy(total - 2, lax.rem(total, 2)).wait()

            out_copy(total - 1, lax.rem(total - 1, 2)).wait()


def _experts(l, first_block, n_blocks, total_blocks, xs, w1, w3, w2):
    _, n_experts, d, f = w1.shape
    tile = (EXPERT_ROWS * SUBLANES, LANES)
    any_spec = pl.BlockSpec(memory_space=pl.ANY)
    weight_spec = lambda rows, cols: pl.BlockSpec((None, None, rows, cols), lambda e, *_: (l, e, 0, 0))
    return pl.pallas_call(
        _expert_kernel,
        out_shape=jax.ShapeDtypeStruct(xs.shape, F32),
        grid_spec=pltpu.PrefetchScalarGridSpec(
            num_scalar_prefetch=3,
            grid=(n_experts,),
            in_specs=[any_spec, weight_spec(d, f), weight_spec(d, f), weight_spec(f, d)],
            out_specs=any_spec,
            scratch_shapes=[
                pltpu.VMEM((d, f), BF16), pltpu.VMEM((d, f), BF16), pltpu.VMEM((f, d), BF16),
                pltpu.VMEM((2,) + tile, F32), pltpu.VMEM((2,) + tile, F32),
                pltpu.SemaphoreType.DMA((2,)), pltpu.SemaphoreType.DMA((2,)),
            ],
        ),
        compiler_params=pltpu.CompilerParams(
            dimension_semantics=("arbitrary",), vmem_limit_bytes=VMEM_LIMIT_BYTES),
        name="moe_experts",
    )(first_block, n_blocks, total_blocks, xs, w1, w3, w2)


def _gather_expert_rows(dest_hbm, ys_hbm, idx_smem, ybuf_ref, idx_sem, row_sem):
    i = pl.program_id(0)
    n_tiles = pl.num_programs(0)
    n_idx = idx_smem[0].shape[0]
    tile = n_idx // TOP_K

    def idx_copy(tile_index, s):
        src = dest_hbm.at[pl.ds(pl.multiple_of(tile_index * n_idx, n_idx), n_idx)]
        return pltpu.make_async_copy(src, idx_smem[s], idx_sem.at[s])

    def start_rows(s):
        def body(t, carry):
            for k in range(TOP_K):
                row = k * tile + t
                pltpu.make_async_copy(_token_tile(ys_hbm, idx_smem[s][row]),
                                      _token_tile(ybuf_ref.at[s], row), row_sem.at[s]).start(priority=k)
            return carry
        lax.fori_loop(0, tile, body, 0, unroll=8)

    def wait_rows(s):
        pltpu.make_async_copy(ybuf_ref.at[s], ybuf_ref.at[s], row_sem.at[s]).wait()

    @pl.when(i == 0)
    def _():
        idx_copy(0, 0).start()
        idx_copy(0, 0).wait()
        start_rows(0)

        @pl.when(n_tiles > 1)
        def _():
            idx_copy(1, 1).start()

    def step(slot):
        other = 1 - slot

        @pl.when(i + 1 < n_tiles)
        def _():
            idx_copy(i + 1, other).wait()
            start_rows(other)

        @pl.when(i + 2 < n_tiles)
        def _():
            idx_copy(i + 2, slot).start()

        wait_rows(slot)

    for slot in range(2):
        pl.when(lax.rem(i, 2) == slot)(functools.partial(step, slot))
    return lax.rem(i, 2)


def _combined_residual(slot, ybuf_ref, x_ref, g2_ref, wt_ref):
    tile = x_ref.shape[0]
    wt = wt_ref[...]
    ytiles = ybuf_ref.at[slot]
    y = (wt[:, 2 * TOP_K:2 * TOP_K + 1] * _load_token_tiles(ytiles, 0, tile)
         + wt[:, 2 * TOP_K + 1:2 * TOP_K + 2] * _load_token_tiles(ytiles, tile, tile))
    return x_ref[...] + _per_batch(y, lambda y3: y3 * g2_ref[...])


def _combine_final_kernel(dest_hbm, ys_hbm, x_ref, g2_ref, wt_ref, fg_ref, o_ref,
                          idx0_smem, idx1_smem, ybuf_ref, idx_sem, row_sem):
    slot = _gather_expert_rows(dest_hbm, ys_hbm, (idx0_smem, idx1_smem), ybuf_ref, idx_sem, row_sem)
    out = _combined_residual(slot, ybuf_ref, x_ref, g2_ref, wt_ref)
    ms = jnp.mean(out * out, axis=-1, keepdims=True)
    out = out * lax.rsqrt(ms + NORM_EPS) * fg_ref[...]
    bsz, steps, d = o_ref.shape
    o_ref[...] = pltpu.einshape("sbd->bsd", out.reshape(steps, bsz, d))


def _combine_inproj_kernel(dest_hbm, ys_hbm, x_ref, g2_ref, wt_ref, g_ref, sc_ref, sh_ref, w_ref,
                           xo_ref, p_ref, idx0_smem, idx1_smem, ybuf_ref, idx_sem, row_sem):
    slot = _gather_expert_rows(dest_hbm, ys_hbm, (idx0_smem, idx1_smem), ybuf_ref, idx_sem, row_sem)
    out = _combined_residual(slot, ybuf_ref, x_ref, g2_ref, wt_ref)
    xo_ref[...] = out
    h = _norm_modulate(out, g_ref[...], sc_ref[...], sh_ref[...])
    p_ref[...] = _dot(h.astype(BF16), w_ref[...]).astype(p_ref.dtype)


def _combine(l, dest_tiles, ys, x, mod, wt, final_g=None, next_layer=None):
    t, d = x.shape
    any_spec = pl.BlockSpec(memory_space=pl.ANY)
    row_tile = lambda cols: pl.BlockSpec((COMBINE_ROWS, cols), lambda i: (i, 0))
    in_specs = [any_spec, any_spec, row_tile(d), _layer_spec(mod, l, MOD_GATE2), row_tile(SUBLANES)]
    if next_layer is None:
        body, operands = _combine_final_kernel, (final_g,)
        in_specs += [pl.BlockSpec((1, d), lambda i: (0, 0))]
        out_shape = jax.ShapeDtypeStruct((BATCH, t // BATCH, d), F32)
        out_specs = pl.BlockSpec((BATCH, COMBINE_ROWS // BATCH, d), lambda i: (0, i, 0))
    else:
        mix_g, w_in = next_layer
        n = w_in.shape[2]
        body, operands = _combine_inproj_kernel, (mix_g, mod, mod, w_in)
        in_specs += [_layer_spec(mix_g, l + 1), _layer_spec(mod, l + 1, MOD_SCALE1),
                     _layer_spec(mod, l + 1, MOD_SHIFT1), _layer_spec(w_in, l + 1)]
        out_shape = (jax.ShapeDtypeStruct((t, d), F32), jax.ShapeDtypeStruct((t, n), BF16))
        out_specs = (row_tile(d), row_tile(n))
    return pl.pallas_call(
        body,
        out_shape=out_shape,
        grid=(t // COMBINE_ROWS,),
        in_specs=in_specs,
        out_specs=out_specs,
        scratch_shapes=[
            pltpu.SMEM((TOP_K * COMBINE_ROWS,), jnp.int32),
            pltpu.SMEM((TOP_K * COMBINE_ROWS,), jnp.int32),
            pltpu.VMEM((2, TOP_K * COMBINE_ROWS * SUBLANES, LANES), F32),
            pltpu.SemaphoreType.DMA((2,)),
            pltpu.SemaphoreType.DMA((2,)),
        ],
        compiler_params=pltpu.CompilerParams(
            dimension_semantics=("arbitrary",), vmem_limit_bytes=VMEM_LIMIT_BYTES),
        name="moe_combine",
    )(dest_tiles, ys, x, mod, wt, *operands)


def _ssm_params(lam_re, lam_im, log_step, b_re, b_im, c_re, c_im):
    step = jnp.exp(log_step)[:, None]
    mag = jnp.exp(lam_re * step)
    a_re, a_im = mag * jnp.cos(lam_im * step), mag * jnp.sin(lam_im * step)
    den = lam_re * lam_re + lam_im * lam_im
    n_re, n_im = a_re - 1.0, a_im
    k_re = (n_re * lam_re + n_im * lam_im) / den
    k_im = (n_im * lam_re - n_re * lam_im) / den
    bb_re = k_re[..., None] * b_re - k_im[..., None] * b_im
    bb_im = k_re[..., None] * b_im + k_im[..., None] * b_re
    gpc = SSM_GROUPS // SSM_CHUNKS
    eye = jnp.eye(gpc, dtype=F32)

    def in_blocks(bb):
        bb = bb.reshape(SSM_CHUNKS, gpc, SSM_STATE, SSM_GROUP)
        return jnp.einsum("cgph,gk->cghkp", bb, eye).reshape(SSM_CHUNKS, SSM_CHUNK_IN, SSM_CHUNK_STATE)

    def out_blocks(cc):
        cc = cc.reshape(SSM_CHUNKS, gpc, SSM_GROUP, SSM_STATE)
        return jnp.einsum("cghp,gk->cgpkh", cc, eye).reshape(SSM_CHUNKS, SSM_CHUNK_STATE, SSM_CHUNK_IN)

    wb = jnp.concatenate([in_blocks(bb_re), in_blocks(bb_im)], axis=2).astype(BF16)
    wc = jnp.concatenate([out_blocks(c_re), out_blocks(-c_im)], axis=1).astype(BF16)
    bcast = lambda a: jnp.broadcast_to(a.reshape(SSM_CHUNKS, 1, SSM_CHUNK_STATE),
                                       (SSM_CHUNKS, BATCH, SSM_CHUNK_STATE))
    return wb, wc, bcast(a_re), bcast(a_im)


def _rows8(w):
    depth, k, c = w.shape
    return jnp.broadcast_to(w[:, :, None, :], (depth, k, SUBLANES, c)).reshape(depth, k * SUBLANES, c)


def _router_weights(w_rg, b_rg, w_re, b_re):
    depth, d, _ = w_rg.shape
    pad = SUBLANES - N_GROUPS
    tail = LANES - ROUTER_ROWS
    w_cols = jnp.concatenate([w_rg, jnp.zeros((depth, d, pad), F32), w_re,
                              jnp.zeros((depth, d, tail), F32)], axis=2)
    b = jnp.concatenate([b_rg, jnp.zeros((depth, pad), F32), b_re, jnp.zeros((depth, tail), F32)], axis=1)
    hi, lo = _split_bf16(w_cols)
    return hi, lo, b.reshape(depth, 1, LANES)


def _routing_tables(route, counts):
    t = route.shape[1]
    counts = counts.astype(jnp.int32)
    padded = (counts + EXPERT_ROWS - 1) // EXPERT_ROWS * EXPERT_ROWS
    pad_end = jnp.cumsum(padded)
    pad_start = pad_end - padded
    experts = route[0:TOP_K].astype(jnp.int32)
    ranks = route[TOP_K:2 * TOP_K].astype(jnp.int32)
    expert_ids = jnp.arange(N_EXPERTS, dtype=jnp.int32)
    is_expert = experts[:, :, None] == expert_ids
    dest = jnp.sum(jnp.where(is_expert, pad_start, 0), axis=-1) + ranks

    def tiles(rows_per_tile):
        n = t // rows_per_tile
        return dest.reshape(TOP_K, n, rows_per_tile).transpose(1, 0, 2).reshape(-1)

    capacity = TOP_K * t + N_EXPERTS * EXPERT_ROWS
    first_block = pad_start // EXPERT_ROWS
    n_blocks = padded // EXPERT_ROWS
    last_start = pad_end - EXPERT_ROWS
    last_valid = counts - (padded - EXPERT_ROWS)
    piece = jnp.arange(EXPERT_ROWS // ZERO_ROWS, dtype=jnp.int32)
    has_pad = (counts[:, None] > 0) & ((piece[None, :] + 1) * ZERO_ROWS > last_valid[:, None])
    zstart = jnp.where(has_pad, last_start[:, None] + piece[None, :] * ZERO_ROWS, -1).reshape(-1)
    total_blocks = (pad_end[-1] // EXPERT_ROWS).reshape(1)
    return (tiles(DISPATCH_ROWS), tiles(COMBINE_ROWS), first_block.astype(jnp.int32), n_blocks.astype(jnp.int32),
            total_blocks.astype(jnp.int32), zstart.astype(jnp.int32), capacity)


def kernel(x, c, norm_mix_g, norm_ffn_g, w_ada, b_ada, w_in, lam_re, lam_im, log_step, ssm_b_re, ssm_b_im, ssm_c_re, ssm_c_im, ssm_d, w_glu, conf_dw_w, conf_dw_b, conf_ln_g, conf_ln_b, sconv_w, w_branch, w_gate, b_gate, w_out, w_router_group, b_router_group, w_router_expert, b_router_expert, w_exp_gate, w_exp_up, w_exp_down, final_norm_g):
    bsz, seq, d = x.shape
    assert bsz == BATCH and d == D_MODEL
    depth = w_in.shape[0]
    t = bsz * seq
    rows = lambda v: v.reshape(depth, 1, -1)

    mix_g, ffn_g = rows(norm_mix_g), rows(norm_ffn_g)
    seq_params = (*jax.vmap(_ssm_params)(lam_re, lam_im, log_step, ssm_b_re, ssm_b_im, ssm_c_re, ssm_c_im),
                  rows(ssm_d), w_glu.astype(BF16), _rows8(conf_dw_w), rows(conf_dw_b), rows(conf_ln_g),
                  rows(conf_ln_b), _rows8(sconv_w))
    w_in_b, w_gate_b, w_branch_b, w_out_b = (w.astype(BF16) for w in (w_in, w_gate, w_branch, w_out))
    r_hi, r_lo, r_b = _router_weights(w_router_group, b_router_group, w_router_expert, b_router_expert)
    tri = jnp.triu(jnp.ones((ROUTE_ROWS, ROUTE_ROWS), BF16))

    mod = _ada(c, w_ada, b_ada)
    xt, proj = _inproj_first(x, mix_g, mod, w_in_b)
    for l in range(depth):
        ycat = _seq_mixers(l, proj, *seq_params)
        xt, h2, route, route_t, counts = _merge_router(l, xt, mix_g, ffn_g, mod, ycat, w_gate_b, rows(b_gate),
                                                       w_branch_b, w_out_b, r_hi, r_lo, r_b, tri)
        (dest_d, dest_c, first_block, n_blocks, total_blocks, zstart,
         cap) = _routing_tables(route, counts[:, 0])
        xs = _dispatch(zstart, dest_d, h2, cap)
        ys = _experts(l, first_block, n_blocks, total_blocks, xs, w_exp_gate, w_exp_up, w_exp_down)
        if l + 1 < depth:
            xt, proj = _combine(l, dest_c, ys, xt, mod, route_t, next_layer=(mix_g, w_in_b))
        else:
            return _combine(l, dest_c, ys, xt, mod, route_t, final_g=final_norm_g.reshape(1, d))
```

```python
import functools
import math

import jax
import jax.numpy as jnp
from jax import lax
from jax.experimental import pallas as pl
from jax.experimental.pallas import tpu as pltpu

F32 = jnp.float32
BF16 = jnp.bfloat16

D_MODEL = 1024
BATCH = 8
BRANCH_WIDTH = D_MODEL // 2
N_BRANCH = 3
N_IN_CHUNKS = 6
SSM_GROUP = 16
SSM_GROUPS = BRANCH_WIDTH // SSM_GROUP
SSM_STATE = 64
CONF_KERNEL = 31
SCONV_KERNEL = 3
N_GROUPS = 4
EXPERTS_PER_GROUP = 8
N_EXPERTS = N_GROUPS * EXPERTS_PER_GROUP
TOP_K = 2
D_FF_EXPERT = D_MODEL // 2
NORM_EPS = 1e-6

SUBLANES = 8
LANES = 128
VMEM_LIMIT_BYTES = 56 * 1024 * 1024

SSM_CHUNKS = 2
SSM_CHUNK_IN = BRANCH_WIDTH // SSM_CHUNKS
SSM_CHUNK_STATE = SSM_GROUPS * SSM_STATE // SSM_CHUNKS
CONF_HIST = (CONF_KERNEL - 1) * BATCH
SCONV_HIST = (SCONV_KERNEL - 1) * BATCH
ROUTER_ROWS = 40
MOD_SHIFT1, MOD_SCALE1, MOD_GATE1, MOD_SHIFT2, MOD_SCALE2, MOD_GATE2 = range(6)

ADA_COLS = 1536
INPROJ_ROWS = 1024
SEQ_ROWS = 512
CONV_ROWS = 64
EW_ROWS = 32
MERGE_ROWS = 512
ROUTE_ROWS = MERGE_ROWS
DISPATCH_ROWS = 512
EXPERT_ROWS = 512
ZERO_ROWS = 64
COMBINE_ROWS = 512


def _dot(a, b):
    return jnp.dot(a, b, preferred_element_type=F32)


def _split_bf16(v):
    hi = v.astype(BF16)
    lo = (v - hi.astype(F32)).astype(BF16)
    return hi, lo


def _sigmoid(v):
    return 1.0 / (1.0 + jnp.exp(-v))


def _gelu_tanh(v):
    return 0.5 * v * (1.0 + jnp.tanh(math.sqrt(2.0 / math.pi) * (v + 0.044715 * (v * v * v))))


def _per_batch(v, fn):
    rows, d = v.shape
    return fn(v.reshape(rows // BATCH, BATCH, d)).reshape(rows, d)


def _load_token_tiles(ref, first_token, n_tokens):
    base = first_token * SUBLANES
    chunks = [ref[pl.ds(base + s, n_tokens, stride=SUBLANES), :] for s in range(SUBLANES)]
    return jnp.concatenate(chunks, axis=1)


def _store_token_tiles(ref, value):
    for s in range(SUBLANES):
        ref[pl.ds(s, value.shape[0], stride=SUBLANES), :] = value[:, s * LANES:(s + 1) * LANES]


def _token_tile(ref, token):
    return ref.at[pl.ds(pl.multiple_of(token * SUBLANES, SUBLANES), SUBLANES)]


def _layer_spec(arr, l, col_block=None):
    tail = arr.shape[1:]
    if col_block is None:
        zeros = (0,) * len(tail)
        return pl.BlockSpec((None,) + tail, lambda *_: (l,) + zeros)
    return pl.BlockSpec((None, tail[0], D_MODEL), lambda *_: (l, 0, col_block))


def _norm_modulate(x, g, scale, shift):
    ms = jnp.mean(x * x, axis=-1, keepdims=True)
    h = x * lax.rsqrt(ms + NORM_EPS) * g
    return _per_batch(h, lambda h3: h3 * (1.0 + scale) + shift)


def _ada_kernel(c_ref, w_ref, b_ref, o_ref):
    c = c_ref[...]
    cond = c * _sigmoid(c)
    chi, clo = _split_bf16(cond)
    whi, wlo = _split_bf16(w_ref[...])
    o_ref[...] = _dot(chi, whi) + _dot(chi, wlo) + _dot(clo, whi) + b_ref[...]


def _ada(c, w_ada, b_ada):
    depth, d, n = w_ada.shape
    return pl.pallas_call(
        _ada_kernel,
        out_shape=jax.ShapeDtypeStruct((depth, BATCH, n), F32),
        grid=(depth, n // ADA_COLS),
        in_specs=[
            pl.BlockSpec((BATCH, d), lambda l, j: (0, 0)),
            pl.BlockSpec((None, d, ADA_COLS), lambda l, j: (l, 0, j)),
            pl.BlockSpec((None, 1, ADA_COLS), lambda l, j: (l, 0, j)),
        ],
        out_specs=pl.BlockSpec((None, BATCH, ADA_COLS), lambda l, j: (l, 0, j)),
        compiler_params=pltpu.CompilerParams(
            dimension_semantics=("arbitrary", "arbitrary"), vmem_limit_bytes=VMEM_LIMIT_BYTES),
        name="ada_modulation",
    )(c, w_ada, b_ada.reshape(depth, 1, n))


def _inproj_kernel(x_ref, g_ref, sc_ref, sh_ref, w_ref, xt_ref, o_ref):
    bsz, steps, d = x_ref.shape
    x = pltpu.einshape("bsd->sbd", x_ref[...]).reshape(steps * bsz, d)
    xt_ref[...] = x
    h = _norm_modulate(x, g_ref[...], sc_ref[...], sh_ref[...])
    o_ref[...] = _dot(h.astype(BF16), w_ref[...]).astype(o_ref.dtype)


def _inproj_first(x, g, mod, w_in):
    bsz, seq, d = x.shape
    n = w_in.shape[2]
    steps = INPROJ_ROWS // bsz
    return pl.pallas_call(
        _inproj_kernel,
        out_shape=(jax.ShapeDtypeStruct((bsz * seq, d), F32), jax.ShapeDtypeStruct((bsz * seq, n), BF16)),
        grid=(seq // steps,),
        in_specs=[
            pl.BlockSpec((bsz, steps, d), lambda i: (0, i, 0)),
            _layer_spec(g, 0),
            _layer_spec(mod, 0, MOD_SCALE1),
            _layer_spec(mod, 0, MOD_SHIFT1),
            _layer_spec(w_in, 0),
        ],
        out_specs=(pl.BlockSpec((INPROJ_ROWS, d), lambda i: (i, 0)),
                   pl.BlockSpec((INPROJ_ROWS, n), lambda i: (i, 0))),
        compiler_params=pltpu.CompilerParams(
            dimension_semantics=("arbitrary",), vmem_limit_bytes=VMEM_LIMIT_BYTES),
        name="mixer_inproj",
    )(x, g, mod, mod, w_in)


def _seq_kernel(p_ref, wb_ref, wc_ref, are_ref, aim_ref, d_ref, wglu_ref, cw_ref, cb_ref,
                lng_ref, lnb_ref, sw_ref, o_ref, bu_ref, st_ref, zbuf_ref, cbuf_ref, qbuf_ref):
    rows = p_ref.shape[0]
    steps = rows // BATCH
    w = BRANCH_WIDTH

    @pl.when(pl.program_id(0) == 0)
    def _():
        st_ref[...] = jnp.zeros_like(st_ref)
        zbuf_ref[0:CONF_HIST, :] = jnp.zeros((CONF_HIST, w), F32)
        qbuf_ref[0:SCONV_HIST, :] = jnp.zeros((SCONV_HIST, w), F32)

    chunk_groups = EW_ROWS // SUBLANES

    def pointwise_chunk(j):
        r0 = j * EW_ROWS
        rs = slice(r0, r0 + EW_ROWS)
        gate_c = p_ref[rs, 4 * w:5 * w].astype(F32)
        hv = p_ref[rs, 5 * w:6 * w].astype(F32)
        qbuf_ref[r0 + SCONV_HIST:r0 + SCONV_HIST + EW_ROWS, :] = gate_c * hv
        q3 = qbuf_ref[r0:r0 + EW_ROWS + SCONV_HIST, :].reshape(
            chunk_groups + SCONV_KERNEL - 1, SUBLANES, w)
        acc = q3[0:chunk_groups] * sw_ref[0:SUBLANES, :]
        for k in range(1, SCONV_KERNEL):
            acc = acc + q3[k:k + chunk_groups] * sw_ref[k * SUBLANES:(k + 1) * SUBLANES, :]
        gate_b = p_ref[rs, 3 * w:4 * w].astype(F32)
        o_ref[rs, 2 * w:3 * w] = (gate_b * acc.reshape(EW_ROWS, w)).astype(o_ref.dtype)
        v = p_ref[rs, w:2 * w].astype(F32)
        g = p_ref[rs, 2 * w:3 * w].astype(F32)
        zbuf_ref[r0 + CONF_HIST:r0 + CONF_HIST + EW_ROWS, :] = v * _sigmoid(g)

    def input_matmul(c):
        u_c = p_ref[:, c * SSM_CHUNK_IN:(c + 1) * SSM_CHUNK_IN]
        bu_ref[c] = _dot(u_c, wb_ref[c])

    def scan(c):
        a_re = are_ref[c]
        a_im = aim_ref[c]
        x_re = st_ref[c, 0]
        x_im = st_ref[c, 1]
        for t in range(steps):
            rs = slice(t * BATCH, (t + 1) * BATCH)
            n_re = a_re * x_re - a_im * x_im + bu_ref[c, rs, 0:SSM_CHUNK_STATE]
            n_im = a_re * x_im + a_im * x_re + bu_ref[c, rs, SSM_CHUNK_STATE:]
            bu_ref[c, rs, 0:SSM_CHUNK_STATE] = n_re
            bu_ref[c, rs, SSM_CHUNK_STATE:] = n_im
            x_re, x_im = n_re, n_im
        st_ref[c, 0] = x_re
        st_ref[c, 1] = x_im

    def output_matmul(c):
        return _dot(bu_ref[c].astype(BF16), wc_ref[c])

    input_matmul(0)
    for j in range(rows // EW_ROWS):
        pointwise_chunk(j)
    qbuf_ref[0:SCONV_HIST, :] = qbuf_ref[rows:rows + SCONV_HIST, :]
    input_matmul(1)
    scan(0)
    y0 = output_matmul(0)
    scan(1)
    ys = [y0, output_matmul(1)]
    u = p_ref[:, 0:w].astype(F32)
    y = _gelu_tanh(jnp.concatenate(ys, axis=1) + d_ref[...] * u)
    y_ssm = y * _sigmoid(_dot(y.astype(BF16), wglu_ref[...]))
    o_ref[:, 0:w] = y_ssm.astype(o_ref.dtype)

    win_groups = (CONV_ROWS + CONF_HIST) // SUBLANES
    out_groups = CONV_ROWS // SUBLANES

    def conv_chunk(j, carry):
        r0 = pl.multiple_of(j * CONV_ROWS, CONV_ROWS)
        for s in range(w // LANES):
            ls = slice(s * LANES, (s + 1) * LANES)
            win = zbuf_ref[pl.ds(r0, CONV_ROWS + CONF_HIST), ls].reshape(win_groups, SUBLANES, LANES)
            acc = win[0:out_groups] * cw_ref[0:SUBLANES, ls]
            for k in range(1, CONF_KERNEL):
                acc = acc + win[k:k + out_groups] * cw_ref[k * SUBLANES:(k + 1) * SUBLANES, ls]
            cbuf_ref[pl.ds(r0, CONV_ROWS), ls] = acc.reshape(CONV_ROWS, LANES)
        return carry

    lax.fori_loop(0, rows // CONV_ROWS, conv_chunk, 0)
    zbuf_ref[0:CONF_HIST, :] = zbuf_ref[rows:rows + CONF_HIST, :]
    cv = cbuf_ref[...] + cb_ref[...]
    cc = cv - jnp.mean(cv, axis=-1, keepdims=True)
    ln = cc * lax.rsqrt(jnp.mean(cc * cc, axis=-1, keepdims=True) + NORM_EPS) * lng_ref[...] + lnb_ref[...]
    o_ref[:, w:2 * w] = (ln * _sigmoid(ln)).astype(o_ref.dtype)


def _seq_mixers(l, proj, *layer_params):
    t = proj.shape[0]
    w = BRANCH_WIDTH
    return pl.pallas_call(
        _seq_kernel,
        out_shape=jax.ShapeDtypeStruct((t, N_BRANCH * w), BF16),
        grid=(t // SEQ_ROWS,),
        in_specs=[pl.BlockSpec((SEQ_ROWS, N_IN_CHUNKS * w), lambda i: (i, 0))]
                 + [_layer_spec(p, l) for p in layer_params],
        out_specs=pl.BlockSpec((SEQ_ROWS, N_BRANCH * w), lambda i: (i, 0)),
        scratch_shapes=[
            pltpu.VMEM((SSM_CHUNKS, SEQ_ROWS, 2 * SSM_CHUNK_STATE), F32),
            pltpu.VMEM((SSM_CHUNKS, 2, BATCH, SSM_CHUNK_STATE), F32),
            pltpu.VMEM((SEQ_ROWS + CONF_HIST, w), F32),
            pltpu.VMEM((SEQ_ROWS, w), F32),
            pltpu.VMEM((SEQ_ROWS + SCONV_HIST, w), F32),
        ],
        compiler_params=pltpu.CompilerParams(
            dimension_semantics=("arbitrary",), vmem_limit_bytes=VMEM_LIMIT_BYTES),
        name="seq_mixers",
    )(proj, *layer_params)


def _merge_router_kernel(x_ref, g_ref, sc_ref, sh_ref, g1_ref, y_ref, wg_ref, bg_ref, wbr_ref, wo_ref,
                         g2_ref, sc2_ref, sh2_ref, whi_ref, wlo_ref, b_ref, tri_ref,
                         o_ref, h_ref, route_ref, route_t_ref, cnt_ref, run_ref):
    x = x_ref[...]
    d = x.shape[1]
    w = BRANCH_WIDTH
    hb = _norm_modulate(x, g_ref[...], sc_ref[...], sh_ref[...]).astype(BF16)
    merged = None
    for n in range(N_BRANCH):
        gate = _sigmoid(_dot(hb, wg_ref[:, n * d:(n + 1) * d]) + bg_ref[:, n * d:(n + 1) * d])
        term = gate * _dot(y_ref[:, n * w:(n + 1) * w], wbr_ref[n])
        merged = term if merged is None else merged + term
    out = _dot(merged.astype(BF16), wo_ref[...])
    x_new = x + _per_batch(out, lambda o3: o3 * g1_ref[...])
    o_ref[...] = x_new
    _route_tile(x_new, g2_ref, sc2_ref, sh2_ref, whi_ref, wlo_ref, b_ref, tri_ref,
                h_ref, route_ref, route_t_ref, cnt_ref, run_ref)


def _merge_router(l, x, mix_g, ffn_g, mod, ycat, w_gate, b_gate, w_branch, w_out, r_hi, r_lo, r_bias, tri):
    t, d = x.shape
    w = BRANCH_WIDTH
    row_tile = lambda cols: pl.BlockSpec((MERGE_ROWS, cols), lambda i: (i, 0))
    return pl.pallas_call(
        _merge_router_kernel,
        out_shape=(
            jax.ShapeDtypeStruct((t, d), F32),
            jax.ShapeDtypeStruct((t * SUBLANES, LANES), F32),
            jax.ShapeDtypeStruct((SUBLANES, t), F32),
            jax.ShapeDtypeStruct((t, SUBLANES), F32),
            jax.ShapeDtypeStruct((N_EXPERTS, MERGE_ROWS), F32),
        ),
        grid=(t // MERGE_ROWS,),
        in_specs=[
            row_tile(d),
            _layer_spec(mix_g, l),
            _layer_spec(mod, l, MOD_SCALE1),
            _layer_spec(mod, l, MOD_SHIFT1),
            _layer_spec(mod, l, MOD_GATE1),
            row_tile(N_BRANCH * w),
            _layer_spec(w_gate, l),
            _layer_spec(b_gate, l),
            _layer_spec(w_branch, l),
            _layer_spec(w_out, l),
            _layer_spec(ffn_g, l),
            _layer_spec(mod, l, MOD_SCALE2),
            _layer_spec(mod, l, MOD_SHIFT2),
            _layer_spec(r_hi, l),
            _layer_spec(r_lo, l),
            _layer_spec(r_bias, l),
            pl.BlockSpec((MERGE_ROWS, MERGE_ROWS), lambda i: (0, 0)),
        ],
        out_specs=(
            row_tile(d),
            pl.BlockSpec((MERGE_ROWS * SUBLANES, LANES), lambda i: (i, 0)),
            pl.BlockSpec((SUBLANES, MERGE_ROWS), lambda i: (0, i)),
            row_tile(SUBLANES),
            pl.BlockSpec((N_EXPERTS, MERGE_ROWS), lambda i: (0, 0)),
        ),
        scratch_shapes=[pltpu.VMEM((N_EXPERTS, MERGE_ROWS), F32)],
        compiler_params=pltpu.CompilerParams(
            dimension_semantics=("arbitrary",), vmem_limit_bytes=VMEM_LIMIT_BYTES),
        name="mixer_merge_router",
    )(x, mix_g, mod, mod, mod, ycat, w_gate, b_gate, w_branch, w_out, ffn_g, mod, mod, r_hi, r_lo, r_bias, tri)


def _route_tile(x, g_ref, sc_ref, sh_ref, whi_ref, wlo_ref, b_ref, tri_ref,
                h_ref, route_ref, route_t_ref, cnt_ref, run_ref):
    @pl.when(pl.program_id(0) == 0)
    def _():
        run_ref[...] = jnp.zeros_like(run_ref)

    h = _norm_modulate(x, g_ref[...], sc_ref[...], sh_ref[...])
    _store_token_tiles(h_ref, h)
    rows = h.shape[0]
    hhi, hlo = _split_bf16(h)
    logits = (_dot(hhi, whi_ref[...]) + _dot(hhi, wlo_ref[...]) + _dot(hlo, whi_ref[...]) + b_ref[...]).T
    lg = logits[0:N_GROUPS]
    le = logits[SUBLANES:SUBLANES + N_EXPERTS]

    g_row = lax.broadcasted_iota(jnp.int32, (N_GROUPS, rows), 0).astype(F32)
    g_max = jnp.max(lg, axis=0, keepdims=True)
    g_sel = jnp.min(jnp.where(lg == g_max, g_row, float(N_GROUPS)), axis=0, keepdims=True)
    g_w = 1.0 / jnp.sum(jnp.exp(lg - g_max), axis=0, keepdims=True)

    e_row = lax.broadcasted_iota(jnp.int32, (N_EXPERTS, rows), 0).astype(F32)
    e_grp = jnp.floor(e_row * (1.0 / EXPERTS_PER_GROUP))
    neg = float("-inf")
    lm = jnp.where(e_grp == g_sel, le, neg)
    v1 = jnp.max(lm, axis=0, keepdims=True)
    i1 = jnp.min(jnp.where(lm == v1, e_row, float(N_EXPERTS)), axis=0, keepdims=True)
    lm2 = jnp.where(e_row == i1, neg, lm)
    v2 = jnp.max(lm2, axis=0, keepdims=True)
    i2 = jnp.min(jnp.where(lm2 == v2, e_row, float(N_EXPERTS)), axis=0, keepdims=True)
    e21 = jnp.exp(v2 - v1)
    p1 = 1.0 / (1.0 + e21)
    w1 = p1 * g_w
    w2 = e21 * p1 * g_w

    hit1 = e_row == i1
    hit2 = e_row == i2
    onehot = jnp.where(hit1 | hit2, 1.0, 0.0)
    cum = _dot(onehot.astype(BF16), tri_ref[...])
    before = cum - onehot + run_ref[...]
    r1 = jnp.sum(jnp.where(hit1, before, 0.0), axis=0, keepdims=True)
    r2 = jnp.sum(jnp.where(hit2, before, 0.0), axis=0, keepdims=True)
    run_ref[...] = run_ref[...] + jnp.broadcast_to(cum[:, rows - 1:rows], run_ref.shape)
    cnt_ref[...] = run_ref[...]
    zero = jnp.zeros_like(w1)
    route = jnp.concatenate([i1, i2, r1, r2, w1, w2, zero, zero], axis=0)
    route_ref[...] = route
    route_t_ref[...] = route.T


def _row_copy(src_hbm, src_token, dst_hbm, dst_token, sem):
    return pltpu.make_async_copy(_token_tile(src_hbm, src_token), _token_tile(dst_hbm, dst_token), sem)


def _dispatch_kernel(zstart_ref, dest_hbm, h_ref, xs_hbm, idx0_smem, idx1_smem, zeros_ref,
                     idx_sem, zero_sem, row_sem):
    i = pl.program_id(0)
    n_tiles = pl.num_programs(0)
    idx_smem = (idx0_smem, idx1_smem)
    n_idx = idx0_smem.shape[0]
    tile = n_idx // TOP_K

    def idx_copy(tile_index, s):
        src = dest_hbm.at[pl.ds(pl.multiple_of(tile_index * n_idx, n_idx), n_idx)]
        return pltpu.make_async_copy(src, idx_smem[s], idx_sem.at[s])

    @pl.when(i == 0)
    def _():
        idx_copy(0, 0).start()

    def zero_copy(j):
        piece_rows = ZERO_ROWS * SUBLANES
        start = pl.multiple_of(zstart_ref[j] * SUBLANES, piece_rows)
        return pltpu.make_async_copy(zeros_ref, xs_hbm.at[pl.ds(start, piece_rows)], zero_sem)

    @pl.when(i == 0)
    def _():
        zeros_ref[...] = jnp.zeros_like(zeros_ref)

        def start(j, carry):
            @pl.when(zstart_ref[j] >= 0)
            def _():
                zero_copy(j).start()
            return carry

        def wait(j, carry):
            @pl.when(zstart_ref[j] >= 0)
            def _():
                zero_copy(j).wait()
            return carry

        lax.fori_loop(0, zstart_ref.shape[0], start, 0)
        lax.fori_loop(0, zstart_ref.shape[0], wait, 0)

    def step(slot):
        @pl.when(i + 1 < n_tiles)
        def _():
            idx_copy(i + 1, 1 - slot).start()

        idx_copy(i, slot).wait()

        def start_rows(t, carry):
            for k in range(TOP_K):
                _row_copy(h_ref, t, xs_hbm, idx_smem[slot][k * tile + t], row_sem).start(priority=k)
            return carry

        lax.fori_loop(0, tile, start_rows, 0, unroll=8)
        for k in range(TOP_K):
            pltpu.make_async_copy(h_ref, xs_hbm.at[pl.ds(0, tile * SUBLANES)], row_sem).wait()

    for slot in range(2):
        pl.when(lax.rem(i, 2) == slot)(functools.partial(step, slot))


def _dispatch(zstart, dest_tiles, h, cap):
    t = h.shape[0] // SUBLANES
    any_spec = pl.BlockSpec(memory_space=pl.ANY)
    return pl.pallas_call(
        _dispatch_kernel,
        out_shape=jax.ShapeDtypeStruct((cap * SUBLANES, LANES), h.dtype),
        grid_spec=pltpu.PrefetchScalarGridSpec(
            num_scalar_prefetch=1,
            grid=(t // DISPATCH_ROWS,),
            in_specs=[any_spec,
                      pl.BlockSpec((DISPATCH_ROWS * SUBLANES, LANES), lambda i, zs: (i, 0))],
            out_specs=any_spec,
            scratch_shapes=[
                pltpu.SMEM((TOP_K * DISPATCH_ROWS,), jnp.int32),
                pltpu.SMEM((TOP_K * DISPATCH_ROWS,), jnp.int32),
                pltpu.VMEM((ZERO_ROWS * SUBLANES, LANES), h.dtype),
                pltpu.SemaphoreType.DMA((2,)),
                pltpu.SemaphoreType.DMA,
                pltpu.SemaphoreType.DMA,
            ],
        ),
        compiler_params=pltpu.CompilerParams(
            dimension_semantics=("arbitrary",), has_side_effects=True),
        name="moe_dispatch",
    )(zstart, dest_tiles, h)


def _expert_kernel(first_ref, nblk_ref, total_ref, xs_hbm, w1_ref, w3_ref, w2_ref, ys_hbm,
                   w1b_ref, w3b_ref, w2b_ref, xbuf_ref, ybuf_ref, in_sem, out_sem):
    e = pl.program_id(0)
    n_blocks = nblk_ref[e]
    first = first_ref[e]
    total = total_ref[0]
    block_rows = EXPERT_ROWS * SUBLANES

    def rows_of(block):
        return pl.ds(pl.multiple_of(block * block_rows, block_rows), block_rows)

    def in_copy(block, slot):
        return pltpu.make_async_copy(xs_hbm.at[rows_of(block)], xbuf_ref.at[slot], in_sem.at[slot])

    def out_copy(block, slot):
        return pltpu.make_async_copy(ybuf_ref.at[slot], ys_hbm.at[rows_of(block)], out_sem.at[slot])

    @pl.when(n_blocks > 0)
    def _():
        @pl.when(first == 0)
        def _():
            in_copy(0, 0).start()

        w1b_ref[...] = w1_ref[...].astype(BF16)
        w3b_ref[...] = w3_ref[...].astype(BF16)
        w2b_ref[...] = w2_ref[...].astype(BF16)

        def block_step(j, carry):
            g = first + j
            slot = lax.rem(g, 2)

            @pl.when(g + 1 < total)
            def _():
                in_copy(g + 1, 1 - slot).start(priority=1)

            in_copy(g, slot).wait()

            @pl.when(g >= 2)
            def _():
                out_copy(g - 2, slot).wait()

            xb = _load_token_tiles(xbuf_ref.at[slot], 0, EXPERT_ROWS).astype(BF16)
            a = _dot(xb, w1b_ref[...])
            hid = a * _sigmoid(a) * _dot(xb, w3b_ref[...])
            _store_token_tiles(ybuf_ref.at[slot], _dot(hid.astype(BF16), w2b_ref[...]))
            out_copy(g, slot).start()
            return carry

        lax.fori_loop(0, n_blocks, block_step, 0)

        @pl.when(first + n_blocks == total)
        def _():
            @pl.when(total >= 2)
            def _():
                out_copy(total - 2, lax.rem(total, 2)).wait()

            out_copy(total - 1, lax.rem(total - 1, 2)).wait()


def _experts(l, first_block, n_blocks, total_blocks, xs, w1, w3, w2):
    _, n_experts, d, f = w1.shape
    tile = (EXPERT_ROWS * SUBLANES, LANES)
    any_spec = pl.BlockSpec(memory_space=pl.ANY)
    weight_spec = lambda rows, cols: pl.BlockSpec((None, None, rows, cols), lambda e, *_: (l, e, 0, 0))
    return pl.pallas_call(
        _expert_kernel,
        out_shape=jax.ShapeDtypeStruct(xs.shape, F32),
        grid_spec=pltpu.PrefetchScalarGridSpec(
            num_scalar_prefetch=3,
            grid=(n_experts,),
            in_specs=[any_spec, weight_spec(d, f), weight_spec(d, f), weight_spec(f, d)],
            out_specs=any_spec,
            scratch_shapes=[
                pltpu.VMEM((d, f), BF16), pltpu.VMEM((d, f), BF16), pltpu.VMEM((f, d), BF16),
                pltpu.VMEM((2,) + tile, F32), pltpu.VMEM((2,) + tile, F32),
                pltpu.SemaphoreType.DMA((2,)), pltpu.SemaphoreType.DMA((2,)),
            ],
        ),
        compiler_params=pltpu.CompilerParams(
            dimension_semantics=("arbitrary",), vmem_limit_bytes=VMEM_LIMIT_BYTES),
        name="moe_experts",
    )(first_block, n_blocks, total_blocks, xs, w1, w3, w2)


def _gather_expert_rows(dest_hbm, ys_hbm, idx_smem, ybuf_ref, idx_sem, row_sem):
    i = pl.program_id(0)
    n_tiles = pl.num_programs(0)
    n_idx = idx_smem[0].shape[0]
    tile = n_idx // TOP_K

    def idx_copy(tile_index, s):
        src = dest_hbm.at[pl.ds(pl.multiple_of(tile_index * n_idx, n_idx), n_idx)]
        return pltpu.make_async_copy(src, idx_smem[s], idx_sem.at[s])

    def start_rows(s):
        def body(t, carry):
            for k in range(TOP_K):
                row = k * tile + t
                pltpu.make_async_copy(_token_tile(ys_hbm, idx_smem[s][row]),
                                      _token_tile(ybuf_ref.at[s], row), row_sem.at[s]).start(priority=k)
            return carry
        lax.fori_loop(0, tile, body, 0, unroll=8)

    def wait_rows(s):
        pltpu.make_async_copy(ybuf_ref.at[s], ybuf_ref.at[s], row_sem.at[s]).wait()

    @pl.when(i == 0)
    def _():
        idx_copy(0, 0).start()
        idx_copy(0, 0).wait()
        start_rows(0)

        @pl.when(n_tiles > 1)
        def _():
            idx_copy(1, 1).start()

    def step(slot):
        other = 1 - slot

        @pl.when(i + 1 < n_tiles)
        def _():
            idx_copy(i + 1, other).wait()
            start_rows(other)

        @pl.when(i + 2 < n_tiles)
        def _():
            idx_copy(i + 2, slot).start()

        wait_rows(slot)

    for slot in range(2):
        pl.when(lax.rem(i, 2) == slot)(functools.partial(step, slot))
    return lax.rem(i, 2)


def _combined_residual(slot, ybuf_ref, x_ref, g2_ref, wt_ref):
    tile = x_ref.shape[0]
    wt = wt_ref[...]
    ytiles = ybuf_ref.at[slot]
    y = (wt[:, 2 * TOP_K:2 * TOP_K + 1] * _load_token_tiles(ytiles, 0, tile)
         + wt[:, 2 * TOP_K + 1:2 * TOP_K + 2] * _load_token_tiles(ytiles, tile, tile))
    return x_ref[...] + _per_batch(y, lambda y3: y3 * g2_ref[...])


def _combine_final_kernel(dest_hbm, ys_hbm, x_ref, g2_ref, wt_ref, fg_ref, o_ref,
                          idx0_smem, idx1_smem, ybuf_ref, idx_sem, row_sem):
    slot = _gather_expert_rows(dest_hbm, ys_hbm, (idx0_smem, idx1_smem), ybuf_ref, idx_sem, row_sem)
    out = _combined_residual(slot, ybuf_ref, x_ref, g2_ref, wt_ref)
    ms = jnp.mean(out * out, axis=-1, keepdims=True)
    out = out * lax.rsqrt(ms + NORM_EPS) * fg_ref[...]
    bsz, steps, d = o_ref.shape
    o_ref[...] = pltpu.einshape("sbd->bsd", out.reshape(steps, bsz, d))


def _combine_inproj_kernel(dest_hbm, ys_hbm, x_ref, g2_ref, wt_ref, g_ref, sc_ref, sh_ref, w_ref,
                           xo_ref, p_ref, idx0_smem, idx1_smem, ybuf_ref, idx_sem, row_sem):
    slot = _gather_expert_rows(dest_hbm, ys_hbm, (idx0_smem, idx1_smem), ybuf_ref, idx_sem, row_sem)
    out = _combined_residual(slot, ybuf_ref, x_ref, g2_ref, wt_ref)
    xo_ref[...] = out
    h = _norm_modulate(out, g_ref[...], sc_ref[...], sh_ref[...])
    p_ref[...] = _dot(h.astype(BF16), w_ref[...]).astype(p_ref.dtype)


def _combine(l, dest_tiles, ys, x, mod, wt, final_g=None, next_layer=None):
    t, d = x.shape
    any_spec = pl.BlockSpec(memory_space=pl.ANY)
    row_tile = lambda cols: pl.BlockSpec((COMBINE_ROWS, cols), lambda i: (i, 0))
    in_specs = [any_spec, any_spec, row_tile(d), _layer_spec(mod, l, MOD_GATE2), row_tile(SUBLANES)]
    if next_layer is None:
        body, operands = _combine_final_kernel, (final_g,)
        in_specs += [pl.BlockSpec((1, d), lambda i: (0, 0))]
        out_shape = jax.ShapeDtypeStruct((BATCH, t // BATCH, d), F32)
        out_specs = pl.BlockSpec((BATCH, COMBINE_ROWS // BATCH, d), lambda i: (0, i, 0))
    else:
        mix_g, w_in = next_layer
        n = w_in.shape[2]
        body, operands = _combine_inproj_kernel, (mix_g, mod, mod, w_in)
        in_specs += [_layer_spec(mix_g, l + 1), _layer_spec(mod, l + 1, MOD_SCALE1),
                     _layer_spec(mod, l + 1, MOD_SHIFT1), _layer_spec(w_in, l + 1)]
        out_shape = (jax.ShapeDtypeStruct((t, d), F32), jax.ShapeDtypeStruct((t, n), BF16))
        out_specs = (row_tile(d), row_tile(n))
    return pl.pallas_call(
        body,
        out_shape=out_shape,
        grid=(t // COMBINE_ROWS,),
        in_specs=in_specs,
        out_specs=out_specs,
        scratch_shapes=[
            pltpu.SMEM((TOP_K * COMBINE_ROWS,), jnp.int32),
            pltpu.SMEM((TOP_K * COMBINE_ROWS,), jnp.int32),
            pltpu.VMEM((2, TOP_K * COMBINE_ROWS * SUBLANES, LANES), F32),
            pltpu.SemaphoreType.DMA((2,)),
            pltpu.SemaphoreType.DMA((2,)),
        ],
        compiler_params=pltpu.CompilerParams(
            dimension_semantics=("arbitrary",), vmem_limit_bytes=VMEM_LIMIT_BYTES),
        name="moe_combine",
    )(dest_tiles, ys, x, mod, wt, *operands)


def _ssm_params(lam_re, lam_im, log_step, b_re, b_im, c_re, c_im):
    step = jnp.exp(log_step)[:, None]
    mag = jnp.exp(lam_re * step)
    a_re, a_im = mag * jnp.cos(lam_im * step), mag * jnp.sin(lam_im * step)
    den = lam_re * lam_re + lam_im * lam_im
    n_re, n_im = a_re - 1.0, a_im
    k_re = (n_re * lam_re + n_im * lam_im) / den
    k_im = (n_im * lam_re - n_re * lam_im) / den
    bb_re = k_re[..., None] * b_re - k_im[..., None] * b_im
    bb_im = k_re[..., None] * b_im + k_im[..., None] * b_re
    gpc = SSM_GROUPS // SSM_CHUNKS
    eye = jnp.eye(gpc, dtype=F32)

    def in_blocks(bb):
        bb = bb.reshape(SSM_CHUNKS, gpc, SSM_STATE, SSM_GROUP)
        return jnp.einsum("cgph,gk->cghkp", bb, eye).reshape(SSM_CHUNKS, SSM_CHUNK_IN, SSM_CHUNK_STATE)

    def out_blocks(cc):
        cc = cc.reshape(SSM_CHUNKS, gpc, SSM_GROUP, SSM_STATE)
        return jnp.einsum("cghp,gk->cgpkh", cc, eye).reshape(SSM_CHUNKS, SSM_CHUNK_STATE, SSM_CHUNK_IN)

    wb = jnp.concatenate([in_blocks(bb_re), in_blocks(bb_im)], axis=2).astype(BF16)
    wc = jnp.concatenate([out_blocks(c_re), out_blocks(-c_im)], axis=1).astype(BF16)
    bcast = lambda a: jnp.broadcast_to(a.reshape(SSM_CHUNKS, 1, SSM_CHUNK_STATE),
                                       (SSM_CHUNKS, BATCH, SSM_CHUNK_STATE))
    return wb, wc, bcast(a_re), bcast(a_im)


def _rows8(w):
    depth, k, c = w.shape
    return jnp.broadcast_to(w[:, :, None, :], (depth, k, SUBLANES, c)).reshape(depth, k * SUBLANES, c)


def _router_weights(w_rg, b_rg, w_re, b_re):
    depth, d, _ = w_rg.shape
    pad = SUBLANES - N_GROUPS
    tail = LANES - ROUTER_ROWS
    w_cols = jnp.concatenate([w_rg, jnp.zeros((depth, d, pad), F32), w_re,
                              jnp.zeros((depth, d, tail), F32)], axis=2)
    b = jnp.concatenate([b_rg, jnp.zeros((depth, pad), F32), b_re, jnp.zeros((depth, tail), F32)], axis=1)
    hi, lo = _split_bf16(w_cols)
    return hi, lo, b.reshape(depth, 1, LANES)


def _routing_tables(route, counts):
    t = route.shape[1]
    counts = counts.astype(jnp.int32)
    padded = (counts + EXPERT_ROWS - 1) // EXPERT_ROWS * EXPERT_ROWS
    pad_end = jnp.cumsum(padded)
    pad_start = pad_end - padded
    experts = route[0:TOP_K].astype(jnp.int32)
    ranks = route[TOP_K:2 * TOP_K].astype(jnp.int32)
    expert_ids = jnp.arange(N_EXPERTS, dtype=jnp.int32)
    is_expert = experts[:, :, None] == expert_ids
    dest = jnp.sum(jnp.where(is_expert, pad_start, 0), axis=-1) + ranks

    def tiles(rows_per_tile):
        n = t // rows_per_tile
        return dest.reshape(TOP_K, n, rows_per_tile).transpose(1, 0, 2).reshape(-1)

    capacity = TOP_K * t + N_EXPERTS * EXPERT_ROWS
    first_block = pad_start // EXPERT_ROWS
    n_blocks = padded // EXPERT_ROWS
    last_start = pad_end - EXPERT_ROWS
    last_valid = counts - (padded - EXPERT_ROWS)
    piece = jnp.arange(EXPERT_ROWS // ZERO_ROWS, dtype=jnp.int32)
    has_pad = (counts[:, None] > 0) & ((piece[None, :] + 1) * ZERO_ROWS > last_valid[:, None])
    zstart = jnp.where(has_pad, last_start[:, None] + piece[None, :] * ZERO_ROWS, -1).reshape(-1)
    total_blocks = (pad_end[-1] // EXPERT_ROWS).reshape(1)
    return (tiles(DISPATCH_ROWS), tiles(COMBINE_ROWS), first_block.astype(jnp.int32), n_blocks.astype(jnp.int32),
            total_blocks.astype(jnp.int32), zstart.astype(jnp.int32), capacity)


def kernel(x, c, norm_mix_g, norm_ffn_g, w_ada, b_ada, w_in, lam_re, lam_im, log_step, ssm_b_re, ssm_b_im, ssm_c_re, ssm_c_im, ssm_d, w_glu, conf_dw_w, conf_dw_b, conf_ln_g, conf_ln_b, sconv_w, w_branch, w_gate, b_gate, w_out, w_router_group, b_router_group, w_router_expert, b_router_expert, w_exp_gate, w_exp_up, w_exp_down, final_norm_g):
    bsz, seq, d = x.shape
    assert bsz == BATCH and d == D_MODEL
    depth = w_in.shape[0]
    t = bsz * seq
    rows = lambda v: v.reshape(depth, 1, -1)

    mix_g, ffn_g = rows(norm_mix_g), rows(norm_ffn_g)
    seq_params = (*jax.vmap(_ssm_params)(lam_re, lam_im, log_step, ssm_b_re, ssm_b_im, ssm_c_re, ssm_c_im),
                  rows(ssm_d), w_glu.astype(BF16), _rows8(conf_dw_w), rows(conf_dw_b), rows(conf_ln_g),
                  rows(conf_ln_b), _rows8(sconv_w))
    w_in_b, w_gate_b, w_branch_b, w_out_b = (w.astype(BF16) for w in (w_in, w_gate, w_branch, w_out))
    r_hi, r_lo, r_b = _router_weights(w_router_group, b_router_group, w_router_expert, b_router_expert)
    tri = jnp.triu(jnp.ones((ROUTE_ROWS, ROUTE_ROWS), BF16))

    mod = _ada(c, w_ada, b_ada)
    xt, proj = _inproj_first(x, mix_g, mod, w_in_b)
    for l in range(depth):
        ycat = _seq_mixers(l, proj, *seq_params)
        xt, h2, route, route_t, counts = _merge_router(l, xt, mix_g, ffn_g, mod, ycat, w_gate_b, rows(b_gate),
                                                       w_branch_b, w_out_b, r_hi, r_lo, r_b, tri)
        (dest_d, dest_c, first_block, n_blocks, total_blocks, zstart,
         cap) = _routing_tables(route, counts[:, 0])
        xs = _dispatch(zstart, dest_d, h2, cap)
        ys = _experts(l, first_block, n_blocks, total_blocks, xs, w_exp_gate, w_exp_up, w_exp_down)
        if l + 1 < depth:
            xt, proj = _combine(l, dest_c, ys, xt, mod, route_t, next_layer=(mix_g, w_in_b))
        else:
            return _combine(l, dest_c, ys, xt, mod, route_t, final_g=final_norm_g.reshape(1, d))
```

```python
import functools
import math

import jax
import jax.numpy as jnp
from jax import lax
from jax.experimental import pallas as pl
from jax.experimental.pallas import tpu as pltpu

F32 = jnp.float32
BF16 = jnp.bfloat16

D_MODEL = 1024
BATCH = 8
BRANCH_WIDTH = D_MODEL // 2
N_BRANCH = 3
N_IN_CHUNKS = 6
SSM_GROUP = 16
SSM_GROUPS = BRANCH_WIDTH // SSM_GROUP
SSM_STATE = 64
CONF_KERNEL = 31
SCONV_KERNEL = 3
N_GROUPS = 4
EXPERTS_PER_GROUP = 8
N_EXPERTS = N_GROUPS * EXPERTS_PER_GROUP
TOP_K = 2
D_FF_EXPERT = D_MODEL // 2
NORM_EPS = 1e-6

SUBLANES = 8
LANES = 128
VMEM_LIMIT_BYTES = 56 * 1024 * 1024

SSM_CHUNKS = 2
SSM_CHUNK_IN = BRANCH_WIDTH // SSM_CHUNKS
SSM_CHUNK_STATE = SSM_GROUPS * SSM_STATE // SSM_CHUNKS
CONF_HIST = (CONF_KERNEL - 1) * BATCH
SCONV_HIST = (SCONV_KERNEL - 1) * BATCH
ROUTER_ROWS = 40
MOD_SHIFT1, MOD_SCALE1, MOD_GATE1, MOD_SHIFT2, MOD_SCALE2, MOD_GATE2 = range(6)

ADA_COLS = 1536
INPROJ_ROWS = 1024
SEQ_ROWS = 512
CONV_ROWS = 64
EW_ROWS = 32
MERGE_ROWS = 512
ROUTE_ROWS = MERGE_ROWS
DISPATCH_ROWS = 512
EXPERT_ROWS = 512
IN_LOOKAHEAD = 2
ZERO_ROWS = 64
COMBINE_ROWS = 512


def _dot(a, b):
    return jnp.dot(a, b, preferred_element_type=F32)


def _split_bf16(v):
    hi = v.astype(BF16)
    lo = (v - hi.astype(F32)).astype(BF16)
    return hi, lo


def _sigmoid(v):
    return 1.0 / (1.0 + jnp.exp(-v))


def _gelu_tanh(v):
    return 0.5 * v * (1.0 + jnp.tanh(math.sqrt(2.0 / math.pi) * (v + 0.044715 * (v * v * v))))


def _per_batch(v, fn):
    rows, d = v.shape
    return fn(v.reshape(rows // BATCH, BATCH, d)).reshape(rows, d)


def _load_token_tiles(ref, first_token, n_tokens):
    base = first_token * SUBLANES
    chunks = [ref[pl.ds(base + s, n_tokens, stride=SUBLANES), :] for s in range(SUBLANES)]
    return jnp.concatenate(chunks, axis=1)


def _store_token_tiles(ref, value):
    for s in range(SUBLANES):
        ref[pl.ds(s, value.shape[0], stride=SUBLANES), :] = value[:, s * LANES:(s + 1) * LANES]


def _token_tile(ref, token):
    return ref.at[pl.ds(pl.multiple_of(token * SUBLANES, SUBLANES), SUBLANES)]


def _layer_spec(arr, l, col_block=None):
    tail = arr.shape[1:]
    if col_block is None:
        zeros = (0,) * len(tail)
        return pl.BlockSpec((None,) + tail, lambda *_: (l,) + zeros)
    return pl.BlockSpec((None, tail[0], D_MODEL), lambda *_: (l, 0, col_block))


def _norm_modulate(x, g, scale, shift):
    ms = jnp.mean(x * x, axis=-1, keepdims=True)
    h = x * lax.rsqrt(ms + NORM_EPS) * g
    return _per_batch(h, lambda h3: h3 * (1.0 + scale) + shift)


def _ada_kernel(c_ref, w_ref, b_ref, o_ref):
    c = c_ref[...]
    cond = c * _sigmoid(c)
    chi, clo = _split_bf16(cond)
    whi, wlo = _split_bf16(w_ref[...])
    o_ref[...] = _dot(chi, whi) + _dot(chi, wlo) + _dot(clo, whi) + b_ref[...]


def _ada(c, w_ada, b_ada):
    depth, d, n = w_ada.shape
    return pl.pallas_call(
        _ada_kernel,
        out_shape=jax.ShapeDtypeStruct((depth, BATCH, n), F32),
        grid=(depth, n // ADA_COLS),
        in_specs=[
            pl.BlockSpec((BATCH, d), lambda l, j: (0, 0)),
            pl.BlockSpec((None, d, ADA_COLS), lambda l, j: (l, 0, j)),
            pl.BlockSpec((None, 1, ADA_COLS), lambda l, j: (l, 0, j)),
        ],
        out_specs=pl.BlockSpec((None, BATCH, ADA_COLS), lambda l, j: (l, 0, j)),
        compiler_params=pltpu.CompilerParams(
            dimension_semantics=("arbitrary", "arbitrary"), vmem_limit_bytes=VMEM_LIMIT_BYTES),
        name="ada_modulation",
    )(c, w_ada, b_ada.reshape(depth, 1, n))


def _inproj_kernel(x_ref, g_ref, sc_ref, sh_ref, w_ref, xt_ref, o_ref):
    bsz, steps, d = x_ref.shape
    x = pltpu.einshape("bsd->sbd", x_ref[...]).reshape(steps * bsz, d)
    xt_ref[...] = x
    h = _norm_modulate(x, g_ref[...], sc_ref[...], sh_ref[...])
    o_ref[...] = _dot(h.astype(BF16), w_ref[...]).astype(o_ref.dtype)


def _inproj_first(x, g, mod, w_in):
    bsz, seq, d = x.shape
    n = w_in.shape[2]
    steps = INPROJ_ROWS // bsz
    return pl.pallas_call(
        _inproj_kernel,
        out_shape=(jax.ShapeDtypeStruct((bsz * seq, d), F32), jax.ShapeDtypeStruct((bsz * seq, n), BF16)),
        grid=(seq // steps,),
        in_specs=[
            pl.BlockSpec((bsz, steps, d), lambda i: (0, i, 0)),
            _layer_spec(g, 0),
            _layer_spec(mod, 0, MOD_SCALE1),
            _layer_spec(mod, 0, MOD_SHIFT1),
            _layer_spec(w_in, 0),
        ],
        out_specs=(pl.BlockSpec((INPROJ_ROWS, d), lambda i: (i, 0)),
                   pl.BlockSpec((INPROJ_ROWS, n), lambda i: (i, 0))),
        compiler_params=pltpu.CompilerParams(
            dimension_semantics=("arbitrary",), vmem_limit_bytes=VMEM_LIMIT_BYTES),
        name="mixer_inproj",
    )(x, g, mod, mod, w_in)


def _seq_kernel(p_ref, wb_ref, wc_ref, are_ref, aim_ref, d_ref, wglu_ref, cw_ref, cb_ref,
                lng_ref, lnb_ref, sw_ref, o_ref, bu_ref, st_ref, zbuf_ref, cbuf_ref, qbuf_ref):
    rows = p_ref.shape[0]
    steps = rows // BATCH
    w = BRANCH_WIDTH

    @pl.when(pl.program_id(0) == 0)
    def _():
        st_ref[...] = jnp.zeros_like(st_ref)
        zbuf_ref[0:CONF_HIST, :] = jnp.zeros((CONF_HIST, w), F32)
        qbuf_ref[0:SCONV_HIST, :] = jnp.zeros((SCONV_HIST, w), F32)

    chunk_groups = EW_ROWS // SUBLANES

    def pointwise_chunk(j):
        r0 = j * EW_ROWS
        rs = slice(r0, r0 + EW_ROWS)
        gate_c = p_ref[rs, 4 * w:5 * w].astype(F32)
        hv = p_ref[rs, 5 * w:6 * w].astype(F32)
        qbuf_ref[r0 + SCONV_HIST:r0 + SCONV_HIST + EW_ROWS, :] = gate_c * hv
        q3 = qbuf_ref[r0:r0 + EW_ROWS + SCONV_HIST, :].reshape(
            chunk_groups + SCONV_KERNEL - 1, SUBLANES, w)
        acc = q3[0:chunk_groups] * sw_ref[0:SUBLANES, :]
        for k in range(1, SCONV_KERNEL):
            acc = acc + q3[k:k + chunk_groups] * sw_ref[k * SUBLANES:(k + 1) * SUBLANES, :]
        gate_b = p_ref[rs, 3 * w:4 * w].astype(F32)
        o_ref[rs, 2 * w:3 * w] = (gate_b * acc.reshape(EW_ROWS, w)).astype(o_ref.dtype)
        v = p_ref[rs, w:2 * w].astype(F32)
        g = p_ref[rs, 2 * w:3 * w].astype(F32)
        zbuf_ref[r0 + CONF_HIST:r0 + CONF_HIST + EW_ROWS, :] = v * _sigmoid(g)

    def input_matmul(c):
        u_c = p_ref[:, c * SSM_CHUNK_IN:(c + 1) * SSM_CHUNK_IN]
        bu_ref[c] = _dot(u_c, wb_ref[c])

    def scan(c):
        a_re = are_ref[c]
        a_im = aim_ref[c]
        x_re = st_ref[c, 0]
        x_im = st_ref[c, 1]
        for t in range(steps):
            rs = slice(t * BATCH, (t + 1) * BATCH)
            n_re = a_re * x_re - a_im * x_im + bu_ref[c, rs, 0:SSM_CHUNK_STATE]
            n_im = a_re * x_im + a_im * x_re + bu_ref[c, rs, SSM_CHUNK_STATE:]
            bu_ref[c, rs, 0:SSM_CHUNK_STATE] = n_re
            bu_ref[c, rs, SSM_CHUNK_STATE:] = n_im
            x_re, x_im = n_re, n_im
        st_ref[c, 0] = x_re
        st_ref[c, 1] = x_im

    def output_matmul(c):
        return _dot(bu_ref[c].astype(BF16), wc_ref[c])

    input_matmul(0)
    for j in range(rows // EW_ROWS):
        pointwise_chunk(j)
    qbuf_ref[0:SCONV_HIST, :] = qbuf_ref[rows:rows + SCONV_HIST, :]
    input_matmul(1)
    scan(0)
    y0 = output_matmul(0)
    scan(1)
    ys = [y0, output_matmul(1)]
    u = p_ref[:, 0:w].astype(F32)
    y = _gelu_tanh(jnp.concatenate(ys, axis=1) + d_ref[...] * u)
    y_ssm = y * _sigmoid(_dot(y.astype(BF16), wglu_ref[...]))
    o_ref[:, 0:w] = y_ssm.astype(o_ref.dtype)

    win_groups = (CONV_ROWS + CONF_HIST) // SUBLANES
    out_groups = CONV_ROWS // SUBLANES

    def conv_chunk(j, carry):
        r0 = pl.multiple_of(j * CONV_ROWS, CONV_ROWS)
        for s in range(w // LANES):
            ls = slice(s * LANES, (s + 1) * LANES)
            win = zbuf_ref[pl.ds(r0, CONV_ROWS + CONF_HIST), ls].reshape(win_groups, SUBLANES, LANES)
            acc = win[0:out_groups] * cw_ref[0:SUBLANES, ls]
            for k in range(1, CONF_KERNEL):
                acc = acc + win[k:k + out_groups] * cw_ref[k * SUBLANES:(k + 1) * SUBLANES, ls]
            cbuf_ref[pl.ds(r0, CONV_ROWS), ls] = acc.reshape(CONV_ROWS, LANES)
        return carry

    lax.fori_loop(0, rows // CONV_ROWS, conv_chunk, 0)
    zbuf_ref[0:CONF_HIST, :] = zbuf_ref[rows:rows + CONF_HIST, :]
    cv = cbuf_ref[...] + cb_ref[...]
    cc = cv - jnp.mean(cv, axis=-1, keepdims=True)
    ln = cc * lax.rsqrt(jnp.mean(cc * cc, axis=-1, keepdims=True) + NORM_EPS) * lng_ref[...] + lnb_ref[...]
    o_ref[:, w:2 * w] = (ln * _sigmoid(ln)).astype(o_ref.dtype)


def _seq_mixers(l, proj, *layer_params):
    t = proj.shape[0]
    w = BRANCH_WIDTH
    return pl.pallas_call(
        _seq_kernel,
        out_shape=jax.ShapeDtypeStruct((t, N_BRANCH * w), BF16),
        grid=(t // SEQ_ROWS,),
        in_specs=[pl.BlockSpec((SEQ_ROWS, N_IN_CHUNKS * w), lambda i: (i, 0))]
                 + [_layer_spec(p, l) for p in layer_params],
        out_specs=pl.BlockSpec((SEQ_ROWS, N_BRANCH * w), lambda i: (i, 0)),
        scratch_shapes=[
            pltpu.VMEM((SSM_CHUNKS, SEQ_ROWS, 2 * SSM_CHUNK_STATE), F32),
            pltpu.VMEM((SSM_CHUNKS, 2, BATCH, SSM_CHUNK_STATE), F32),
            pltpu.VMEM((SEQ_ROWS + CONF_HIST, w), F32),
            pltpu.VMEM((SEQ_ROWS, w), F32),
            pltpu.VMEM((SEQ_ROWS + SCONV_HIST, w), F32),
        ],
        compiler_params=pltpu.CompilerParams(
            dimension_semantics=("arbitrary",), vmem_limit_bytes=VMEM_LIMIT_BYTES),
        name="seq_mixers",
    )(proj, *layer_params)


def _merge_router_kernel(x_ref, g_ref, sc_ref, sh_ref, g1_ref, y_ref, wg_ref, bg_ref, wbr_ref, wo_ref,
                         g2_ref, sc2_ref, sh2_ref, whi_ref, wlo_ref, b_ref, tri_ref,
                         o_ref, h_ref, route_ref, route_t_ref, cnt_ref, run_ref):
    x = x_ref[...]
    d = x.shape[1]
    w = BRANCH_WIDTH
    hb = _norm_modulate(x, g_ref[...], sc_ref[...], sh_ref[...]).astype(BF16)
    merged = None
    for n in range(N_BRANCH):
        gate = _sigmoid(_dot(hb, wg_ref[:, n * d:(n + 1) * d]) + bg_ref[:, n * d:(n + 1) * d])
        term = gate * _dot(y_ref[:, n * w:(n + 1) * w], wbr_ref[n])
        merged = term if merged is None else merged + term
    out = _dot(merged.astype(BF16), wo_ref[...])
    x_new = x + _per_batch(out, lambda o3: o3 * g1_ref[...])
    o_ref[...] = x_new
    _route_tile(x_new, g2_ref, sc2_ref, sh2_ref, whi_ref, wlo_ref, b_ref, tri_ref,
                h_ref, route_ref, route_t_ref, cnt_ref, run_ref)


def _merge_router(l, x, mix_g, ffn_g, mod, ycat, w_gate, b_gate, w_branch, w_out, r_hi, r_lo, r_bias, tri):
    t, d = x.shape
    w = BRANCH_WIDTH
    row_tile = lambda cols: pl.BlockSpec((MERGE_ROWS, cols), lambda i: (i, 0))
    return pl.pallas_call(
        _merge_router_kernel,
        out_shape=(
            jax.ShapeDtypeStruct((t, d), F32),
            jax.ShapeDtypeStruct((t * SUBLANES, LANES), F32),
            jax.ShapeDtypeStruct((SUBLANES, t), F32),
            jax.ShapeDtypeStruct((t, SUBLANES), F32),
            jax.ShapeDtypeStruct((N_EXPERTS, MERGE_ROWS), F32),
        ),
        grid=(t // MERGE_ROWS,),
        in_specs=[
            row_tile(d),
            _layer_spec(mix_g, l),
            _layer_spec(mod, l, MOD_SCALE1),
            _layer_spec(mod, l, MOD_SHIFT1),
            _layer_spec(mod, l, MOD_GATE1),
            row_tile(N_BRANCH * w),
            _layer_spec(w_gate, l),
            _layer_spec(b_gate, l),
            _layer_spec(w_branch, l),
            _layer_spec(w_out, l),
            _layer_spec(ffn_g, l),
            _layer_spec(mod, l, MOD_SCALE2),
            _layer_spec(mod, l, MOD_SHIFT2),
            _layer_spec(r_hi, l),
            _layer_spec(r_lo, l),
            _layer_spec(r_bias, l),
            pl.BlockSpec((MERGE_ROWS, MERGE_ROWS), lambda i: (0, 0)),
        ],
        out_specs=(
            row_tile(d),
            pl.BlockSpec((MERGE_ROWS * SUBLANES, LANES), lambda i: (i, 0)),
            pl.BlockSpec((SUBLANES, MERGE_ROWS), lambda i: (0, i)),
            row_tile(SUBLANES),
            pl.BlockSpec((N_EXPERTS, MERGE_ROWS), lambda i: (0, 0)),
        ),
        scratch_shapes=[pltpu.VMEM((N_EXPERTS, MERGE_ROWS), F32)],
        compiler_params=pltpu.CompilerParams(
            dimension_semantics=("arbitrary",), vmem_limit_bytes=VMEM_LIMIT_BYTES),
        name="mixer_merge_router",
    )(x, mix_g, mod, mod, mod, ycat, w_gate, b_gate, w_branch, w_out, ffn_g, mod, mod, r_hi, r_lo, r_bias, tri)


def _route_tile(x, g_ref, sc_ref, sh_ref, whi_ref, wlo_ref, b_ref, tri_ref,
                h_ref, route_ref, route_t_ref, cnt_ref, run_ref):
    @pl.when(pl.program_id(0) == 0)
    def _():
        run_ref[...] = jnp.zeros_like(run_ref)

    h = _norm_modulate(x, g_ref[...], sc_ref[...], sh_ref[...])
    _store_token_tiles(h_ref, h)
    rows = h.shape[0]
    hhi, hlo = _split_bf16(h)
    logits = (_dot(hhi, whi_ref[...]) + _dot(hhi, wlo_ref[...]) + _dot(hlo, whi_ref[...]) + b_ref[...]).T
    lg = logits[0:N_GROUPS]
    le = logits[SUBLANES:SUBLANES + N_EXPERTS]

    g_row = lax.broadcasted_iota(jnp.int32, (N_GROUPS, rows), 0).astype(F32)
    g_max = jnp.max(lg, axis=0, keepdims=True)
    g_sel = jnp.min(jnp.where(lg == g_max, g_row, float(N_GROUPS)), axis=0, keepdims=True)
    g_w = 1.0 / jnp.sum(jnp.exp(lg - g_max), axis=0, keepdims=True)

    e_row = lax.broadcasted_iota(jnp.int32, (N_EXPERTS, rows), 0).astype(F32)
    e_grp = jnp.floor(e_row * (1.0 / EXPERTS_PER_GROUP))
    neg = float("-inf")
    lm = jnp.where(e_grp == g_sel, le, neg)
    v1 = jnp.max(lm, axis=0, keepdims=True)
    i1 = jnp.min(jnp.where(lm == v1, e_row, float(N_EXPERTS)), axis=0, keepdims=True)
    lm2 = jnp.where(e_row == i1, neg, lm)
    v2 = jnp.max(lm2, axis=0, keepdims=True)
    i2 = jnp.min(jnp.where(lm2 == v2, e_row, float(N_EXPERTS)), axis=0, keepdims=True)
    e21 = jnp.exp(v2 - v1)
    p1 = 1.0 / (1.0 + e21)
    w1 = p1 * g_w
    w2 = e21 * p1 * g_w

    hit1 = e_row == i1
    hit2 = e_row == i2
    onehot = jnp.where(hit1 | hit2, 1.0, 0.0)
    cum = _dot(onehot.astype(BF16), tri_ref[...])
    before = cum - onehot + run_ref[...]
    r1 = jnp.sum(jnp.where(hit1, before, 0.0), axis=0, keepdims=True)
    r2 = jnp.sum(jnp.where(hit2, before, 0.0), axis=0, keepdims=True)
    run_ref[...] = run_ref[...] + jnp.broadcast_to(cum[:, rows - 1:rows], run_ref.shape)
    cnt_ref[...] = run_ref[...]
    zero = jnp.zeros_like(w1)
    route = jnp.concatenate([i1, i2, r1, r2, w1, w2, zero, zero], axis=0)
    route_ref[...] = route
    route_t_ref[...] = route.T


def _row_copy(src_hbm, src_token, dst_hbm, dst_token, sem):
    return pltpu.make_async_copy(_token_tile(src_hbm, src_token), _token_tile(dst_hbm, dst_token), sem)


def _dispatch_kernel(zstart_ref, dest_hbm, h_ref, xs_hbm, idx0_smem, idx1_smem, zeros_ref,
                     idx_sem, zero_sem, row_sem):
    i = pl.program_id(0)
    n_tiles = pl.num_programs(0)
    idx_smem = (idx0_smem, idx1_smem)
    n_idx = idx0_smem.shape[0]
    tile = n_idx // TOP_K

    def idx_copy(tile_index, s):
        src = dest_hbm.at[pl.ds(pl.multiple_of(tile_index * n_idx, n_idx), n_idx)]
        return pltpu.make_async_copy(src, idx_smem[s], idx_sem.at[s])

    @pl.when(i == 0)
    def _():
        idx_copy(0, 0).start()

    def zero_copy(j):
        piece_rows = ZERO_ROWS * SUBLANES
        start = pl.multiple_of(zstart_ref[j] * SUBLANES, piece_rows)
        return pltpu.make_async_copy(zeros_ref, xs_hbm.at[pl.ds(start, piece_rows)], zero_sem)

    @pl.when(i == 0)
    def _():
        zeros_ref[...] = jnp.zeros_like(zeros_ref)

        def start(j, carry):
            @pl.when(zstart_ref[j] >= 0)
            def _():
                zero_copy(j).start()
            return carry

        def wait(j, carry):
            @pl.when(zstart_ref[j] >= 0)
            def _():
                zero_copy(j).wait()
            return carry

        lax.fori_loop(0, zstart_ref.shape[0], start, 0)
        lax.fori_loop(0, zstart_ref.shape[0], wait, 0)

    def step(slot):
        @pl.when(i + 1 < n_tiles)
        def _():
            idx_copy(i + 1, 1 - slot).start()

        idx_copy(i, slot).wait()

        def start_rows(t, carry):
            for k in range(TOP_K):
                _row_copy(h_ref, t, xs_hbm, idx_smem[slot][k * tile + t], row_sem).start(priority=k)
            return carry

        lax.fori_loop(0, tile, start_rows, 0, unroll=8)
        for k in range(TOP_K):
            pltpu.make_async_copy(h_ref, xs_hbm.at[pl.ds(0, tile * SUBLANES)], row_sem).wait()

    for slot in range(2):
        pl.when(lax.rem(i, 2) == slot)(functools.partial(step, slot))


def _dispatch(zstart, dest_tiles, h, cap):
    t = h.shape[0] // SUBLANES
    any_spec = pl.BlockSpec(memory_space=pl.ANY)
    return pl.pallas_call(
        _dispatch_kernel,
        out_shape=jax.ShapeDtypeStruct((cap * SUBLANES, LANES), h.dtype),
        grid_spec=pltpu.PrefetchScalarGridSpec(
            num_scalar_prefetch=1,
            grid=(t // DISPATCH_ROWS,),
            in_specs=[any_spec,
                      pl.BlockSpec((DISPATCH_ROWS * SUBLANES, LANES), lambda i, zs: (i, 0))],
            out_specs=any_spec,
            scratch_shapes=[
                pltpu.SMEM((TOP_K * DISPATCH_ROWS,), jnp.int32),
                pltpu.SMEM((TOP_K * DISPATCH_ROWS,), jnp.int32),
                pltpu.VMEM((ZERO_ROWS * SUBLANES, LANES), h.dtype),
                pltpu.SemaphoreType.DMA((2,)),
                pltpu.SemaphoreType.DMA,
                pltpu.SemaphoreType.DMA,
            ],
        ),
        compiler_params=pltpu.CompilerParams(
            dimension_semantics=("arbitrary",), has_side_effects=True),
        name="moe_dispatch",
    )(zstart, dest_tiles, h)


def _expert_kernel(first_ref, nblk_ref, total_ref, xs_hbm, w1_ref, w3_ref, w2_ref, ys_hbm,
                   w1b_ref, w3b_ref, w2b_ref, xbuf_ref, ybuf_ref, in_sem, out_sem):
    e = pl.program_id(0)
    n_blocks = nblk_ref[e]
    first = first_ref[e]
    total = total_ref[0]
    block_rows = EXPERT_ROWS * SUBLANES

    def rows_of(block):
        return pl.ds(pl.multiple_of(block * block_rows, block_rows), block_rows)

    def in_copy(block, slot):
        return pltpu.make_async_copy(xs_hbm.at[rows_of(block)], xbuf_ref.at[slot], in_sem.at[slot])

    def out_copy(block, slot):
        return pltpu.make_async_copy(ybuf_ref.at[slot], ys_hbm.at[rows_of(block)], out_sem.at[slot])

    @pl.when(n_blocks > 0)
    def _():
        @pl.when(first == 0)
        def _():
            for b in range(IN_LOOKAHEAD):
                @pl.when(b < total)
                def _():
                    in_copy(b, b).start(priority=1)

        w1b_ref[...] = w1_ref[...].astype(BF16)
        w3b_ref[...] = w3_ref[...].astype(BF16)
        w2b_ref[...] = w2_ref[...].astype(BF16)

        def block_step(j, carry):
            g = first + j
            in_slot = lax.rem(g, IN_LOOKAHEAD + 1)
            slot = lax.rem(g, 2)

            @pl.when(g + IN_LOOKAHEAD < total)
            def _():
                in_copy(g + IN_LOOKAHEAD, lax.rem(g + IN_LOOKAHEAD, IN_LOOKAHEAD + 1)).start(priority=1)

            in_copy(g, in_slot).wait()

            @pl.when(g >= 2)
            def _():
                out_copy(g - 2, slot).wait()

            xb = _load_token_tiles(xbuf_ref.at[in_slot], 0, EXPERT_ROWS).astype(BF16)
            a = _dot(xb, w1b_ref[...])
            hid = a * _sigmoid(a) * _dot(xb, w3b_ref[...])
            _store_token_tiles(ybuf_ref.at[slot], _dot(hid.astype(BF16), w2b_ref[...]))
            out_copy(g, slot).start()
            return carry

        lax.fori_loop(0, n_blocks, block_step, 0)

        @pl.when(first + n_blocks == total)
        def _():
            @pl.when(total >= 2)
            def _():
                out_copy(total - 2, lax.rem(total, 2)).wait()

            out_copy(total - 1, lax.rem(total - 1, 2)).wait()


def _experts(l, first_block, n_blocks, total_blocks, xs, w1, w3, w2):
    _, n_experts, d, f = w1.shape
    tile = (EXPERT_ROWS * SUBLANES, LANES)
    any_spec = pl.BlockSpec(memory_space=pl.ANY)
    weight_spec = lambda rows, cols: pl.BlockSpec((None, None, rows, cols), lambda e, *_: (l, e, 0, 0))
    return pl.pallas_call(
        _expert_kernel,
        out_shape=jax.ShapeDtypeStruct(xs.shape, F32),
        grid_spec=pltpu.PrefetchScalarGridSpec(
            num_scalar_prefetch=3,
            grid=(n_experts,),
            in_specs=[any_spec, weight_spec(d, f), weight_spec(d, f), weight_spec(f, d)],
            out_specs=any_spec,
            scratch_shapes=[
                pltpu.VMEM((d, f), BF16), pltpu.VMEM((d, f), BF16), pltpu.VMEM((f, d), BF16),
                pltpu.VMEM((IN_LOOKAHEAD + 1,) + tile, F32), pltpu.VMEM((2,) + tile, F32),
                pltpu.SemaphoreType.DMA((IN_LOOKAHEAD + 1,)), pltpu.SemaphoreType.DMA((2,)),
            ],
        ),
        compiler_params=pltpu.CompilerParams(
            dimension_semantics=("arbitrary",), vmem_limit_bytes=VMEM_LIMIT_BYTES),
        name="moe_experts",
    )(first_block, n_blocks, total_blocks, xs, w1, w3, w2)


def _gather_expert_rows(dest_hbm, ys_hbm, idx_smem, ybuf_ref, idx_sem, row_sem):
    i = pl.program_id(0)
    n_tiles = pl.num_programs(0)
    n_idx = idx_smem[0].shape[0]
    tile = n_idx // TOP_K

    def idx_copy(tile_index, s):
        src = dest_hbm.at[pl.ds(pl.multiple_of(tile_index * n_idx, n_idx), n_idx)]
        return pltpu.make_async_copy(src, idx_smem[s], idx_sem.at[s])

    def start_rows(s):
        def body(t, carry):
            for k in range(TOP_K):
                row = k * tile + t
                pltpu.make_async_copy(_token_tile(ys_hbm, idx_smem[s][row]),
                                      _token_tile(ybuf_ref.at[s], row), row_sem.at[s]).start(priority=k)
            return carry
        lax.fori_loop(0, tile, body, 0, unroll=8)

    def wait_rows(s):
        pltpu.make_async_copy(ybuf_ref.at[s], ybuf_ref.at[s], row_sem.at[s]).wait()

    @pl.when(i == 0)
    def _():
        idx_copy(0, 0).start()
        idx_copy(0, 0).wait()
        start_rows(0)

        @pl.when(n_tiles > 1)
        def _():
            idx_copy(1, 1).start()

    def step(slot):
        other = 1 - slot

        @pl.when(i + 1 < n_tiles)
        def _():
            idx_copy(i + 1, other).wait()
            start_rows(other)

        @pl.when(i + 2 < n_tiles)
        def _():
            idx_copy(i + 2, slot).start()

        wait_rows(slot)

    for slot in range(2):
        pl.when(lax.rem(i, 2) == slot)(functools.partial(step, slot))
    return lax.rem(i, 2)


def _combined_residual(slot, ybuf_ref, x_ref, g2_ref, wt_ref):
    tile = x_ref.shape[0]
    wt = wt_ref[...]
    ytiles = ybuf_ref.at[slot]
    y = (wt[:, 2 * TOP_K:2 * TOP_K + 1] * _load_token_tiles(ytiles, 0, tile)
         + wt[:, 2 * TOP_K + 1:2 * TOP_K + 2] * _load_token_tiles(ytiles, tile, tile))
    return x_ref[...] + _per_batch(y, lambda y3: y3 * g2_ref[...])


def _combine_final_kernel(dest_hbm, ys_hbm, x_ref, g2_ref, wt_ref, fg_ref, o_ref,
                          idx0_smem, idx1_smem, ybuf_ref, idx_sem, row_sem):
    slot = _gather_expert_rows(dest_hbm, ys_hbm, (idx0_smem, idx1_smem), ybuf_ref, idx_sem, row_sem)
    out = _combined_residual(slot, ybuf_ref, x_ref, g2_ref, wt_ref)
    ms = jnp.mean(out * out, axis=-1, keepdims=True)
    out = out * lax.rsqrt(ms + NORM_EPS) * fg_ref[...]
    bsz, steps, d = o_ref.shape
    o_ref[...] = pltpu.einshape("sbd->bsd", out.reshape(steps, bsz, d))


def _combine_inproj_kernel(dest_hbm, ys_hbm, x_ref, g2_ref, wt_ref, g_ref, sc_ref, sh_ref, w_ref,
                           xo_ref, p_ref, idx0_smem, idx1_smem, ybuf_ref, idx_sem, row_sem):
    slot = _gather_expert_rows(dest_hbm, ys_hbm, (idx0_smem, idx1_smem), ybuf_ref, idx_sem, row_sem)
    out = _combined_residual(slot, ybuf_ref, x_ref, g2_ref, wt_ref)
    xo_ref[...] = out
    h = _norm_modulate(out, g_ref[...], sc_ref[...], sh_ref[...])
    p_ref[...] = _dot(h.astype(BF16), w_ref[...]).astype(p_ref.dtype)


def _combine(l, dest_tiles, ys, x, mod, wt, final_g=None, next_layer=None):
    t, d = x.shape
    any_spec = pl.BlockSpec(memory_space=pl.ANY)
    row_tile = lambda cols: pl.BlockSpec((COMBINE_ROWS, cols), lambda i: (i, 0))
    in_specs = [any_spec, any_spec, row_tile(d), _layer_spec(mod, l, MOD_GATE2), row_tile(SUBLANES)]
    if next_layer is None:
        body, operands = _combine_final_kernel, (final_g,)
        in_specs += [pl.BlockSpec((1, d), lambda i: (0, 0))]
        out_shape = jax.ShapeDtypeStruct((BATCH, t // BATCH, d), F32)
        out_specs = pl.BlockSpec((BATCH, COMBINE_ROWS // BATCH, d), lambda i: (0, i, 0))
    else:
        mix_g, w_in = next_layer
        n = w_in.shape[2]
        body, operands = _combine_inproj_kernel, (mix_g, mod, mod, w_in)
        in_specs += [_layer_spec(mix_g, l + 1), _layer_spec(mod, l + 1, MOD_SCALE1),
                     _layer_spec(mod, l + 1, MOD_SHIFT1), _layer_spec(w_in, l + 1)]
        out_shape = (jax.ShapeDtypeStruct((t, d), F32), jax.ShapeDtypeStruct((t, n), BF16))
        out_specs = (row_tile(d), row_tile(n))
    return pl.pallas_call(
        body,
        out_shape=out_shape,
        grid=(t // COMBINE_ROWS,),
        in_specs=in_specs,
        out_specs=out_specs,
        scratch_shapes=[
            pltpu.SMEM((TOP_K * COMBINE_ROWS,), jnp.int32),
            pltpu.SMEM((TOP_K * COMBINE_ROWS,), jnp.int32),
            pltpu.VMEM((2, TOP_K * COMBINE_ROWS * SUBLANES, LANES), F32),
            pltpu.SemaphoreType.DMA((2,)),
            pltpu.SemaphoreType.DMA((2,)),
        ],
        compiler_params=pltpu.CompilerParams(
            dimension_semantics=("arbitrary",), vmem_limit_bytes=VMEM_LIMIT_BYTES),
        name="moe_combine",
    )(dest_tiles, ys, x, mod, wt, *operands)


def _ssm_params(lam_re, lam_im, log_step, b_re, b_im, c_re, c_im):
    step = jnp.exp(log_step)[:, None]
    mag = jnp.exp(lam_re * step)
    a_re, a_im = mag * jnp.cos(lam_im * step), mag * jnp.sin(lam_im * step)
    den = lam_re * lam_re + lam_im * lam_im
    n_re, n_im = a_re - 1.0, a_im
    k_re = (n_re * lam_re + n_im * lam_im) / den
    k_im = (n_im * lam_re - n_re * lam_im) / den
    bb_re = k_re[..., None] * b_re - k_im[..., None] * b_im
    bb_im = k_re[..., None] * b_im + k_im[..., None] * b_re
    gpc = SSM_GROUPS // SSM_CHUNKS
    eye = jnp.eye(gpc, dtype=F32)

    def in_blocks(bb):
        bb = bb.reshape(SSM_CHUNKS, gpc, SSM_STATE, SSM_GROUP)
        return jnp.einsum("cgph,gk->cghkp", bb, eye).reshape(SSM_CHUNKS, SSM_CHUNK_IN, SSM_CHUNK_STATE)

    def out_blocks(cc):
        cc = cc.reshape(SSM_CHUNKS, gpc, SSM_GROUP, SSM_STATE)
        return jnp.einsum("cghp,gk->cgpkh", cc, eye).reshape(SSM_CHUNKS, SSM_CHUNK_STATE, SSM_CHUNK_IN)

    wb = jnp.concatenate([in_blocks(bb_re), in_blocks(bb_im)], axis=2).astype(BF16)
    wc = jnp.concatenate([out_blocks(c_re), out_blocks(-c_im)], axis=1).astype(BF16)
    bcast = lambda a: jnp.broadcast_to(a.reshape(SSM_CHUNKS, 1, SSM_CHUNK_STATE),
                                       (SSM_CHUNKS, BATCH, SSM_CHUNK_STATE))
    return wb, wc, bcast(a_re), bcast(a_im)


def _rows8(w):
    depth, k, c = w.shape
    return jnp.broadcast_to(w[:, :, None, :], (depth, k, SUBLANES, c)).reshape(depth, k * SUBLANES, c)


def _router_weights(w_rg, b_rg, w_re, b_re):
    depth, d, _ = w_rg.shape
    pad = SUBLANES - N_GROUPS
    tail = LANES - ROUTER_ROWS
    w_cols = jnp.concatenate([w_rg, jnp.zeros((depth, d, pad), F32), w_re,
                              jnp.zeros((depth, d, tail), F32)], axis=2)
    b = jnp.concatenate([b_rg, jnp.zeros((depth, pad), F32), b_re, jnp.zeros((depth, tail), F32)], axis=1)
    hi, lo = _split_bf16(w_cols)
    return hi, lo, b.reshape(depth, 1, LANES)


def _routing_tables(route, counts):
    t = route.shape[1]
    counts = counts.astype(jnp.int32)
    padded = (counts + EXPERT_ROWS - 1) // EXPERT_ROWS * EXPERT_ROWS
    pad_end = jnp.cumsum(padded)
    pad_start = pad_end - padded
    experts = route[0:TOP_K].astype(jnp.int32)
    ranks = route[TOP_K:2 * TOP_K].astype(jnp.int32)
    expert_ids = jnp.arange(N_EXPERTS, dtype=jnp.int32)
    is_expert = experts[:, :, None] == expert_ids
    dest = jnp.sum(jnp.where(is_expert, pad_start, 0), axis=-1) + ranks

    def tiles(rows_per_tile):
        n = t // rows_per_tile
        return dest.reshape(TOP_K, n, rows_per_tile).transpose(1, 0, 2).reshape(-1)

    capacity = TOP_K * t + N_EXPERTS * EXPERT_ROWS
    first_block = pad_start // EXPERT_ROWS
    n_blocks = padded // EXPERT_ROWS
    last_start = pad_end - EXPERT_ROWS
    last_valid = counts - (padded - EXPERT_ROWS)
    piece = jnp.arange(EXPERT_ROWS // ZERO_ROWS, dtype=jnp.int32)
    has_pad = (counts[:, None] > 0) & ((piece[None, :] + 1) * ZERO_ROWS > last_valid[:, None])
    zstart = jnp.where(has_pad, last_start[:, None] + piece[None, :] * ZERO_ROWS, -1).reshape(-1)
    total_blocks = (pad_end[-1] // EXPERT_ROWS).reshape(1)
    return (tiles(DISPATCH_ROWS), tiles(COMBINE_ROWS), first_block.astype(jnp.int32), n_blocks.astype(jnp.int32),
            total_blocks.astype(jnp.int32), zstart.astype(jnp.int32), capacity)


def kernel(x, c, norm_mix_g, norm_ffn_g, w_ada, b_ada, w_in, lam_re, lam_im, log_step, ssm_b_re, ssm_b_im, ssm_c_re, ssm_c_im, ssm_d, w_glu, conf_dw_w, conf_dw_b, conf_ln_g, conf_ln_b, sconv_w, w_branch, w_gate, b_gate, w_out, w_router_group, b_router_group, w_router_expert, b_router_expert, w_exp_gate, w_exp_up, w_exp_down, final_norm_g):
    bsz, seq, d = x.shape
    assert bsz == BATCH and d == D_MODEL
    depth = w_in.shape[0]
    t = bsz * seq
    rows = lambda v: v.reshape(depth, 1, -1)

    mix_g, ffn_g = rows(norm_mix_g), rows(norm_ffn_g)
    seq_params = (*jax.vmap(_ssm_params)(lam_re, lam_im, log_step, ssm_b_re, ssm_b_im, ssm_c_re, ssm_c_im),
                  rows(ssm_d), w_glu.astype(BF16), _rows8(conf_dw_w), rows(conf_dw_b), rows(conf_ln_g),
                  rows(conf_ln_b), _rows8(sconv_w))
    w_in_b, w_gate_b, w_branch_b, w_out_b = (w.astype(BF16) for w in (w_in, w_gate, w_branch, w_out))
    r_hi, r_lo, r_b = _router_weights(w_router_group, b_router_group, w_router_expert, b_router_expert)
    tri = jnp.triu(jnp.ones((ROUTE_ROWS, ROUTE_ROWS), BF16))

    mod = _ada(c, w_ada, b_ada)
    xt, proj = _inproj_first(x, mix_g, mod, w_in_b)
    for l in range(depth):
        ycat = _seq_mixers(l, proj, *seq_params)
        xt, h2, route, route_t, counts = _merge_router(l, xt, mix_g, ffn_g, mod, ycat, w_gate_b, rows(b_gate),
                                                       w_branch_b, w_out_b, r_hi, r_lo, r_b, tri)
        (dest_d, dest_c, first_block, n_blocks, total_blocks, zstart,
         cap) = _routing_tables(route, counts[:, 0])
        xs = _dispatch(zstart, dest_d, h2, cap)
        ys = _experts(l, first_block, n_blocks, total_blocks, xs, w_exp_gate, w_exp_up, w_exp_down)
        if l + 1 < depth:
            xt, proj = _combine(l, dest_c, ys, xt, mod, route_t, next_layer=(mix_g, w_in_b))
        else:
            return _combine(l, dest_c, ys, xt, mod, route_t, final_g=final_norm_g.reshape(1, d))
```

```python
import functools
import math

import jax
import jax.numpy as jnp
from jax import lax
from jax.experimental import pallas as pl
from jax.experimental.pallas import tpu as pltpu

F32 = jnp.float32
BF16 = jnp.bfloat16

D_MODEL = 1024
BATCH = 8
BRANCH_WIDTH = D_MODEL // 2
N_BRANCH = 3
N_IN_CHUNKS = 6
SSM_GROUP = 16
SSM_GROUPS = BRANCH_WIDTH // SSM_GROUP
SSM_STATE = 64
CONF_KERNEL = 31
SCONV_KERNEL = 3
N_GROUPS = 4
EXPERTS_PER_GROUP = 8
N_EXPERTS = N_GROUPS * EXPERTS_PER_GROUP
TOP_K = 2
D_FF_EXPERT = D_MODEL // 2
NORM_EPS = 1e-6

SUBLANES = 8
LANES = 128
VMEM_LIMIT_BYTES = 56 * 1024 * 1024

SSM_CHUNKS = 2
SSM_CHUNK_IN = BRANCH_WIDTH // SSM_CHUNKS
SSM_CHUNK_STATE = SSM_GROUPS * SSM_STATE // SSM_CHUNKS
CONF_HIST = (CONF_KERNEL - 1) * BATCH
SCONV_HIST = (SCONV_KERNEL - 1) * BATCH
ROUTER_ROWS = 40
MOD_SHIFT1, MOD_SCALE1, MOD_GATE1, MOD_SHIFT2, MOD_SCALE2, MOD_GATE2 = range(6)

ADA_COLS = 1536
INPROJ_ROWS = 1024
SEQ_ROWS = 512
CONV_ROWS = 64
EW_ROWS = 32
MERGE_ROWS = 512
ROUTE_ROWS = MERGE_ROWS
DISPATCH_ROWS = 512
EXPERT_ROWS = 512
IN_LOOKAHEAD = 3
OUT_SLOTS = 3
ZERO_ROWS = 64
COMBINE_ROWS = 512


def _dot(a, b):
    return jnp.dot(a, b, preferred_element_type=F32)


def _split_bf16(v):
    hi = v.astype(BF16)
    lo = (v - hi.astype(F32)).astype(BF16)
    return hi, lo


def _sigmoid(v):
    return 1.0 / (1.0 + jnp.exp(-v))


def _gelu_tanh(v):
    return 0.5 * v * (1.0 + jnp.tanh(math.sqrt(2.0 / math.pi) * (v + 0.044715 * (v * v * v))))


def _per_batch(v, fn):
    rows, d = v.shape
    return fn(v.reshape(rows // BATCH, BATCH, d)).reshape(rows, d)


def _load_token_tiles(ref, first_token, n_tokens):
    base = first_token * SUBLANES
    chunks = [ref[pl.ds(base + s, n_tokens, stride=SUBLANES), :] for s in range(SUBLANES)]
    return jnp.concatenate(chunks, axis=1)


def _store_token_tiles(ref, value):
    for s in range(SUBLANES):
        ref[pl.ds(s, value.shape[0], stride=SUBLANES), :] = value[:, s * LANES:(s + 1) * LANES]


def _token_tile(ref, token):
    return ref.at[pl.ds(pl.multiple_of(token * SUBLANES, SUBLANES), SUBLANES)]


def _layer_spec(arr, l, col_block=None):
    tail = arr.shape[1:]
    if col_block is None:
        zeros = (0,) * len(tail)
        return pl.BlockSpec((None,) + tail, lambda *_: (l,) + zeros)
    return pl.BlockSpec((None, tail[0], D_MODEL), lambda *_: (l, 0, col_block))


def _norm_modulate(x, g, scale, shift):
    ms = jnp.mean(x * x, axis=-1, keepdims=True)
    h = x * lax.rsqrt(ms + NORM_EPS) * g
    return _per_batch(h, lambda h3: h3 * (1.0 + scale) + shift)


def _ada_kernel(c_ref, w_ref, b_ref, o_ref):
    c = c_ref[...]
    cond = c * _sigmoid(c)
    chi, clo = _split_bf16(cond)
    whi, wlo = _split_bf16(w_ref[...])
    o_ref[...] = _dot(chi, whi) + _dot(chi, wlo) + _dot(clo, whi) + b_ref[...]


def _ada(c, w_ada, b_ada):
    depth, d, n = w_ada.shape
    return pl.pallas_call(
        _ada_kernel,
        out_shape=jax.ShapeDtypeStruct((depth, BATCH, n), F32),
        grid=(depth, n // ADA_COLS),
        in_specs=[
            pl.BlockSpec((BATCH, d), lambda l, j: (0, 0)),
            pl.BlockSpec((None, d, ADA_COLS), lambda l, j: (l, 0, j)),
            pl.BlockSpec((None, 1, ADA_COLS), lambda l, j: (l, 0, j)),
        ],
        out_specs=pl.BlockSpec((None, BATCH, ADA_COLS), lambda l, j: (l, 0, j)),
        compiler_params=pltpu.CompilerParams(
            dimension_semantics=("arbitrary", "arbitrary"), vmem_limit_bytes=VMEM_LIMIT_BYTES),
        name="ada_modulation",
    )(c, w_ada, b_ada.reshape(depth, 1, n))


def _inproj_kernel(x_ref, g_ref, sc_ref, sh_ref, w_ref, xt_ref, o_ref):
    bsz, steps, d = x_ref.shape
    x = pltpu.einshape("bsd->sbd", x_ref[...]).reshape(steps * bsz, d)
    xt_ref[...] = x
    h = _norm_modulate(x, g_ref[...], sc_ref[...], sh_ref[...])
    o_ref[...] = _dot(h.astype(BF16), w_ref[...]).astype(o_ref.dtype)


def _inproj_first(x, g, mod, w_in):
    bsz, seq, d = x.shape
    n = w_in.shape[2]
    steps = INPROJ_ROWS // bsz
    return pl.pallas_call(
        _inproj_kernel,
        out_shape=(jax.ShapeDtypeStruct((bsz * seq, d), F32), jax.ShapeDtypeStruct((bsz * seq, n), BF16)),
        grid=(seq // steps,),
        in_specs=[
            pl.BlockSpec((bsz, steps, d), lambda i: (0, i, 0)),
            _layer_spec(g, 0),
            _layer_spec(mod, 0, MOD_SCALE1),
            _layer_spec(mod, 0, MOD_SHIFT1),
            _layer_spec(w_in, 0),
        ],
        out_specs=(pl.BlockSpec((INPROJ_ROWS, d), lambda i: (i, 0)),
                   pl.BlockSpec((INPROJ_ROWS, n), lambda i: (i, 0))),
        compiler_params=pltpu.CompilerParams(
            dimension_semantics=("arbitrary",), vmem_limit_bytes=VMEM_LIMIT_BYTES),
        name="mixer_inproj",
    )(x, g, mod, mod, w_in)


def _seq_kernel(p_ref, wb_ref, wc_ref, are_ref, aim_ref, d_ref, wglu_ref, cw_ref, cb_ref,
                lng_ref, lnb_ref, sw_ref, o_ref, bu_ref, st_ref, zbuf_ref, cbuf_ref, qbuf_ref):
    rows = p_ref.shape[0]
    steps = rows // BATCH
    w = BRANCH_WIDTH

    @pl.when(pl.program_id(0) == 0)
    def _():
        st_ref[...] = jnp.zeros_like(st_ref)
        zbuf_ref[0:CONF_HIST, :] = jnp.zeros((CONF_HIST, w), F32)
        qbuf_ref[0:SCONV_HIST, :] = jnp.zeros((SCONV_HIST, w), F32)

    chunk_groups = EW_ROWS // SUBLANES

    def pointwise_chunk(j):
        r0 = j * EW_ROWS
        rs = slice(r0, r0 + EW_ROWS)
        gate_c = p_ref[rs, 4 * w:5 * w].astype(F32)
        hv = p_ref[rs, 5 * w:6 * w].astype(F32)
        qbuf_ref[r0 + SCONV_HIST:r0 + SCONV_HIST + EW_ROWS, :] = gate_c * hv
        q3 = qbuf_ref[r0:r0 + EW_ROWS + SCONV_HIST, :].reshape(
            chunk_groups + SCONV_KERNEL - 1, SUBLANES, w)
        acc = q3[0:chunk_groups] * sw_ref[0:SUBLANES, :]
        for k in range(1, SCONV_KERNEL):
            acc = acc + q3[k:k + chunk_groups] * sw_ref[k * SUBLANES:(k + 1) * SUBLANES, :]
        gate_b = p_ref[rs, 3 * w:4 * w].astype(F32)
        o_ref[rs, 2 * w:3 * w] = (gate_b * acc.reshape(EW_ROWS, w)).astype(o_ref.dtype)
        v = p_ref[rs, w:2 * w].astype(F32)
        g = p_ref[rs, 2 * w:3 * w].astype(F32)
        zbuf_ref[r0 + CONF_HIST:r0 + CONF_HIST + EW_ROWS, :] = v * _sigmoid(g)

    def input_matmul(c):
        u_c = p_ref[:, c * SSM_CHUNK_IN:(c + 1) * SSM_CHUNK_IN]
        bu_ref[c] = _dot(u_c, wb_ref[c])

    def scan(c):
        a_re = are_ref[c]
        a_im = aim_ref[c]
        x_re = st_ref[c, 0]
        x_im = st_ref[c, 1]
        for t in range(steps):
            rs = slice(t * BATCH, (t + 1) * BATCH)
            n_re = a_re * x_re - a_im * x_im + bu_ref[c, rs, 0:SSM_CHUNK_STATE]
            n_im = a_re * x_im + a_im * x_re + bu_ref[c, rs, SSM_CHUNK_STATE:]
            bu_ref[c, rs, 0:SSM_CHUNK_STATE] = n_re
            bu_ref[c, rs, SSM_CHUNK_STATE:] = n_im
            x_re, x_im = n_re, n_im
        st_ref[c, 0] = x_re
        st_ref[c, 1] = x_im

    def output_matmul(c):
        return _dot(bu_ref[c].astype(BF16), wc_ref[c])

    input_matmul(0)
    for j in range(rows // EW_ROWS):
        pointwise_chunk(j)
    qbuf_ref[0:SCONV_HIST, :] = qbuf_ref[rows:rows + SCONV_HIST, :]
    input_matmul(1)
    scan(0)
    y0 = output_matmul(0)
    scan(1)
    ys = [y0, output_matmul(1)]
    u = p_ref[:, 0:w].astype(F32)
    y = _gelu_tanh(jnp.concatenate(ys, axis=1) + d_ref[...] * u)
    y_ssm = y * _sigmoid(_dot(y.astype(BF16), wglu_ref[...]))
    o_ref[:, 0:w] = y_ssm.astype(o_ref.dtype)

    win_groups = (CONV_ROWS + CONF_HIST) // SUBLANES
    out_groups = CONV_ROWS // SUBLANES

    def conv_chunk(j, carry):
        r0 = pl.multiple_of(j * CONV_ROWS, CONV_ROWS)
        for s in range(w // LANES):
            ls = slice(s * LANES, (s + 1) * LANES)
            win = zbuf_ref[pl.ds(r0, CONV_ROWS + CONF_HIST), ls].reshape(win_groups, SUBLANES, LANES)
            acc = win[0:out_groups] * cw_ref[0:SUBLANES, ls]
            for k in range(1, CONF_KERNEL):
                acc = acc + win[k:k + out_groups] * cw_ref[k * SUBLANES:(k + 1) * SUBLANES, ls]
            cbuf_ref[pl.ds(r0, CONV_ROWS), ls] = acc.reshape(CONV_ROWS, LANES)
        return carry

    lax.fori_loop(0, rows // CONV_ROWS, conv_chunk, 0)
    zbuf_ref[0:CONF_HIST, :] = zbuf_ref[rows:rows + CONF_HIST, :]
    cv = cbuf_ref[...] + cb_ref[...]
    cc = cv - jnp.mean(cv, axis=-1, keepdims=True)
    ln = cc * lax.rsqrt(jnp.mean(cc * cc, axis=-1, keepdims=True) + NORM_EPS) * lng_ref[...] + lnb_ref[...]
    o_ref[:, w:2 * w] = (ln * _sigmoid(ln)).astype(o_ref.dtype)


def _seq_mixers(l, proj, *layer_params):
    t = proj.shape[0]
    w = BRANCH_WIDTH
    return pl.pallas_call(
        _seq_kernel,
        out_shape=jax.ShapeDtypeStruct((t, N_BRANCH * w), BF16),
        grid=(t // SEQ_ROWS,),
        in_specs=[pl.BlockSpec((SEQ_ROWS, N_IN_CHUNKS * w), lambda i: (i, 0))]
                 + [_layer_spec(p, l) for p in layer_params],
        out_specs=pl.BlockSpec((SEQ_ROWS, N_BRANCH * w), lambda i: (i, 0)),
        scratch_shapes=[
            pltpu.VMEM((SSM_CHUNKS, SEQ_ROWS, 2 * SSM_CHUNK_STATE), F32),
            pltpu.VMEM((SSM_CHUNKS, 2, BATCH, SSM_CHUNK_STATE), F32),
            pltpu.VMEM((SEQ_ROWS + CONF_HIST, w), F32),
            pltpu.VMEM((SEQ_ROWS, w), F32),
            pltpu.VMEM((SEQ_ROWS + SCONV_HIST, w), F32),
        ],
        compiler_params=pltpu.CompilerParams(
            dimension_semantics=("arbitrary",), vmem_limit_bytes=VMEM_LIMIT_BYTES),
        name="seq_mixers",
    )(proj, *layer_params)


def _merge_router_kernel(x_ref, g_ref, sc_ref, sh_ref, g1_ref, y_ref, wg_ref, bg_ref, wbr_ref, wo_ref,
                         g2_ref, sc2_ref, sh2_ref, whi_ref, wlo_ref, b_ref, tri_ref,
                         o_ref, h_ref, route_ref, route_t_ref, cnt_ref, run_ref):
    x = x_ref[...]
    d = x.shape[1]
    w = BRANCH_WIDTH
    hb = _norm_modulate(x, g_ref[...], sc_ref[...], sh_ref[...]).astype(BF16)
    merged = None
    for n in range(N_BRANCH):
        gate = _sigmoid(_dot(hb, wg_ref[:, n * d:(n + 1) * d]) + bg_ref[:, n * d:(n + 1) * d])
        term = gate * _dot(y_ref[:, n * w:(n + 1) * w], wbr_ref[n])
        merged = term if merged is None else merged + term
    out = _dot(merged.astype(BF16), wo_ref[...])
    x_new = x + _per_batch(out, lambda o3: o3 * g1_ref[...])
    o_ref[...] = x_new
    _route_tile(x_new, g2_ref, sc2_ref, sh2_ref, whi_ref, wlo_ref, b_ref, tri_ref,
                h_ref, route_ref, route_t_ref, cnt_ref, run_ref)


def _merge_router(l, x, mix_g, ffn_g, mod, ycat, w_gate, b_gate, w_branch, w_out, r_hi, r_lo, r_bias, tri):
    t, d = x.shape
    w = BRANCH_WIDTH
    row_tile = lambda cols: pl.BlockSpec((MERGE_ROWS, cols), lambda i: (i, 0))
    return pl.pallas_call(
        _merge_router_kernel,
        out_shape=(
            jax.ShapeDtypeStruct((t, d), F32),
            jax.ShapeDtypeStruct((t * SUBLANES, LANES), F32),
            jax.ShapeDtypeStruct((SUBLANES, t), F32),
            jax.ShapeDtypeStruct((t, SUBLANES), F32),
            jax.ShapeDtypeStruct((N_EXPERTS, MERGE_ROWS), F32),
        ),
        grid=(t // MERGE_ROWS,),
        in_specs=[
            row_tile(d),
            _layer_spec(mix_g, l),
            _layer_spec(mod, l, MOD_SCALE1),
            _layer_spec(mod, l, MOD_SHIFT1),
            _layer_spec(mod, l, MOD_GATE1),
            row_tile(N_BRANCH * w),
            _layer_spec(w_gate, l),
            _layer_spec(b_gate, l),
            _layer_spec(w_branch, l),
            _layer_spec(w_out, l),
            _layer_spec(ffn_g, l),
            _layer_spec(mod, l, MOD_SCALE2),
            _layer_spec(mod, l, MOD_SHIFT2),
            _layer_spec(r_hi, l),
            _layer_spec(r_lo, l),
            _layer_spec(r_bias, l),
            pl.BlockSpec((MERGE_ROWS, MERGE_ROWS), lambda i: (0, 0)),
        ],
        out_specs=(
            row_tile(d),
            pl.BlockSpec((MERGE_ROWS * SUBLANES, LANES), lambda i: (i, 0)),
            pl.BlockSpec((SUBLANES, MERGE_ROWS), lambda i: (0, i)),
            row_tile(SUBLANES),
            pl.BlockSpec((N_EXPERTS, MERGE_ROWS), lambda i: (0, 0)),
        ),
        scratch_shapes=[pltpu.VMEM((N_EXPERTS, MERGE_ROWS), F32)],
        compiler_params=pltpu.CompilerParams(
            dimension_semantics=("arbitrary",), vmem_limit_bytes=VMEM_LIMIT_BYTES),
        name="mixer_merge_router",
    )(x, mix_g, mod, mod, mod, ycat, w_gate, b_gate, w_branch, w_out, ffn_g, mod, mod, r_hi, r_lo, r_bias, tri)


def _route_tile(x, g_ref, sc_ref, sh_ref, whi_ref, wlo_ref, b_ref, tri_ref,
                h_ref, route_ref, route_t_ref, cnt_ref, run_ref):
    @pl.when(pl.program_id(0) == 0)
    def _():
        run_ref[...] = jnp.zeros_like(run_ref)

    h = _norm_modulate(x, g_ref[...], sc_ref[...], sh_ref[...])
    _store_token_tiles(h_ref, h)
    rows = h.shape[0]
    hhi, hlo = _split_bf16(h)
    logits = (_dot(hhi, whi_ref[...]) + _dot(hhi, wlo_ref[...]) + _dot(hlo, whi_ref[...]) + b_ref[...]).T
    lg = logits[0:N_GROUPS]
    le = logits[SUBLANES:SUBLANES + N_EXPERTS]

    g_row = lax.broadcasted_iota(jnp.int32, (N_GROUPS, rows), 0).astype(F32)
    g_max = jnp.max(lg, axis=0, keepdims=True)
    g_sel = jnp.min(jnp.where(lg == g_max, g_row, float(N_GROUPS)), axis=0, keepdims=True)
    g_w = 1.0 / jnp.sum(jnp.exp(lg - g_max), axis=0, keepdims=True)

    e_row = lax.broadcasted_iota(jnp.int32, (N_EXPERTS, rows), 0).astype(F32)
    e_grp = jnp.floor(e_row * (1.0 / EXPERTS_PER_GROUP))
    neg = float("-inf")
    lm = jnp.where(e_grp == g_sel, le, neg)
    v1 = jnp.max(lm, axis=0, keepdims=True)
    i1 = jnp.min(jnp.where(lm == v1, e_row, float(N_EXPERTS)), axis=0, keepdims=True)
    lm2 = jnp.where(e_row == i1, neg, lm)
    v2 = jnp.max(lm2, axis=0, keepdims=True)
    i2 = jnp.min(jnp.where(lm2 == v2, e_row, float(N_EXPERTS)), axis=0, keepdims=True)
    e21 = jnp.exp(v2 - v1)
    p1 = 1.0 / (1.0 + e21)
    w1 = p1 * g_w
    w2 = e21 * p1 * g_w

    hit1 = e_row == i1
    hit2 = e_row == i2
    onehot = jnp.where(hit1 | hit2, 1.0, 0.0)
    cum = _dot(onehot.astype(BF16), tri_ref[...])
    before = cum - onehot + run_ref[...]
    r1 = jnp.sum(jnp.where(hit1, before, 0.0), axis=0, keepdims=True)
    r2 = jnp.sum(jnp.where(hit2, before, 0.0), axis=0, keepdims=True)
    run_ref[...] = run_ref[...] + jnp.broadcast_to(cum[:, rows - 1:rows], run_ref.shape)
    cnt_ref[...] = run_ref[...]
    zero = jnp.zeros_like(w1)
    route = jnp.concatenate([i1, i2, r1, r2, w1, w2, zero, zero], axis=0)
    route_ref[...] = route
    route_t_ref[...] = route.T


def _row_copy(src_hbm, src_token, dst_hbm, dst_token, sem):
    return pltpu.make_async_copy(_token_tile(src_hbm, src_token), _token_tile(dst_hbm, dst_token), sem)


def _dispatch_kernel(zstart_ref, dest_hbm, h_ref, xs_hbm, idx0_smem, idx1_smem, zeros_ref,
                     idx_sem, zero_sem, row_sem):
    i = pl.program_id(0)
    n_tiles = pl.num_programs(0)
    idx_smem = (idx0_smem, idx1_smem)
    n_idx = idx0_smem.shape[0]
    tile = n_idx // TOP_K

    def idx_copy(tile_index, s):
        src = dest_hbm.at[pl.ds(pl.multiple_of(tile_index * n_idx, n_idx), n_idx)]
        return pltpu.make_async_copy(src, idx_smem[s], idx_sem.at[s])

    @pl.when(i == 0)
    def _():
        idx_copy(0, 0).start()

    def zero_copy(j):
        piece_rows = ZERO_ROWS * SUBLANES
        start = pl.multiple_of(zstart_ref[j] * SUBLANES, piece_rows)
        return pltpu.make_async_copy(zeros_ref, xs_hbm.at[pl.ds(start, piece_rows)], zero_sem)

    @pl.when(i == 0)
    def _():
        zeros_ref[...] = jnp.zeros_like(zeros_ref)

        def start(j, carry):
            @pl.when(zstart_ref[j] >= 0)
            def _():
                zero_copy(j).start()
            return carry

        def wait(j, carry):
            @pl.when(zstart_ref[j] >= 0)
            def _():
                zero_copy(j).wait()
            return carry

        lax.fori_loop(0, zstart_ref.shape[0], start, 0)
        lax.fori_loop(0, zstart_ref.shape[0], wait, 0)

    def step(slot):
        @pl.when(i + 1 < n_tiles)
        def _():
            idx_copy(i + 1, 1 - slot).start()

        idx_copy(i, slot).wait()

        def start_rows(t, carry):
            for k in range(TOP_K):
                _row_copy(h_ref, t, xs_hbm, idx_smem[slot][k * tile + t], row_sem).start(priority=k)
            return carry

        lax.fori_loop(0, tile, start_rows, 0, unroll=8)
        for k in range(TOP_K):
            pltpu.make_async_copy(h_ref, xs_hbm.at[pl.ds(0, tile * SUBLANES)], row_sem).wait()

    for slot in range(2):
        pl.when(lax.rem(i, 2) == slot)(functools.partial(step, slot))


def _dispatch(zstart, dest_tiles, h, cap):
    t = h.shape[0] // SUBLANES
    any_spec = pl.BlockSpec(memory_space=pl.ANY)
    return pl.pallas_call(
        _dispatch_kernel,
        out_shape=jax.ShapeDtypeStruct((cap * SUBLANES, LANES), h.dtype),
        grid_spec=pltpu.PrefetchScalarGridSpec(
            num_scalar_prefetch=1,
            grid=(t // DISPATCH_ROWS,),
            in_specs=[any_spec,
                      pl.BlockSpec((DISPATCH_ROWS * SUBLANES, LANES), lambda i, zs: (i, 0))],
            out_specs=any_spec,
            scratch_shapes=[
                pltpu.SMEM((TOP_K * DISPATCH_ROWS,), jnp.int32),
                pltpu.SMEM((TOP_K * DISPATCH_ROWS,), jnp.int32),
                pltpu.VMEM((ZERO_ROWS * SUBLANES, LANES), h.dtype),
                pltpu.SemaphoreType.DMA((2,)),
                pltpu.SemaphoreType.DMA,
                pltpu.SemaphoreType.DMA,
            ],
        ),
        compiler_params=pltpu.CompilerParams(
            dimension_semantics=("arbitrary",), has_side_effects=True),
        name="moe_dispatch",
    )(zstart, dest_tiles, h)


def _expert_kernel(first_ref, nblk_ref, total_ref, xs_hbm, w1_ref, w3_ref, w2_ref, ys_hbm,
                   w1b_ref, w3b_ref, w2b_ref, xbuf_ref, ybuf_ref, in_sem, out_sem):
    e = pl.program_id(0)
    n_blocks = nblk_ref[e]
    first = first_ref[e]
    total = total_ref[0]
    block_rows = EXPERT_ROWS * SUBLANES

    def rows_of(block):
        return pl.ds(pl.multiple_of(block * block_rows, block_rows), block_rows)

    def in_copy(block, slot):
        return pltpu.make_async_copy(xs_hbm.at[rows_of(block)], xbuf_ref.at[slot], in_sem.at[slot])

    def out_copy(block, slot):
        return pltpu.make_async_copy(ybuf_ref.at[slot], ys_hbm.at[rows_of(block)], out_sem.at[slot])

    @pl.when(n_blocks > 0)
    def _():
        @pl.when(first == 0)
        def _():
            for b in range(IN_LOOKAHEAD):
                @pl.when(b < total)
                def _():
                    in_copy(b, b).start(priority=1)

        w1b_ref[...] = w1_ref[...].astype(BF16)
        w3b_ref[...] = w3_ref[...].astype(BF16)
        w2b_ref[...] = w2_ref[...].astype(BF16)

        def block_step(j, carry):
            g = first + j
            in_slot = lax.rem(g, IN_LOOKAHEAD + 1)
            slot = lax.rem(g, OUT_SLOTS)

            @pl.when(g + IN_LOOKAHEAD < total)
            def _():
                in_copy(g + IN_LOOKAHEAD, lax.rem(g + IN_LOOKAHEAD, IN_LOOKAHEAD + 1)).start(priority=1)

            in_copy(g, in_slot).wait()

            @pl.when(g >= OUT_SLOTS)
            def _():
                out_copy(g - OUT_SLOTS, slot).wait()

            xb = _load_token_tiles(xbuf_ref.at[in_slot], 0, EXPERT_ROWS).astype(BF16)
            a = _dot(xb, w1b_ref[...])
            hid = a * _sigmoid(a) * _dot(xb, w3b_ref[...])
            _store_token_tiles(ybuf_ref.at[slot], _dot(hid.astype(BF16), w2b_ref[...]))
            out_copy(g, slot).start()
            return carry

        lax.fori_loop(0, n_blocks, block_step, 0)

        @pl.when(first + n_blocks == total)
        def _():
            for back in range(OUT_SLOTS, 0, -1):
                @pl.when(total >= back)
                def _():
                    out_copy(total - back, lax.rem(total - back, OUT_SLOTS)).wait()


def _experts(l, first_block, n_blocks, total_blocks, xs, w1, w3, w2):
    _, n_experts, d, f = w1.shape
    tile = (EXPERT_ROWS * SUBLANES, LANES)
    any_spec = pl.BlockSpec(memory_space=pl.ANY)
    weight_spec = lambda rows, cols: pl.BlockSpec((None, None, rows, cols), lambda e, *_: (l, e, 0, 0))
    return pl.pallas_call(
        _expert_kernel,
        out_shape=jax.ShapeDtypeStruct(xs.shape, F32),
        grid_spec=pltpu.PrefetchScalarGridSpec(
            num_scalar_prefetch=3,
            grid=(n_experts,),
            in_specs=[any_spec, weight_spec(d, f), weight_spec(d, f), weight_spec(f, d)],
            out_specs=any_spec,
            scratch_shapes=[
                pltpu.VMEM((d, f), BF16), pltpu.VMEM((d, f), BF16), pltpu.VMEM((f, d), BF16),
                pltpu.VMEM((IN_LOOKAHEAD + 1,) + tile, F32), pltpu.VMEM((OUT_SLOTS,) + tile, F32),
                pltpu.SemaphoreType.DMA((IN_LOOKAHEAD + 1,)), pltpu.SemaphoreType.DMA((OUT_SLOTS,)),
            ],
        ),
        compiler_params=pltpu.CompilerParams(
            dimension_semantics=("arbitrary",), vmem_limit_bytes=VMEM_LIMIT_BYTES),
        name="moe_experts",
    )(first_block, n_blocks, total_blocks, xs, w1, w3, w2)


def _gather_expert_rows(dest_hbm, ys_hbm, idx_smem, ybuf_ref, idx_sem, row_sem):
    i = pl.program_id(0)
    n_tiles = pl.num_programs(0)
    n_idx = idx_smem[0].shape[0]
    tile = n_idx // TOP_K

    def idx_copy(tile_index, s):
        src = dest_hbm.at[pl.ds(pl.multiple_of(tile_index * n_idx, n_idx), n_idx)]
        return pltpu.make_async_copy(src, idx_smem[s], idx_sem.at[s])

    def start_rows(s):
        def body(t, carry):
            for k in range(TOP_K):
                row = k * tile + t
                pltpu.make_async_copy(_token_tile(ys_hbm, idx_smem[s][row]),
                                      _token_tile(ybuf_ref.at[s], row), row_sem.at[s]).start(priority=k)
            return carry
        lax.fori_loop(0, tile, body, 0, unroll=8)

    def wait_rows(s):
        pltpu.make_async_copy(ybuf_ref.at[s], ybuf_ref.at[s], row_sem.at[s]).wait()

    @pl.when(i == 0)
    def _():
        idx_copy(0, 0).start()
        idx_copy(0, 0).wait()
        start_rows(0)

        @pl.when(n_tiles > 1)
        def _():
            idx_copy(1, 1).start()

    def step(slot):
        other = 1 - slot

        @pl.when(i + 1 < n_tiles)
        def _():
            idx_copy(i + 1, other).wait()
            start_rows(other)

        @pl.when(i + 2 < n_tiles)
        def _():
            idx_copy(i + 2, slot).start()

        wait_rows(slot)

    for slot in range(2):
        pl.when(lax.rem(i, 2) == slot)(functools.partial(step, slot))
    return lax.rem(i, 2)


def _combined_residual(slot, ybuf_ref, x_ref, g2_ref, wt_ref):
    tile = x_ref.shape[0]
    wt = wt_ref[...]
    ytiles = ybuf_ref.at[slot]
    y = (wt[:, 2 * TOP_K:2 * TOP_K + 1] * _load_token_tiles(ytiles, 0, tile)
         + wt[:, 2 * TOP_K + 1:2 * TOP_K + 2] * _load_token_tiles(ytiles, tile, tile))
    return x_ref[...] + _per_batch(y, lambda y3: y3 * g2_ref[...])


def _combine_final_kernel(dest_hbm, ys_hbm, x_ref, g2_ref, wt_ref, fg_ref, o_ref,
                          idx0_smem, idx1_smem, ybuf_ref, idx_sem, row_sem):
    slot = _gather_expert_rows(dest_hbm, ys_hbm, (idx0_smem, idx1_smem), ybuf_ref, idx_sem, row_sem)
    out = _combined_residual(slot, ybuf_ref, x_ref, g2_ref, wt_ref)
    ms = jnp.mean(out * out, axis=-1, keepdims=True)
    out = out * lax.rsqrt(ms + NORM_EPS) * fg_ref[...]
    bsz, steps, d = o_ref.shape
    o_ref[...] = pltpu.einshape("sbd->bsd", out.reshape(steps, bsz, d))


def _combine_inproj_kernel(dest_hbm, ys_hbm, x_ref, g2_ref, wt_ref, g_ref, sc_ref, sh_ref, w_ref,
                           xo_ref, p_ref, idx0_smem, idx1_smem, ybuf_ref, idx_sem, row_sem):
    slot = _gather_expert_rows(dest_hbm, ys_hbm, (idx0_smem, idx1_smem), ybuf_ref, idx_sem, row_sem)
    out = _combined_residual(slot, ybuf_ref, x_ref, g2_ref, wt_ref)
    xo_ref[...] = out
    h = _norm_modulate(out, g_ref[...], sc_ref[...], sh_ref[...])
    p_ref[...] = _dot(h.astype(BF16), w_ref[...]).astype(p_ref.dtype)


def _combine(l, dest_tiles, ys, x, mod, wt, final_g=None, next_layer=None):
    t, d = x.shape
    any_spec = pl.BlockSpec(memory_space=pl.ANY)
    row_tile = lambda cols: pl.BlockSpec((COMBINE_ROWS, cols), lambda i: (i, 0))
    in_specs = [any_spec, any_spec, row_tile(d), _layer_spec(mod, l, MOD_GATE2), row_tile(SUBLANES)]
    if next_layer is None:
        body, operands = _combine_final_kernel, (final_g,)
        in_specs += [pl.BlockSpec((1, d), lambda i: (0, 0))]
        out_shape = jax.ShapeDtypeStruct((BATCH, t // BATCH, d), F32)
        out_specs = pl.BlockSpec((BATCH, COMBINE_ROWS // BATCH, d), lambda i: (0, i, 0))
    else:
        mix_g, w_in = next_layer
        n = w_in.shape[2]
        body, operands = _combine_inproj_kernel, (mix_g, mod, mod, w_in)
        in_specs += [_layer_spec(mix_g, l + 1), _layer_spec(mod, l + 1, MOD_SCALE1),
                     _layer_spec(mod, l + 1, MOD_SHIFT1), _layer_spec(w_in, l + 1)]
        out_shape = (jax.ShapeDtypeStruct((t, d), F32), jax.ShapeDtypeStruct((t, n), BF16))
        out_specs = (row_tile(d), row_tile(n))
    return pl.pallas_call(
        body,
        out_shape=out_shape,
        grid=(t // COMBINE_ROWS,),
        in_specs=in_specs,
        out_specs=out_specs,
        scratch_shapes=[
            pltpu.SMEM((TOP_K * COMBINE_ROWS,), jnp.int32),
            pltpu.SMEM((TOP_K * COMBINE_ROWS,), jnp.int32),
            pltpu.VMEM((2, TOP_K * COMBINE_ROWS * SUBLANES, LANES), F32),
            pltpu.SemaphoreType.DMA((2,)),
            pltpu.SemaphoreType.DMA((2,)),
        ],
        compiler_params=pltpu.CompilerParams(
            dimension_semantics=("arbitrary",), vmem_limit_bytes=VMEM_LIMIT_BYTES),
        name="moe_combine",
    )(dest_tiles, ys, x, mod, wt, *operands)


def _ssm_params(lam_re, lam_im, log_step, b_re, b_im, c_re, c_im):
    step = jnp.exp(log_step)[:, None]
    mag = jnp.exp(lam_re * step)
    a_re, a_im = mag * jnp.cos(lam_im * step), mag * jnp.sin(lam_im * step)
    den = lam_re * lam_re + lam_im * lam_im
    n_re, n_im = a_re - 1.0, a_im
    k_re = (n_re * lam_re + n_im * lam_im) / den
    k_im = (n_im * lam_re - n_re * lam_im) / den
    bb_re = k_re[..., None] * b_re - k_im[..., None] * b_im
    bb_im = k_re[..., None] * b_im + k_im[..., None] * b_re
    gpc = SSM_GROUPS // SSM_CHUNKS
    eye = jnp.eye(gpc, dtype=F32)

    def in_blocks(bb):
        bb = bb.reshape(SSM_CHUNKS, gpc, SSM_STATE, SSM_GROUP)
        return jnp.einsum("cgph,gk->cghkp", bb, eye).reshape(SSM_CHUNKS, SSM_CHUNK_IN, SSM_CHUNK_STATE)

    def out_blocks(cc):
        cc = cc.reshape(SSM_CHUNKS, gpc, SSM_GROUP, SSM_STATE)
        return jnp.einsum("cghp,gk->cgpkh", cc, eye).reshape(SSM_CHUNKS, SSM_CHUNK_STATE, SSM_CHUNK_IN)

    wb = jnp.concatenate([in_blocks(bb_re), in_blocks(bb_im)], axis=2).astype(BF16)
    wc = jnp.concatenate([out_blocks(c_re), out_blocks(-c_im)], axis=1).astype(BF16)
    bcast = lambda a: jnp.broadcast_to(a.reshape(SSM_CHUNKS, 1, SSM_CHUNK_STATE),
                                       (SSM_CHUNKS, BATCH, SSM_CHUNK_STATE))
    return wb, wc, bcast(a_re), bcast(a_im)


def _rows8(w):
    depth, k, c = w.shape
    return jnp.broadcast_to(w[:, :, None, :], (depth, k, SUBLANES, c)).reshape(depth, k * SUBLANES, c)


def _router_weights(w_rg, b_rg, w_re, b_re):
    depth, d, _ = w_rg.shape
    pad = SUBLANES - N_GROUPS
    tail = LANES - ROUTER_ROWS
    w_cols = jnp.concatenate([w_rg, jnp.zeros((depth, d, pad), F32), w_re,
                              jnp.zeros((depth, d, tail), F32)], axis=2)
    b = jnp.concatenate([b_rg, jnp.zeros((depth, pad), F32), b_re, jnp.zeros((depth, tail), F32)], axis=1)
    hi, lo = _split_bf16(w_cols)
    return hi, lo, b.reshape(depth, 1, LANES)


def _routing_tables(route, counts):
    t = route.shape[1]
    counts = counts.astype(jnp.int32)
    padded = (counts + EXPERT_ROWS - 1) // EXPERT_ROWS * EXPERT_ROWS
    pad_end = jnp.cumsum(padded)
    pad_start = pad_end - padded
    experts = route[0:TOP_K].astype(jnp.int32)
    ranks = route[TOP_K:2 * TOP_K].astype(jnp.int32)
    expert_ids = jnp.arange(N_EXPERTS, dtype=jnp.int32)
    is_expert = experts[:, :, None] == expert_ids
    dest = jnp.sum(jnp.where(is_expert, pad_start, 0), axis=-1) + ranks

    def tiles(rows_per_tile):
        n = t // rows_per_tile
        return dest.reshape(TOP_K, n, rows_per_tile).transpose(1, 0, 2).reshape(-1)

    capacity = TOP_K * t + N_EXPERTS * EXPERT_ROWS
    first_block = pad_start // EXPERT_ROWS
    n_blocks = padded // EXPERT_ROWS
    last_start = pad_end - EXPERT_ROWS
    last_valid = counts - (padded - EXPERT_ROWS)
    piece = jnp.arange(EXPERT_ROWS // ZERO_ROWS, dtype=jnp.int32)
    has_pad = (counts[:, None] > 0) & ((piece[None, :] + 1) * ZERO_ROWS > last_valid[:, None])
    zstart = jnp.where(has_pad, last_start[:, None] + piece[None, :] * ZERO_ROWS, -1).reshape(-1)
    total_blocks = (pad_end[-1] // EXPERT_ROWS).reshape(1)
    return (tiles(DISPATCH_ROWS), tiles(COMBINE_ROWS), first_block.astype(jnp.int32), n_blocks.astype(jnp.int32),
            total_blocks.astype(jnp.int32), zstart.astype(jnp.int32), capacity)


def kernel(x, c, norm_mix_g, norm_ffn_g, w_ada, b_ada, w_in, lam_re, lam_im, log_step, ssm_b_re, ssm_b_im, ssm_c_re, ssm_c_im, ssm_d, w_glu, conf_dw_w, conf_dw_b, conf_ln_g, conf_ln_b, sconv_w, w_branch, w_gate, b_gate, w_out, w_router_group, b_router_group, w_router_expert, b_router_expert, w_exp_gate, w_exp_up, w_exp_down, final_norm_g):
    bsz, seq, d = x.shape
    assert bsz == BATCH and d == D_MODEL
    depth = w_in.shape[0]
    t = bsz * seq
    rows = lambda v: v.reshape(depth, 1, -1)

    mix_g, ffn_g = rows(norm_mix_g), rows(norm_ffn_g)
    seq_params = (*jax.vmap(_ssm_params)(lam_re, lam_im, log_step, ssm_b_re, ssm_b_im, ssm_c_re, ssm_c_im),
                  rows(ssm_d), w_glu.astype(BF16), _rows8(conf_dw_w), rows(conf_dw_b), rows(conf_ln_g),
                  rows(conf_ln_b), _rows8(sconv_w))
    w_in_b, w_gate_b, w_branch_b, w_out_b = (w.astype(BF16) for w in (w_in, w_gate, w_branch, w_out))
    r_hi, r_lo, r_b = _router_weights(w_router_group, b_router_group, w_router_expert, b_router_expert)
    tri = jnp.triu(jnp.ones((ROUTE_ROWS, ROUTE_ROWS), BF16))

    mod = _ada(c, w_ada, b_ada)
    xt, proj = _inproj_first(x, mix_g, mod, w_in_b)
    for l in range(depth):
        ycat = _seq_mixers(l, proj, *seq_params)
        xt, h2, route, route_t, counts = _merge_router(l, xt, mix_g, ffn_g, mod, ycat, w_gate_b, rows(b_gate),
                                                       w_branch_b, w_out_b, r_hi, r_lo, r_b, tri)
        (dest_d, dest_c, first_block, n_blocks, total_blocks, zstart,
         cap) = _routing_tables(route, counts[:, 0])
        xs = _dispatch(zstart, dest_d, h2, cap)
        ys = _experts(l, first_block, n_blocks, total_blocks, xs, w_exp_gate, w_exp_up, w_exp_down)
        if l + 1 < depth:
            xt, proj = _combine(l, dest_c, ys, xt, mod, route_t, next_layer=(mix_g, w_in_b))
        else:
            return _combine(l, dest_c, ys, xt, mod, route_t, final_g=final_norm_g.reshape(1, d))
```

```python
import functools
import math

import jax
import jax.numpy as jnp
from jax import lax
from jax.experimental import pallas as pl
from jax.experimental.pallas import tpu as pltpu

F32 = jnp.float32
BF16 = jnp.bfloat16

D_MODEL = 1024
BATCH = 8
BRANCH_WIDTH = D_MODEL // 2
N_BRANCH = 3
N_IN_CHUNKS = 6
SSM_GROUP = 16
SSM_GROUPS = BRANCH_WIDTH // SSM_GROUP
SSM_STATE = 64
CONF_KERNEL = 31
SCONV_KERNEL = 3
N_GROUPS = 4
EXPERTS_PER_GROUP = 8
N_EXPERTS = N_GROUPS * EXPERTS_PER_GROUP
TOP_K = 2
D_FF_EXPERT = D_MODEL // 2
NORM_EPS = 1e-6

SUBLANES = 8
LANES = 128
VMEM_LIMIT_BYTES = 56 * 1024 * 1024

SSM_CHUNKS = 2
SSM_CHUNK_IN = BRANCH_WIDTH // SSM_CHUNKS
SSM_CHUNK_STATE = SSM_GROUPS * SSM_STATE // SSM_CHUNKS
CONF_HIST = (CONF_KERNEL - 1) * BATCH
SCONV_HIST = (SCONV_KERNEL - 1) * BATCH
ROUTER_ROWS = 40
MOD_SHIFT1, MOD_SCALE1, MOD_GATE1, MOD_SHIFT2, MOD_SCALE2, MOD_GATE2 = range(6)

ADA_COLS = 1536
INPROJ_ROWS = 1024
SEQ_ROWS = 512
CONV_ROWS = 64
EW_ROWS = 32
MERGE_ROWS = 512
ROUTE_ROWS = MERGE_ROWS
DISPATCH_ROWS = 512
DISPATCH_SLOTS = 4
DISPATCH_LOOKAHEAD = 2
EXPERT_ROWS = 512
IN_LOOKAHEAD = 3
OUT_SLOTS = 3
ZERO_ROWS = 64
COMBINE_ROWS = 512


def _dot(a, b):
    return jnp.dot(a, b, preferred_element_type=F32)


def _split_bf16(v):
    hi = v.astype(BF16)
    lo = (v - hi.astype(F32)).astype(BF16)
    return hi, lo


def _sigmoid(v):
    return 1.0 / (1.0 + jnp.exp(-v))


def _gelu_tanh(v):
    return 0.5 * v * (1.0 + jnp.tanh(math.sqrt(2.0 / math.pi) * (v + 0.044715 * (v * v * v))))


def _per_batch(v, fn):
    rows, d = v.shape
    return fn(v.reshape(rows // BATCH, BATCH, d)).reshape(rows, d)


def _load_token_tiles(ref, first_token, n_tokens):
    base = first_token * SUBLANES
    chunks = [ref[pl.ds(base + s, n_tokens, stride=SUBLANES), :] for s in range(SUBLANES)]
    return jnp.concatenate(chunks, axis=1)


def _store_token_tiles(ref, value):
    for s in range(SUBLANES):
        ref[pl.ds(s, value.shape[0], stride=SUBLANES), :] = value[:, s * LANES:(s + 1) * LANES]


def _token_tile(ref, token):
    return ref.at[pl.ds(pl.multiple_of(token * SUBLANES, SUBLANES), SUBLANES)]


def _layer_spec(arr, l, col_block=None):
    tail = arr.shape[1:]
    if col_block is None:
        zeros = (0,) * len(tail)
        return pl.BlockSpec((None,) + tail, lambda *_: (l,) + zeros)
    return pl.BlockSpec((None, tail[0], D_MODEL), lambda *_: (l, 0, col_block))


def _norm_modulate(x, g, scale, shift):
    ms = jnp.mean(x * x, axis=-1, keepdims=True)
    h = x * lax.rsqrt(ms + NORM_EPS) * g
    return _per_batch(h, lambda h3: h3 * (1.0 + scale) + shift)


def _ada_kernel(c_ref, w_ref, b_ref, o_ref):
    c = c_ref[...]
    cond = c * _sigmoid(c)
    chi, clo = _split_bf16(cond)
    whi, wlo = _split_bf16(w_ref[...])
    o_ref[...] = _dot(chi, whi) + _dot(chi, wlo) + _dot(clo, whi) + b_ref[...]


def _ada(c, w_ada, b_ada):
    depth, d, n = w_ada.shape
    return pl.pallas_call(
        _ada_kernel,
        out_shape=jax.ShapeDtypeStruct((depth, BATCH, n), F32),
        grid=(depth, n // ADA_COLS),
        in_specs=[
            pl.BlockSpec((BATCH, d), lambda l, j: (0, 0)),
            pl.BlockSpec((None, d, ADA_COLS), lambda l, j: (l, 0, j)),
            pl.BlockSpec((None, 1, ADA_COLS), lambda l, j: (l, 0, j)),
        ],
        out_specs=pl.BlockSpec((None, BATCH, ADA_COLS), lambda l, j: (l, 0, j)),
        compiler_params=pltpu.CompilerParams(
            dimension_semantics=("arbitrary", "arbitrary"), vmem_limit_bytes=VMEM_LIMIT_BYTES),
        name="ada_modulation",
    )(c, w_ada, b_ada.reshape(depth, 1, n))


def _inproj_kernel(x_ref, g_ref, sc_ref, sh_ref, w_ref, xt_ref, o_ref):
    bsz, steps, d = x_ref.shape
    x = pltpu.einshape("bsd->sbd", x_ref[...]).reshape(steps * bsz, d)
    xt_ref[...] = x
    h = _norm_modulate(x, g_ref[...], sc_ref[...], sh_ref[...])
    o_ref[...] = _dot(h.astype(BF16), w_ref[...]).astype(o_ref.dtype)


def _inproj_first(x, g, mod, w_in):
    bsz, seq, d = x.shape
    n = w_in.shape[2]
    steps = INPROJ_ROWS // bsz
    return pl.pallas_call(
        _inproj_kernel,
        out_shape=(jax.ShapeDtypeStruct((bsz * seq, d), F32), jax.ShapeDtypeStruct((bsz * seq, n), BF16)),
        grid=(seq // steps,),
        in_specs=[
            pl.BlockSpec((bsz, steps, d), lambda i: (0, i, 0)),
            _layer_spec(g, 0),
            _layer_spec(mod, 0, MOD_SCALE1),
            _layer_spec(mod, 0, MOD_SHIFT1),
            _layer_spec(w_in, 0),
        ],
        out_specs=(pl.BlockSpec((INPROJ_ROWS, d), lambda i: (i, 0)),
                   pl.BlockSpec((INPROJ_ROWS, n), lambda i: (i, 0))),
        compiler_params=pltpu.CompilerParams(
            dimension_semantics=("arbitrary",), vmem_limit_bytes=VMEM_LIMIT_BYTES),
        name="mixer_inproj",
    )(x, g, mod, mod, w_in)


def _seq_kernel(p_ref, wb_ref, wc_ref, are_ref, aim_ref, d_ref, wglu_ref, cw_ref, cb_ref,
                lng_ref, lnb_ref, sw_ref, o_ref, bu_ref, st_ref, zbuf_ref, cbuf_ref, qbuf_ref):
    rows = p_ref.shape[0]
    steps = rows // BATCH
    w = BRANCH_WIDTH

    @pl.when(pl.program_id(0) == 0)
    def _():
        st_ref[...] = jnp.zeros_like(st_ref)
        zbuf_ref[0:CONF_HIST, :] = jnp.zeros((CONF_HIST, w), F32)
        qbuf_ref[0:SCONV_HIST, :] = jnp.zeros((SCONV_HIST, w), F32)

    chunk_groups = EW_ROWS // SUBLANES

    def pointwise_chunk(j):
        r0 = j * EW_ROWS
        rs = slice(r0, r0 + EW_ROWS)
        gate_c = p_ref[rs, 4 * w:5 * w].astype(F32)
        hv = p_ref[rs, 5 * w:6 * w].astype(F32)
        qbuf_ref[r0 + SCONV_HIST:r0 + SCONV_HIST + EW_ROWS, :] = gate_c * hv
        q3 = qbuf_ref[r0:r0 + EW_ROWS + SCONV_HIST, :].reshape(
            chunk_groups + SCONV_KERNEL - 1, SUBLANES, w)
        acc = q3[0:chunk_groups] * sw_ref[0:SUBLANES, :]
        for k in range(1, SCONV_KERNEL):
            acc = acc + q3[k:k + chunk_groups] * sw_ref[k * SUBLANES:(k + 1) * SUBLANES, :]
        gate_b = p_ref[rs, 3 * w:4 * w].astype(F32)
        o_ref[rs, 2 * w:3 * w] = (gate_b * acc.reshape(EW_ROWS, w)).astype(o_ref.dtype)
        v = p_ref[rs, w:2 * w].astype(F32)
        g = p_ref[rs, 2 * w:3 * w].astype(F32)
        zbuf_ref[r0 + CONF_HIST:r0 + CONF_HIST + EW_ROWS, :] = v * _sigmoid(g)

    def input_matmul(c):
        u_c = p_ref[:, c * SSM_CHUNK_IN:(c + 1) * SSM_CHUNK_IN]
        bu_ref[c] = _dot(u_c, wb_ref[c])

    def scan(c):
        a_re = are_ref[c]
        a_im = aim_ref[c]
        x_re = st_ref[c, 0]
        x_im = st_ref[c, 1]
        for t in range(steps):
            rs = slice(t * BATCH, (t + 1) * BATCH)
            n_re = a_re * x_re - a_im * x_im + bu_ref[c, rs, 0:SSM_CHUNK_STATE]
            n_im = a_re * x_im + a_im * x_re + bu_ref[c, rs, SSM_CHUNK_STATE:]
            bu_ref[c, rs, 0:SSM_CHUNK_STATE] = n_re
            bu_ref[c, rs, SSM_CHUNK_STATE:] = n_im
            x_re, x_im = n_re, n_im
        st_ref[c, 0] = x_re
        st_ref[c, 1] = x_im

    def output_matmul(c):
        return _dot(bu_ref[c].astype(BF16), wc_ref[c])

    input_matmul(0)
    for j in range(rows // EW_ROWS):
        pointwise_chunk(j)
    qbuf_ref[0:SCONV_HIST, :] = qbuf_ref[rows:rows + SCONV_HIST, :]
    input_matmul(1)
    scan(0)
    y0 = output_matmul(0)
    scan(1)
    ys = [y0, output_matmul(1)]
    u = p_ref[:, 0:w].astype(F32)
    y = _gelu_tanh(jnp.concatenate(ys, axis=1) + d_ref[...] * u)
    y_ssm = y * _sigmoid(_dot(y.astype(BF16), wglu_ref[...]))
    o_ref[:, 0:w] = y_ssm.astype(o_ref.dtype)

    win_groups = (CONV_ROWS + CONF_HIST) // SUBLANES
    out_groups = CONV_ROWS // SUBLANES

    def conv_chunk(j, carry):
        r0 = pl.multiple_of(j * CONV_ROWS, CONV_ROWS)
        for s in range(w // LANES):
            ls = slice(s * LANES, (s + 1) * LANES)
            win = zbuf_ref[pl.ds(r0, CONV_ROWS + CONF_HIST), ls].reshape(win_groups, SUBLANES, LANES)
            acc = win[0:out_groups] * cw_ref[0:SUBLANES, ls]
            for k in range(1, CONF_KERNEL):
                acc = acc + win[k:k + out_groups] * cw_ref[k * SUBLANES:(k + 1) * SUBLANES, ls]
            cbuf_ref[pl.ds(r0, CONV_ROWS), ls] = acc.reshape(CONV_ROWS, LANES)
        return carry

    lax.fori_loop(0, rows // CONV_ROWS, conv_chunk, 0)
    zbuf_ref[0:CONF_HIST, :] = zbuf_ref[rows:rows + CONF_HIST, :]
    cv = cbuf_ref[...] + cb_ref[...]
    cc = cv - jnp.mean(cv, axis=-1, keepdims=True)
    ln = cc * lax.rsqrt(jnp.mean(cc * cc, axis=-1, keepdims=True) + NORM_EPS) * lng_ref[...] + lnb_ref[...]
    o_ref[:, w:2 * w] = (ln * _sigmoid(ln)).astype(o_ref.dtype)


def _seq_mixers(l, proj, *layer_params):
    t = proj.shape[0]
    w = BRANCH_WIDTH
    return pl.pallas_call(
        _seq_kernel,
        out_shape=jax.ShapeDtypeStruct((t, N_BRANCH * w), BF16),
        grid=(t // SEQ_ROWS,),
        in_specs=[pl.BlockSpec((SEQ_ROWS, N_IN_CHUNKS * w), lambda i: (i, 0))]
                 + [_layer_spec(p, l) for p in layer_params],
        out_specs=pl.BlockSpec((SEQ_ROWS, N_BRANCH * w), lambda i: (i, 0)),
        scratch_shapes=[
            pltpu.VMEM((SSM_CHUNKS, SEQ_ROWS, 2 * SSM_CHUNK_STATE), F32),
            pltpu.VMEM((SSM_CHUNKS, 2, BATCH, SSM_CHUNK_STATE), F32),
            pltpu.VMEM((SEQ_ROWS + CONF_HIST, w), F32),
            pltpu.VMEM((SEQ_ROWS, w), F32),
            pltpu.VMEM((SEQ_ROWS + SCONV_HIST, w), F32),
        ],
        compiler_params=pltpu.CompilerParams(
            dimension_semantics=("arbitrary",), vmem_limit_bytes=VMEM_LIMIT_BYTES),
        name="seq_mixers",
    )(proj, *layer_params)


def _merge_router_kernel(x_ref, g_ref, sc_ref, sh_ref, g1_ref, y_ref, wg_ref, bg_ref, wbr_ref, wo_ref,
                         g2_ref, sc2_ref, sh2_ref, whi_ref, wlo_ref, b_ref, tri_ref,
                         o_ref, h_ref, route_ref, route_t_ref, cnt_ref, run_ref):
    x = x_ref[...]
    d = x.shape[1]
    w = BRANCH_WIDTH
    hb = _norm_modulate(x, g_ref[...], sc_ref[...], sh_ref[...]).astype(BF16)
    merged = None
    for n in range(N_BRANCH):
        gate = _sigmoid(_dot(hb, wg_ref[:, n * d:(n + 1) * d]) + bg_ref[:, n * d:(n + 1) * d])
        term = gate * _dot(y_ref[:, n * w:(n + 1) * w], wbr_ref[n])
        merged = term if merged is None else merged + term
    out = _dot(merged.astype(BF16), wo_ref[...])
    x_new = x + _per_batch(out, lambda o3: o3 * g1_ref[...])
    o_ref[...] = x_new
    _route_tile(x_new, g2_ref, sc2_ref, sh2_ref, whi_ref, wlo_ref, b_ref, tri_ref,
                h_ref, route_ref, route_t_ref, cnt_ref, run_ref)


def _merge_router(l, x, mix_g, ffn_g, mod, ycat, w_gate, b_gate, w_branch, w_out, r_hi, r_lo, r_bias, tri):
    t, d = x.shape
    w = BRANCH_WIDTH
    row_tile = lambda cols: pl.BlockSpec((MERGE_ROWS, cols), lambda i: (i, 0))
    return pl.pallas_call(
        _merge_router_kernel,
        out_shape=(
            jax.ShapeDtypeStruct((t, d), F32),
            jax.ShapeDtypeStruct((t * SUBLANES, LANES), F32),
            jax.ShapeDtypeStruct((SUBLANES, t), F32),
            jax.ShapeDtypeStruct((t, SUBLANES), F32),
            jax.ShapeDtypeStruct((N_EXPERTS, MERGE_ROWS), F32),
        ),
        grid=(t // MERGE_ROWS,),
        in_specs=[
            row_tile(d),
            _layer_spec(mix_g, l),
            _layer_spec(mod, l, MOD_SCALE1),
            _layer_spec(mod, l, MOD_SHIFT1),
            _layer_spec(mod, l, MOD_GATE1),
            row_tile(N_BRANCH * w),
            _layer_spec(w_gate, l),
            _layer_spec(b_gate, l),
            _layer_spec(w_branch, l),
            _layer_spec(w_out, l),
            _layer_spec(ffn_g, l),
            _layer_spec(mod, l, MOD_SCALE2),
            _layer_spec(mod, l, MOD_SHIFT2),
            _layer_spec(r_hi, l),
            _layer_spec(r_lo, l),
            _layer_spec(r_bias, l),
            pl.BlockSpec((MERGE_ROWS, MERGE_ROWS), lambda i: (0, 0)),
        ],
        out_specs=(
            row_tile(d),
            pl.BlockSpec((MERGE_ROWS * SUBLANES, LANES), lambda i: (i, 0)),
            pl.BlockSpec((SUBLANES, MERGE_ROWS), lambda i: (0, i)),
            row_tile(SUBLANES),
            pl.BlockSpec((N_EXPERTS, MERGE_ROWS), lambda i: (0, 0)),
        ),
        scratch_shapes=[pltpu.VMEM((N_EXPERTS, MERGE_ROWS), F32)],
        compiler_params=pltpu.CompilerParams(
            dimension_semantics=("arbitrary",), vmem_limit_bytes=VMEM_LIMIT_BYTES),
        name="mixer_merge_router",
    )(x, mix_g, mod, mod, mod, ycat, w_gate, b_gate, w_branch, w_out, ffn_g, mod, mod, r_hi, r_lo, r_bias, tri)


def _route_tile(x, g_ref, sc_ref, sh_ref, whi_ref, wlo_ref, b_ref, tri_ref,
                h_ref, route_ref, route_t_ref, cnt_ref, run_ref):
    @pl.when(pl.program_id(0) == 0)
    def _():
        run_ref[...] = jnp.zeros_like(run_ref)

    h = _norm_modulate(x, g_ref[...], sc_ref[...], sh_ref[...])
    _store_token_tiles(h_ref, h)
    rows = h.shape[0]
    hhi, hlo = _split_bf16(h)
    logits = (_dot(hhi, whi_ref[...]) + _dot(hhi, wlo_ref[...]) + _dot(hlo, whi_ref[...]) + b_ref[...]).T
    lg = logits[0:N_GROUPS]
    le = logits[SUBLANES:SUBLANES + N_EXPERTS]

    g_row = lax.broadcasted_iota(jnp.int32, (N_GROUPS, rows), 0).astype(F32)
    g_max = jnp.max(lg, axis=0, keepdims=True)
    g_sel = jnp.min(jnp.where(lg == g_max, g_row, float(N_GROUPS)), axis=0, keepdims=True)
    g_w = 1.0 / jnp.sum(jnp.exp(lg - g_max), axis=0, keepdims=True)

    e_row = lax.broadcasted_iota(jnp.int32, (N_EXPERTS, rows), 0).astype(F32)
    e_grp = jnp.floor(e_row * (1.0 / EXPERTS_PER_GROUP))
    neg = float("-inf")
    lm = jnp.where(e_grp == g_sel, le, neg)
    v1 = jnp.max(lm, axis=0, keepdims=True)
    i1 = jnp.min(jnp.where(lm == v1, e_row, float(N_EXPERTS)), axis=0, keepdims=True)
    lm2 = jnp.where(e_row == i1, neg, lm)
    v2 = jnp.max(lm2, axis=0, keepdims=True)
    i2 = jnp.min(jnp.where(lm2 == v2, e_row, float(N_EXPERTS)), axis=0, keepdims=True)
    e21 = jnp.exp(v2 - v1)
    p1 = 1.0 / (1.0 + e21)
    w1 = p1 * g_w
    w2 = e21 * p1 * g_w

    hit1 = e_row == i1
    hit2 = e_row == i2
    onehot = jnp.where(hit1 | hit2, 1.0, 0.0)
    cum = _dot(onehot.astype(BF16), tri_ref[...])
    before = cum - onehot + run_ref[...]
    r1 = jnp.sum(jnp.where(hit1, before, 0.0), axis=0, keepdims=True)
    r2 = jnp.sum(jnp.where(hit2, before, 0.0), axis=0, keepdims=True)
    run_ref[...] = run_ref[...] + jnp.broadcast_to(cum[:, rows - 1:rows], run_ref.shape)
    cnt_ref[...] = run_ref[...]
    zero = jnp.zeros_like(w1)
    route = jnp.concatenate([i1, i2, r1, r2, w1, w2, zero, zero], axis=0)
    route_ref[...] = route
    route_t_ref[...] = route.T


def _row_copy(src_hbm, src_token, dst_hbm, dst_token, sem):
    return pltpu.make_async_copy(_token_tile(src_hbm, src_token), _token_tile(dst_hbm, dst_token), sem)


def _dispatch_kernel(zstart_ref, dest_hbm, h_hbm, xs_hbm, idx0_smem, idx1_smem, idx2_smem, idx3_smem,
                     hbuf_ref, zeros_ref, idx_sem, in_sem, zero_sem, row_sem):
    i = pl.program_id(0)
    n_tiles = pl.num_programs(0)
    idx_smem = (idx0_smem, idx1_smem, idx2_smem, idx3_smem)
    n_idx = idx0_smem.shape[0]
    tile = n_idx // TOP_K
    tile_rows = tile * SUBLANES

    def idx_copy(tile_index, s):
        src = dest_hbm.at[pl.ds(pl.multiple_of(tile_index * n_idx, n_idx), n_idx)]
        return pltpu.make_async_copy(src, idx_smem[s], idx_sem.at[s])

    def in_copy(tile_index, s):
        src = h_hbm.at[pl.ds(pl.multiple_of(tile_index * tile_rows, tile_rows), tile_rows)]
        return pltpu.make_async_copy(src, hbuf_ref.at[s], in_sem.at[s])

    def wait_rows(s):
        for k in range(TOP_K):
            pltpu.make_async_copy(hbuf_ref.at[s], xs_hbm.at[pl.ds(0, tile_rows)], row_sem.at[s % 2]).wait()

    def zero_copy(j):
        piece_rows = ZERO_ROWS * SUBLANES
        start = pl.multiple_of(zstart_ref[j] * SUBLANES, piece_rows)
        return pltpu.make_async_copy(zeros_ref, xs_hbm.at[pl.ds(start, piece_rows)], zero_sem)

    @pl.when(i == 0)
    def _():
        for b in range(DISPATCH_LOOKAHEAD):
            @pl.when(b < n_tiles)
            def _():
                idx_copy(b, b).start()
                in_copy(b, b).start()

        zeros_ref[...] = jnp.zeros_like(zeros_ref)

        def zero_start(j, carry):
            @pl.when(zstart_ref[j] >= 0)
            def _():
                zero_copy(j).start()
            return carry

        def zero_wait(j, carry):
            @pl.when(zstart_ref[j] >= 0)
            def _():
                zero_copy(j).wait()
            return carry

        lax.fori_loop(0, zstart_ref.shape[0], zero_start, 0)
        lax.fori_loop(0, zstart_ref.shape[0], zero_wait, 0)

    def tile_step(s):
        ahead = (s + DISPATCH_LOOKAHEAD) % DISPATCH_SLOTS

        @pl.when(i + DISPATCH_LOOKAHEAD < n_tiles)
        def _():
            idx_copy(i + DISPATCH_LOOKAHEAD, ahead).start()
            in_copy(i + DISPATCH_LOOKAHEAD, ahead).start()

        idx_copy(i, s).wait()
        in_copy(i, s).wait()

        def start_rows(t, c):
            for k in range(TOP_K):
                _row_copy(hbuf_ref.at[s], t, xs_hbm, idx_smem[s][k * tile + t],
                          row_sem.at[s % 2]).start(priority=k)
            return c

        lax.fori_loop(0, tile, start_rows, 0, unroll=8)

        @pl.when(i >= 1)
        def _():
            wait_rows((s - 1) % DISPATCH_SLOTS)

        @pl.when(i == n_tiles - 1)
        def _():
            wait_rows(s)

    for s in range(DISPATCH_SLOTS):
        pl.when(lax.rem(i, DISPATCH_SLOTS) == s)(functools.partial(tile_step, s))


def _dispatch(zstart, dest_tiles, h, cap):
    t = h.shape[0] // SUBLANES
    any_spec = pl.BlockSpec(memory_space=pl.ANY)
    idx_buf = pltpu.SMEM((TOP_K * DISPATCH_ROWS,), jnp.int32)
    return pl.pallas_call(
        _dispatch_kernel,
        out_shape=jax.ShapeDtypeStruct((cap * SUBLANES, LANES), h.dtype),
        grid_spec=pltpu.PrefetchScalarGridSpec(
            num_scalar_prefetch=1,
            grid=(t // DISPATCH_ROWS,),
            in_specs=[any_spec, any_spec],
            out_specs=any_spec,
            scratch_shapes=[
                idx_buf, idx_buf, idx_buf, idx_buf,
                pltpu.VMEM((DISPATCH_SLOTS, DISPATCH_ROWS * SUBLANES, LANES), h.dtype),
                pltpu.VMEM((ZERO_ROWS * SUBLANES, LANES), h.dtype),
                pltpu.SemaphoreType.DMA((DISPATCH_SLOTS,)),
                pltpu.SemaphoreType.DMA((DISPATCH_SLOTS,)),
                pltpu.SemaphoreType.DMA,
                pltpu.SemaphoreType.DMA((2,)),
            ],
        ),
        compiler_params=pltpu.CompilerParams(
            dimension_semantics=("arbitrary",), has_side_effects=True),
        name="moe_dispatch",
    )(zstart, dest_tiles, h)


def _expert_kernel(first_ref, nblk_ref, total_ref, xs_hbm, w1_ref, w3_ref, w2_ref, ys_hbm,
                   w1b_ref, w3b_ref, w2b_ref, xbuf_ref, ybuf_ref, in_sem, out_sem):
    e = pl.program_id(0)
    n_blocks = nblk_ref[e]
    first = first_ref[e]
    total = total_ref[0]
    block_rows = EXPERT_ROWS * SUBLANES

    def rows_of(block):
        return pl.ds(pl.multiple_of(block * block_rows, block_rows), block_rows)

    def in_copy(block, slot):
        return pltpu.make_async_copy(xs_hbm.at[rows_of(block)], xbuf_ref.at[slot], in_sem.at[slot])

    def out_copy(block, slot):
        return pltpu.make_async_copy(ybuf_ref.at[slot], ys_hbm.at[rows_of(block)], out_sem.at[slot])

    @pl.when(n_blocks > 0)
    def _():
        @pl.when(first == 0)
        def _():
            for b in range(IN_LOOKAHEAD):
                @pl.when(b < total)
                def _():
                    in_copy(b, b).start(priority=1)

        w1b_ref[...] = w1_ref[...].astype(BF16)
        w3b_ref[...] = w3_ref[...].astype(BF16)
        w2b_ref[...] = w2_ref[...].astype(BF16)

        def block_step(j, carry):
            g = first + j
            in_slot = lax.rem(g, IN_LOOKAHEAD + 1)
            slot = lax.rem(g, OUT_SLOTS)

            @pl.when(g + IN_LOOKAHEAD < total)
            def _():
                in_copy(g + IN_LOOKAHEAD, lax.rem(g + IN_LOOKAHEAD, IN_LOOKAHEAD + 1)).start(priority=1)

            in_copy(g, in_slot).wait()

            @pl.when(g >= OUT_SLOTS)
            def _():
                out_copy(g - OUT_SLOTS, slot).wait()

            xb = _load_token_tiles(xbuf_ref.at[in_slot], 0, EXPERT_ROWS).astype(BF16)
            a = _dot(xb, w1b_ref[...])
            hid = a * _sigmoid(a) * _dot(xb, w3b_ref[...])
            _store_token_tiles(ybuf_ref.at[slot], _dot(hid.astype(BF16), w2b_ref[...]))
            out_copy(g, slot).start()
            return carry

        lax.fori_loop(0, n_blocks, block_step, 0)

        @pl.when(first + n_blocks == total)
        def _():
            for back in range(OUT_SLOTS, 0, -1):
                @pl.when(total >= back)
                def _():
                    out_copy(total - back, lax.rem(total - back, OUT_SLOTS)).wait()


def _experts(l, first_block, n_blocks, total_blocks, xs, w1, w3, w2):
    _, n_experts, d, f = w1.shape
    tile = (EXPERT_ROWS * SUBLANES, LANES)
    any_spec = pl.BlockSpec(memory_space=pl.ANY)
    weight_spec = lambda rows, cols: pl.BlockSpec((None, None, rows, cols), lambda e, *_: (l, e, 0, 0))
    return pl.pallas_call(
        _expert_kernel,
        out_shape=jax.ShapeDtypeStruct(xs.shape, F32),
        grid_spec=pltpu.PrefetchScalarGridSpec(
            num_scalar_prefetch=3,
            grid=(n_experts,),
            in_specs=[any_spec, weight_spec(d, f), weight_spec(d, f), weight_spec(f, d)],
            out_specs=any_spec,
            scratch_shapes=[
                pltpu.VMEM((d, f), BF16), pltpu.VMEM((d, f), BF16), pltpu.VMEM((f, d), BF16),
                pltpu.VMEM((IN_LOOKAHEAD + 1,) + tile, F32), pltpu.VMEM((OUT_SLOTS,) + tile, F32),
                pltpu.SemaphoreType.DMA((IN_LOOKAHEAD + 1,)), pltpu.SemaphoreType.DMA((OUT_SLOTS,)),
            ],
        ),
        compiler_params=pltpu.CompilerParams(
            dimension_semantics=("arbitrary",), vmem_limit_bytes=VMEM_LIMIT_BYTES),
        name="moe_experts",
    )(first_block, n_blocks, total_blocks, xs, w1, w3, w2)


def _gather_expert_rows(dest_hbm, ys_hbm, idx_smem, ybuf_ref, idx_sem, row_sem):
    i = pl.program_id(0)
    n_tiles = pl.num_programs(0)
    n_idx = idx_smem[0].shape[0]
    tile = n_idx // TOP_K

    def idx_copy(tile_index, s):
        src = dest_hbm.at[pl.ds(pl.multiple_of(tile_index * n_idx, n_idx), n_idx)]
        return pltpu.make_async_copy(src, idx_smem[s], idx_sem.at[s])

    def start_rows(s):
        def body(t, carry):
            for k in range(TOP_K):
                row = k * tile + t
                pltpu.make_async_copy(_token_tile(ys_hbm, idx_smem[s][row]),
                                      _token_tile(ybuf_ref.at[s], row), row_sem.at[s]).start(priority=k)
            return carry
        lax.fori_loop(0, tile, body, 0, unroll=8)

    def wait_rows(s):
        pltpu.make_async_copy(ybuf_ref.at[s], ybuf_ref.at[s], row_sem.at[s]).wait()

    @pl.when(i == 0)
    def _():
        idx_copy(0, 0).start()
        idx_copy(0, 0).wait()
        start_rows(0)

        @pl.when(n_tiles > 1)
        def _():
            idx_copy(1, 1).start()

    def step(slot):
        other = 1 - slot

        @pl.when(i + 1 < n_tiles)
        def _():
            idx_copy(i + 1, other).wait()
            start_rows(other)

        @pl.when(i + 2 < n_tiles)
        def _():
            idx_copy(i + 2, slot).start()

        wait_rows(slot)

    for slot in range(2):
        pl.when(lax.rem(i, 2) == slot)(functools.partial(step, slot))
    return lax.rem(i, 2)


def _combined_residual(slot, ybuf_ref, x_ref, g2_ref, wt_ref):
    tile = x_ref.shape[0]
    wt = wt_ref[...]
    ytiles = ybuf_ref.at[slot]
    y = (wt[:, 2 * TOP_K:2 * TOP_K + 1] * _load_token_tiles(ytiles, 0, tile)
         + wt[:, 2 * TOP_K + 1:2 * TOP_K + 2] * _load_token_tiles(ytiles, tile, tile))
    return x_ref[...] + _per_batch(y, lambda y3: y3 * g2_ref[...])


def _combine_final_kernel(dest_hbm, ys_hbm, x_ref, g2_ref, wt_ref, fg_ref, o_ref,
                          idx0_smem, idx1_smem, ybuf_ref, idx_sem, row_sem):
    slot = _gather_expert_rows(dest_hbm, ys_hbm, (idx0_smem, idx1_smem), ybuf_ref, idx_sem, row_sem)
    out = _combined_residual(slot, ybuf_ref, x_ref, g2_ref, wt_ref)
    ms = jnp.mean(out * out, axis=-1, keepdims=True)
    out = out * lax.rsqrt(ms + NORM_EPS) * fg_ref[...]
    bsz, steps, d = o_ref.shape
    o_ref[...] = pltpu.einshape("sbd->bsd", out.reshape(steps, bsz, d))


def _combine_inproj_kernel(dest_hbm, ys_hbm, x_ref, g2_ref, wt_ref, g_ref, sc_ref, sh_ref, w_ref,
                           xo_ref, p_ref, idx0_smem, idx1_smem, ybuf_ref, idx_sem, row_sem):
    slot = _gather_expert_rows(dest_hbm, ys_hbm, (idx0_smem, idx1_smem), ybuf_ref, idx_sem, row_sem)
    out = _combined_residual(slot, ybuf_ref, x_ref, g2_ref, wt_ref)
    xo_ref[...] = out
    h = _norm_modulate(out, g_ref[...], sc_ref[...], sh_ref[...])
    p_ref[...] = _dot(h.astype(BF16), w_ref[...]).astype(p_ref.dtype)


def _combine(l, dest_tiles, ys, x, mod, wt, final_g=None, next_layer=None):
    t, d = x.shape
    any_spec = pl.BlockSpec(memory_space=pl.ANY)
    row_tile = lambda cols: pl.BlockSpec((COMBINE_ROWS, cols), lambda i: (i, 0))
    in_specs = [any_spec, any_spec, row_tile(d), _layer_spec(mod, l, MOD_GATE2), row_tile(SUBLANES)]
    if next_layer is None:
        body, operands = _combine_final_kernel, (final_g,)
        in_specs += [pl.BlockSpec((1, d), lambda i: (0, 0))]
        out_shape = jax.ShapeDtypeStruct((BATCH, t // BATCH, d), F32)
        out_specs = pl.BlockSpec((BATCH, COMBINE_ROWS // BATCH, d), lambda i: (0, i, 0))
    else:
        mix_g, w_in = next_layer
        n = w_in.shape[2]
        body, operands = _combine_inproj_kernel, (mix_g, mod, mod, w_in)
        in_specs += [_layer_spec(mix_g, l + 1), _layer_spec(mod, l + 1, MOD_SCALE1),
                     _layer_spec(mod, l + 1, MOD_SHIFT1), _layer_spec(w_in, l + 1)]
        out_shape = (jax.ShapeDtypeStruct((t, d), F32), jax.ShapeDtypeStruct((t, n), BF16))
        out_specs = (row_tile(d), row_tile(n))
    return pl.pallas_call(
        body,
        out_shape=out_shape,
        grid=(t // COMBINE_ROWS,),
        in_specs=in_specs,
        out_specs=out_specs,
        scratch_shapes=[
            pltpu.SMEM((TOP_K * COMBINE_ROWS,), jnp.int32),
            pltpu.SMEM((TOP_K * COMBINE_ROWS,), jnp.int32),
            pltpu.VMEM((2, TOP_K * COMBINE_ROWS * SUBLANES, LANES), F32),
            pltpu.SemaphoreType.DMA((2,)),
            pltpu.SemaphoreType.DMA((2,)),
        ],
        compiler_params=pltpu.CompilerParams(
            dimension_semantics=("arbitrary",), vmem_limit_bytes=VMEM_LIMIT_BYTES),
        name="moe_combine",
    )(dest_tiles, ys, x, mod, wt, *operands)


def _ssm_params(lam_re, lam_im, log_step, b_re, b_im, c_re, c_im):
    step = jnp.exp(log_step)[:, None]
    mag = jnp.exp(lam_re * step)
    a_re, a_im = mag * jnp.cos(lam_im * step), mag * jnp.sin(lam_im * step)
    den = lam_re * lam_re + lam_im * lam_im
    n_re, n_im = a_re - 1.0, a_im
    k_re = (n_re * lam_re + n_im * lam_im) / den
    k_im = (n_im * lam_re - n_re * lam_im) / den
    bb_re = k_re[..., None] * b_re - k_im[..., None] * b_im
    bb_im = k_re[..., None] * b_im + k_im[..., None] * b_re
    gpc = SSM_GROUPS // SSM_CHUNKS
    eye = jnp.eye(gpc, dtype=F32)

    def in_blocks(bb):
        bb = bb.reshape(SSM_CHUNKS, gpc, SSM_STATE, SSM_GROUP)
        return jnp.einsum("cgph,gk->cghkp", bb, eye).reshape(SSM_CHUNKS, SSM_CHUNK_IN, SSM_CHUNK_STATE)

    def out_blocks(cc):
        cc = cc.reshape(SSM_CHUNKS, gpc, SSM_GROUP, SSM_STATE)
        return jnp.einsum("cghp,gk->cgpkh", cc, eye).reshape(SSM_CHUNKS, SSM_CHUNK_STATE, SSM_CHUNK_IN)

    wb = jnp.concatenate([in_blocks(bb_re), in_blocks(bb_im)], axis=2).astype(BF16)
    wc = jnp.concatenate([out_blocks(c_re), out_blocks(-c_im)], axis=1).astype(BF16)
    bcast = lambda a: jnp.broadcast_to(a.reshape(SSM_CHUNKS, 1, SSM_CHUNK_STATE),
                                       (SSM_CHUNKS, BATCH, SSM_CHUNK_STATE))
    return wb, wc, bcast(a_re), bcast(a_im)


def _rows8(w):
    depth, k, c = w.shape
    return jnp.broadcast_to(w[:, :, None, :], (depth, k, SUBLANES, c)).reshape(depth, k * SUBLANES, c)


def _router_weights(w_rg, b_rg, w_re, b_re):
    depth, d, _ = w_rg.shape
    pad = SUBLANES - N_GROUPS
    tail = LANES - ROUTER_ROWS
    w_cols = jnp.concatenate([w_rg, jnp.zeros((depth, d, pad), F32), w_re,
                              jnp.zeros((depth, d, tail), F32)], axis=2)
    b = jnp.concatenate([b_rg, jnp.zeros((depth, pad), F32), b_re, jnp.zeros((depth, tail), F32)], axis=1)
    hi, lo = _split_bf16(w_cols)
    return hi, lo, b.reshape(depth, 1, LANES)


def _routing_tables(route, counts):
    t = route.shape[1]
    counts = counts.astype(jnp.int32)
    padded = (counts + EXPERT_ROWS - 1) // EXPERT_ROWS * EXPERT_ROWS
    pad_end = jnp.cumsum(padded)
    pad_start = pad_end - padded
    experts = route[0:TOP_K].astype(jnp.int32)
    ranks = route[TOP_K:2 * TOP_K].astype(jnp.int32)
    expert_ids = jnp.arange(N_EXPERTS, dtype=jnp.int32)
    is_expert = experts[:, :, None] == expert_ids
    dest = jnp.sum(jnp.where(is_expert, pad_start, 0), axis=-1) + ranks

    def tiles(rows_per_tile):
        n = t // rows_per_tile
        return dest.reshape(TOP_K, n, rows_per_tile).transpose(1, 0, 2).reshape(-1)

    capacity = TOP_K * t + N_EXPERTS * EXPERT_ROWS
    first_block = pad_start // EXPERT_ROWS
    n_blocks = padded // EXPERT_ROWS
    last_start = pad_end - EXPERT_ROWS
    last_valid = counts - (padded - EXPERT_ROWS)
    piece = jnp.arange(EXPERT_ROWS // ZERO_ROWS, dtype=jnp.int32)
    has_pad = (counts[:, None] > 0) & ((piece[None, :] + 1) * ZERO_ROWS > last_valid[:, None])
    zstart = jnp.where(has_pad, last_start[:, None] + piece[None, :] * ZERO_ROWS, -1).reshape(-1)
    total_blocks = (pad_end[-1] // EXPERT_ROWS).reshape(1)
    return (tiles(DISPATCH_ROWS), tiles(COMBINE_ROWS), first_block.astype(jnp.int32), n_blocks.astype(jnp.int32),
            total_blocks.astype(jnp.int32), zstart.astype(jnp.int32), capacity)


def kernel(x, c, norm_mix_g, norm_ffn_g, w_ada, b_ada, w_in, lam_re, lam_im, log_step, ssm_b_re, ssm_b_im, ssm_c_re, ssm_c_im, ssm_d, w_glu, conf_dw_w, conf_dw_b, conf_ln_g, conf_ln_b, sconv_w, w_branch, w_gate, b_gate, w_out, w_router_group, b_router_group, w_router_expert, b_router_expert, w_exp_gate, w_exp_up, w_exp_down, final_norm_g):
    bsz, seq, d = x.shape
    assert bsz == BATCH and d == D_MODEL
    depth = w_in.shape[0]
    t = bsz * seq
    rows = lambda v: v.reshape(depth, 1, -1)

    mix_g, ffn_g = rows(norm_mix_g), rows(norm_ffn_g)
    seq_params = (*jax.vmap(_ssm_params)(lam_re, lam_im, log_step, ssm_b_re, ssm_b_im, ssm_c_re, ssm_c_im),
                  rows(ssm_d), w_glu.astype(BF16), _rows8(conf_dw_w), rows(conf_dw_b), rows(conf_ln_g),
                  rows(conf_ln_b), _rows8(sconv_w))
    w_in_b, w_gate_b, w_branch_b, w_out_b = (w.astype(BF16) for w in (w_in, w_gate, w_branch, w_out))
    r_hi, r_lo, r_b = _router_weights(w_router_group, b_router_group, w_router_expert, b_router_expert)
    tri = jnp.triu(jnp.ones((ROUTE_ROWS, ROUTE_ROWS), BF16))

    mod = _ada(c, w_ada, b_ada)
    xt, proj = _inproj_first(x, mix_g, mod, w_in_b)
    for l in range(depth):
        ycat = _seq_mixers(l, proj, *seq_params)
        xt, h2, route, route_t, counts = _merge_router(l, xt, mix_g, ffn_g, mod, ycat, w_gate_b, rows(b_gate),
                                                       w_branch_b, w_out_b, r_hi, r_lo, r_b, tri)
        (dest_d, dest_c, first_block, n_blocks, total_blocks, zstart,
         cap) = _routing_tables(route, counts[:, 0])
        xs = _dispatch(zstart, dest_d, h2, cap)
        ys = _experts(l, first_block, n_blocks, total_blocks, xs, w_exp_gate, w_exp_up, w_exp_down)
        if l + 1 < depth:
            xt, proj = _combine(l, dest_c, ys, xt, mod, route_t, next_layer=(mix_g, w_in_b))
        else:
            return _combine(l, dest_c, ys, xt, mod, route_t, final_g=final_norm_g.reshape(1, d))
```

```python
import functools
import math

import jax
import jax.numpy as jnp
from jax import lax
from jax.experimental import pallas as pl
from jax.experimental.pallas import tpu as pltpu

F32 = jnp.float32
BF16 = jnp.bfloat16

D_MODEL = 1024
BATCH = 8
BRANCH_WIDTH = D_MODEL // 2
N_BRANCH = 3
N_IN_CHUNKS = 6
SSM_GROUP = 16
SSM_GROUPS = BRANCH_WIDTH // SSM_GROUP
SSM_STATE = 64
CONF_KERNEL = 31
SCONV_KERNEL = 3
N_GROUPS = 4
EXPERTS_PER_GROUP = 8
N_EXPERTS = N_GROUPS * EXPERTS_PER_GROUP
TOP_K = 2
D_FF_EXPERT = D_MODEL // 2
NORM_EPS = 1e-6

SUBLANES = 8
LANES = 128
VMEM_LIMIT_BYTES = 56 * 1024 * 1024

SSM_CHUNKS = 2
SSM_CHUNK_IN = BRANCH_WIDTH // SSM_CHUNKS
SSM_CHUNK_STATE = SSM_GROUPS * SSM_STATE // SSM_CHUNKS
CONF_HIST = (CONF_KERNEL - 1) * BATCH
SCONV_HIST = (SCONV_KERNEL - 1) * BATCH
ROUTER_ROWS = 40
MOD_SHIFT1, MOD_SCALE1, MOD_GATE1, MOD_SHIFT2, MOD_SCALE2, MOD_GATE2 = range(6)

ADA_COLS = 1536
INPROJ_ROWS = 1024
SEQ_ROWS = 512
CONV_ROWS = 64
EW_ROWS = 32
MERGE_ROWS = 512
ROUTE_ROWS = MERGE_ROWS
DISPATCH_ROWS = 512
DISPATCH_SLOTS = 4
DISPATCH_LOOKAHEAD = 2
EXPERT_ROWS = 512
IN_LOOKAHEAD = 3
OUT_SLOTS = 3
ZERO_ROWS = 64
COMBINE_ROWS = 512


def _dot(a, b):
    return jnp.dot(a, b, preferred_element_type=F32)


def _split_bf16(v):
    hi = v.astype(BF16)
    lo = (v - hi.astype(F32)).astype(BF16)
    return hi, lo


def _sigmoid(v):
    return 1.0 / (1.0 + jnp.exp(-v))


def _gelu_tanh(v):
    return 0.5 * v * (1.0 + jnp.tanh(math.sqrt(2.0 / math.pi) * (v + 0.044715 * (v * v * v))))


def _per_batch(v, fn):
    rows, d = v.shape
    return fn(v.reshape(rows // BATCH, BATCH, d)).reshape(rows, d)


def _load_token_tiles(ref, first_token, n_tokens):
    base = first_token * SUBLANES
    chunks = [ref[pl.ds(base + s, n_tokens, stride=SUBLANES), :] for s in range(SUBLANES)]
    return jnp.concatenate(chunks, axis=1)


def _store_token_tiles(ref, value):
    for s in range(SUBLANES):
        ref[pl.ds(s, value.shape[0], stride=SUBLANES), :] = value[:, s * LANES:(s + 1) * LANES]


def _token_tile(ref, token):
    return ref.at[pl.ds(pl.multiple_of(token * SUBLANES, SUBLANES), SUBLANES)]


def _layer_spec(arr, l, col_block=None):
    tail = arr.shape[1:]
    if col_block is None:
        zeros = (0,) * len(tail)
        return pl.BlockSpec((None,) + tail, lambda *_: (l,) + zeros)
    return pl.BlockSpec((None, tail[0], D_MODEL), lambda *_: (l, 0, col_block))


def _norm_modulate(x, g, scale, shift):
    ms = jnp.mean(x * x, axis=-1, keepdims=True)
    h = x * lax.rsqrt(ms + NORM_EPS) * g
    return _per_batch(h, lambda h3: h3 * (1.0 + scale) + shift)


def _ada_kernel(c_ref, w_ref, b_ref, o_ref):
    c = c_ref[...]
    cond = c * _sigmoid(c)
    chi, clo = _split_bf16(cond)
    whi, wlo = _split_bf16(w_ref[...])
    o_ref[...] = _dot(chi, whi) + _dot(chi, wlo) + _dot(clo, whi) + b_ref[...]


def _ada(c, w_ada, b_ada):
    depth, d, n = w_ada.shape
    return pl.pallas_call(
        _ada_kernel,
        out_shape=jax.ShapeDtypeStruct((depth, BATCH, n), F32),
        grid=(depth, n // ADA_COLS),
        in_specs=[
            pl.BlockSpec((BATCH, d), lambda l, j: (0, 0)),
            pl.BlockSpec((None, d, ADA_COLS), lambda l, j: (l, 0, j)),
            pl.BlockSpec((None, 1, ADA_COLS), lambda l, j: (l, 0, j)),
        ],
        out_specs=pl.BlockSpec((None, BATCH, ADA_COLS), lambda l, j: (l, 0, j)),
        compiler_params=pltpu.CompilerParams(
            dimension_semantics=("arbitrary", "arbitrary"), vmem_limit_bytes=VMEM_LIMIT_BYTES),
        name="ada_modulation",
    )(c, w_ada, b_ada.reshape(depth, 1, n))


def _inproj_kernel(x_ref, g_ref, sc_ref, sh_ref, w_ref, xt_ref, o_ref):
    bsz, steps, d = x_ref.shape
    x = pltpu.einshape("bsd->sbd", x_ref[...]).reshape(steps * bsz, d)
    xt_ref[...] = x
    h = _norm_modulate(x, g_ref[...], sc_ref[...], sh_ref[...])
    o_ref[...] = _dot(h.astype(BF16), w_ref[...]).astype(o_ref.dtype)


def _inproj_first(x, g, mod, w_in):
    bsz, seq, d = x.shape
    n = w_in.shape[2]
    steps = INPROJ_ROWS // bsz
    return pl.pallas_call(
        _inproj_kernel,
        out_shape=(jax.ShapeDtypeStruct((bsz * seq, d), F32), jax.ShapeDtypeStruct((bsz * seq, n), BF16)),
        grid=(seq // steps,),
        in_specs=[
            pl.BlockSpec((bsz, steps, d), lambda i: (0, i, 0)),
            _layer_spec(g, 0),
            _layer_spec(mod, 0, MOD_SCALE1),
            _layer_spec(mod, 0, MOD_SHIFT1),
            _layer_spec(w_in, 0),
        ],
        out_specs=(pl.BlockSpec((INPROJ_ROWS, d), lambda i: (i, 0)),
                   pl.BlockSpec((INPROJ_ROWS, n), lambda i: (i, 0))),
        compiler_params=pltpu.CompilerParams(
            dimension_semantics=("arbitrary",), vmem_limit_bytes=VMEM_LIMIT_BYTES),
        name="mixer_inproj",
    )(x, g, mod, mod, w_in)


def _seq_kernel(p_ref, wb_ref, wc_ref, are_ref, aim_ref, d_ref, wglu_ref, cw_ref, cb_ref,
                lng_ref, lnb_ref, sw_ref, o_ref, bu_ref, st_ref, zbuf_ref, cbuf_ref, qbuf_ref):
    rows = p_ref.shape[0]
    steps = rows // BATCH
    w = BRANCH_WIDTH

    @pl.when(pl.program_id(0) == 0)
    def _():
        st_ref[...] = jnp.zeros_like(st_ref)
        zbuf_ref[0:CONF_HIST, :] = jnp.zeros((CONF_HIST, w), F32)
        qbuf_ref[0:SCONV_HIST, :] = jnp.zeros((SCONV_HIST, w), F32)

    chunk_groups = EW_ROWS // SUBLANES

    def pointwise_chunk(j):
        r0 = j * EW_ROWS
        rs = slice(r0, r0 + EW_ROWS)
        gate_c = p_ref[rs, 4 * w:5 * w].astype(F32)
        hv = p_ref[rs, 5 * w:6 * w].astype(F32)
        qbuf_ref[r0 + SCONV_HIST:r0 + SCONV_HIST + EW_ROWS, :] = gate_c * hv
        q3 = qbuf_ref[r0:r0 + EW_ROWS + SCONV_HIST, :].reshape(
            chunk_groups + SCONV_KERNEL - 1, SUBLANES, w)
        acc = q3[0:chunk_groups] * sw_ref[0:SUBLANES, :]
        for k in range(1, SCONV_KERNEL):
            acc = acc + q3[k:k + chunk_groups] * sw_ref[k * SUBLANES:(k + 1) * SUBLANES, :]
        gate_b = p_ref[rs, 3 * w:4 * w].astype(F32)
        o_ref[rs, 2 * w:3 * w] = (gate_b * acc.reshape(EW_ROWS, w)).astype(o_ref.dtype)
        v = p_ref[rs, w:2 * w].astype(F32)
        g = p_ref[rs, 2 * w:3 * w].astype(F32)
        zbuf_ref[r0 + CONF_HIST:r0 + CONF_HIST + EW_ROWS, :] = v * _sigmoid(g)

    def input_matmul(c):
        u_c = p_ref[:, c * SSM_CHUNK_IN:(c + 1) * SSM_CHUNK_IN]
        bu_ref[c] = _dot(u_c, wb_ref[c])

    def scan(c):
        a_re = are_ref[c]
        a_im = aim_ref[c]
        x_re = st_ref[c, 0]
        x_im = st_ref[c, 1]
        for t in range(steps):
            rs = slice(t * BATCH, (t + 1) * BATCH)
            n_re = a_re * x_re - a_im * x_im + bu_ref[c, rs, 0:SSM_CHUNK_STATE]
            n_im = a_re * x_im + a_im * x_re + bu_ref[c, rs, SSM_CHUNK_STATE:]
            bu_ref[c, rs, 0:SSM_CHUNK_STATE] = n_re
            bu_ref[c, rs, SSM_CHUNK_STATE:] = n_im
            x_re, x_im = n_re, n_im
        st_ref[c, 0] = x_re
        st_ref[c, 1] = x_im

    def output_matmul(c):
        return _dot(bu_ref[c].astype(BF16), wc_ref[c])

    input_matmul(0)
    for j in range(rows // EW_ROWS):
        pointwise_chunk(j)
    qbuf_ref[0:SCONV_HIST, :] = qbuf_ref[rows:rows + SCONV_HIST, :]
    input_matmul(1)
    scan(0)
    y0 = output_matmul(0)
    scan(1)
    ys = [y0, output_matmul(1)]
    u = p_ref[:, 0:w].astype(F32)
    y = _gelu_tanh(jnp.concatenate(ys, axis=1) + d_ref[...] * u)
    y_ssm = y * _sigmoid(_dot(y.astype(BF16), wglu_ref[...]))
    o_ref[:, 0:w] = y_ssm.astype(o_ref.dtype)

    win_groups = (CONV_ROWS + CONF_HIST) // SUBLANES
    out_groups = CONV_ROWS // SUBLANES

    def conv_chunk(j, carry):
        r0 = pl.multiple_of(j * CONV_ROWS, CONV_ROWS)
        for s in range(w // LANES):
            ls = slice(s * LANES, (s + 1) * LANES)
            win = zbuf_ref[pl.ds(r0, CONV_ROWS + CONF_HIST), ls].reshape(win_groups, SUBLANES, LANES)
            acc = win[0:out_groups] * cw_ref[0:SUBLANES, ls]
            for k in range(1, CONF_KERNEL):
                acc = acc + win[k:k + out_groups] * cw_ref[k * SUBLANES:(k + 1) * SUBLANES, ls]
            cbuf_ref[pl.ds(r0, CONV_ROWS), ls] = acc.reshape(CONV_ROWS, LANES)
        return carry

    lax.fori_loop(0, rows // CONV_ROWS, conv_chunk, 0)
    zbuf_ref[0:CONF_HIST, :] = zbuf_ref[rows:rows + CONF_HIST, :]
    cv = cbuf_ref[...] + cb_ref[...]
    cc = cv - jnp.mean(cv, axis=-1, keepdims=True)
    ln = cc * lax.rsqrt(jnp.mean(cc * cc, axis=-1, keepdims=True) + NORM_EPS) * lng_ref[...] + lnb_ref[...]
    o_ref[:, w:2 * w] = (ln * _sigmoid(ln)).astype(o_ref.dtype)


def _seq_mixers(l, proj, *layer_params):
    t = proj.shape[0]
    w = BRANCH_WIDTH
    return pl.pallas_call(
        _seq_kernel,
        out_shape=jax.ShapeDtypeStruct((t, N_BRANCH * w), BF16),
        grid=(t // SEQ_ROWS,),
        in_specs=[pl.BlockSpec((SEQ_ROWS, N_IN_CHUNKS * w), lambda i: (i, 0))]
                 + [_layer_spec(p, l) for p in layer_params],
        out_specs=pl.BlockSpec((SEQ_ROWS, N_BRANCH * w), lambda i: (i, 0)),
        scratch_shapes=[
            pltpu.VMEM((SSM_CHUNKS, SEQ_ROWS, 2 * SSM_CHUNK_STATE), F32),
            pltpu.VMEM((SSM_CHUNKS, 2, BATCH, SSM_CHUNK_STATE), F32),
            pltpu.VMEM((SEQ_ROWS + CONF_HIST, w), F32),
            pltpu.VMEM((SEQ_ROWS, w), F32),
            pltpu.VMEM((SEQ_ROWS + SCONV_HIST, w), F32),
        ],
        compiler_params=pltpu.CompilerParams(
            dimension_semantics=("arbitrary",), vmem_limit_bytes=VMEM_LIMIT_BYTES),
        name="seq_mixers",
    )(proj, *layer_params)


def _merge_router_kernel(x_ref, g_ref, sc_ref, sh_ref, g1_ref, y_ref, wg_ref, bg_ref, wbr_ref, wo_ref,
                         g2_ref, sc2_ref, sh2_ref, whi_ref, wlo_ref, b_ref, tri_ref,
                         o_ref, h_ref, route_ref, route_t_ref, cnt_ref, run_ref):
    x = x_ref[...]
    d = x.shape[1]
    w = BRANCH_WIDTH
    hb = _norm_modulate(x, g_ref[...], sc_ref[...], sh_ref[...]).astype(BF16)
    merged = None
    for n in range(N_BRANCH):
        gate = _sigmoid(_dot(hb, wg_ref[:, n * d:(n + 1) * d]) + bg_ref[:, n * d:(n + 1) * d])
        term = gate * _dot(y_ref[:, n * w:(n + 1) * w], wbr_ref[n])
        merged = term if merged is None else merged + term
    out = _dot(merged.astype(BF16), wo_ref[...])
    x_new = x + _per_batch(out, lambda o3: o3 * g1_ref[...])
    o_ref[...] = x_new
    _route_tile(x_new, g2_ref, sc2_ref, sh2_ref, whi_ref, wlo_ref, b_ref, tri_ref,
                h_ref, route_ref, route_t_ref, cnt_ref, run_ref)


def _merge_router(l, x, mix_g, ffn_g, mod, ycat, w_gate, b_gate, w_branch, w_out, r_hi, r_lo, r_bias, tri):
    t, d = x.shape
    w = BRANCH_WIDTH
    row_tile = lambda cols: pl.BlockSpec((MERGE_ROWS, cols), lambda i: (i, 0))
    return pl.pallas_call(
        _merge_router_kernel,
        out_shape=(
            jax.ShapeDtypeStruct((t, d), F32),
            jax.ShapeDtypeStruct((t * SUBLANES, LANES), F32),
            jax.ShapeDtypeStruct((t // MERGE_ROWS, SUBLANES, MERGE_ROWS), F32),
            jax.ShapeDtypeStruct((t, SUBLANES), F32),
            jax.ShapeDtypeStruct((N_EXPERTS, MERGE_ROWS), F32),
        ),
        grid=(t // MERGE_ROWS,),
        in_specs=[
            row_tile(d),
            _layer_spec(mix_g, l),
            _layer_spec(mod, l, MOD_SCALE1),
            _layer_spec(mod, l, MOD_SHIFT1),
            _layer_spec(mod, l, MOD_GATE1),
            row_tile(N_BRANCH * w),
            _layer_spec(w_gate, l),
            _layer_spec(b_gate, l),
            _layer_spec(w_branch, l),
            _layer_spec(w_out, l),
            _layer_spec(ffn_g, l),
            _layer_spec(mod, l, MOD_SCALE2),
            _layer_spec(mod, l, MOD_SHIFT2),
            _layer_spec(r_hi, l),
            _layer_spec(r_lo, l),
            _layer_spec(r_bias, l),
            pl.BlockSpec((MERGE_ROWS, MERGE_ROWS), lambda i: (0, 0)),
        ],
        out_specs=(
            row_tile(d),
            pl.BlockSpec((MERGE_ROWS * SUBLANES, LANES), lambda i: (i, 0)),
            pl.BlockSpec((None, SUBLANES, MERGE_ROWS), lambda i: (i, 0, 0)),
            row_tile(SUBLANES),
            pl.BlockSpec((N_EXPERTS, MERGE_ROWS), lambda i: (0, 0)),
        ),
        scratch_shapes=[pltpu.VMEM((N_EXPERTS, MERGE_ROWS), F32)],
        compiler_params=pltpu.CompilerParams(
            dimension_semantics=("arbitrary",), vmem_limit_bytes=VMEM_LIMIT_BYTES),
        name="mixer_merge_router",
    )(x, mix_g, mod, mod, mod, ycat, w_gate, b_gate, w_branch, w_out, ffn_g, mod, mod, r_hi, r_lo, r_bias, tri)


def _route_tile(x, g_ref, sc_ref, sh_ref, whi_ref, wlo_ref, b_ref, tri_ref,
                h_ref, route_ref, route_t_ref, cnt_ref, run_ref):
    @pl.when(pl.program_id(0) == 0)
    def _():
        run_ref[...] = jnp.zeros_like(run_ref)

    h = _norm_modulate(x, g_ref[...], sc_ref[...], sh_ref[...])
    _store_token_tiles(h_ref, h)
    rows = h.shape[0]
    hhi, hlo = _split_bf16(h)
    logits = (_dot(hhi, whi_ref[...]) + _dot(hhi, wlo_ref[...]) + _dot(hlo, whi_ref[...]) + b_ref[...]).T
    lg = logits[0:N_GROUPS]
    le = logits[SUBLANES:SUBLANES + N_EXPERTS]

    g_row = lax.broadcasted_iota(jnp.int32, (N_GROUPS, rows), 0).astype(F32)
    g_max = jnp.max(lg, axis=0, keepdims=True)
    g_sel = jnp.min(jnp.where(lg == g_max, g_row, float(N_GROUPS)), axis=0, keepdims=True)
    g_w = 1.0 / jnp.sum(jnp.exp(lg - g_max), axis=0, keepdims=True)

    e_row = lax.broadcasted_iota(jnp.int32, (N_EXPERTS, rows), 0).astype(F32)
    e_grp = jnp.floor(e_row * (1.0 / EXPERTS_PER_GROUP))
    neg = float("-inf")
    lm = jnp.where(e_grp == g_sel, le, neg)
    v1 = jnp.max(lm, axis=0, keepdims=True)
    i1 = jnp.min(jnp.where(lm == v1, e_row, float(N_EXPERTS)), axis=0, keepdims=True)
    lm2 = jnp.where(e_row == i1, neg, lm)
    v2 = jnp.max(lm2, axis=0, keepdims=True)
    i2 = jnp.min(jnp.where(lm2 == v2, e_row, float(N_EXPERTS)), axis=0, keepdims=True)
    e21 = jnp.exp(v2 - v1)
    p1 = 1.0 / (1.0 + e21)
    w1 = p1 * g_w
    w2 = e21 * p1 * g_w

    hit1 = e_row == i1
    hit2 = e_row == i2
    onehot = jnp.where(hit1 | hit2, 1.0, 0.0)
    cum = _dot(onehot.astype(BF16), tri_ref[...])
    before = cum - onehot + run_ref[...]
    r1 = jnp.sum(jnp.where(hit1, before, 0.0), axis=0, keepdims=True)
    r2 = jnp.sum(jnp.where(hit2, before, 0.0), axis=0, keepdims=True)
    run_ref[...] = run_ref[...] + jnp.broadcast_to(cum[:, rows - 1:rows], run_ref.shape)
    cnt_ref[...] = run_ref[...]
    zero = jnp.zeros_like(w1)
    route = jnp.concatenate([i1, i2, r1, r2, w1, w2, zero, zero], axis=0)
    route_ref[...] = route
    route_t_ref[...] = route.T


def _row_copy(src_hbm, src_token, dst_hbm, dst_token, sem):
    return pltpu.make_async_copy(_token_tile(src_hbm, src_token), _token_tile(dst_hbm, dst_token), sem)


def _dispatch_kernel(zstart_ref, dest_hbm, h_hbm, xs_hbm, idx0_smem, idx1_smem, idx2_smem, idx3_smem,
                     hbuf_ref, zeros_ref, idx_sem, in_sem, zero_sem, row_sem):
    i = pl.program_id(0)
    n_tiles = pl.num_programs(0)
    idx_smem = (idx0_smem, idx1_smem, idx2_smem, idx3_smem)
    n_idx = idx0_smem.shape[0]
    tile = n_idx // TOP_K
    tile_rows = tile * SUBLANES

    def idx_copy(tile_index, s):
        src = dest_hbm.at[pl.ds(pl.multiple_of(tile_index * n_idx, n_idx), n_idx)]
        return pltpu.make_async_copy(src, idx_smem[s], idx_sem.at[s])

    def in_copy(tile_index, s):
        src = h_hbm.at[pl.ds(pl.multiple_of(tile_index * tile_rows, tile_rows), tile_rows)]
        return pltpu.make_async_copy(src, hbuf_ref.at[s], in_sem.at[s])

    def wait_rows(s):
        for k in range(TOP_K):
            pltpu.make_async_copy(hbuf_ref.at[s], xs_hbm.at[pl.ds(0, tile_rows)], row_sem.at[s % 2]).wait()

    def zero_copy(j):
        piece_rows = ZERO_ROWS * SUBLANES
        start = pl.multiple_of(zstart_ref[j] * SUBLANES, piece_rows)
        return pltpu.make_async_copy(zeros_ref, xs_hbm.at[pl.ds(start, piece_rows)], zero_sem)

    @pl.when(i == 0)
    def _():
        for b in range(DISPATCH_LOOKAHEAD):
            @pl.when(b < n_tiles)
            def _():
                idx_copy(b, b).start()
                in_copy(b, b).start()

        zeros_ref[...] = jnp.zeros_like(zeros_ref)

        def zero_start(j, carry):
            @pl.when(zstart_ref[j] >= 0)
            def _():
                zero_copy(j).start()
            return carry

        def zero_wait(j, carry):
            @pl.when(zstart_ref[j] >= 0)
            def _():
                zero_copy(j).wait()
            return carry

        lax.fori_loop(0, zstart_ref.shape[0], zero_start, 0)
        lax.fori_loop(0, zstart_ref.shape[0], zero_wait, 0)

    def tile_step(s):
        ahead = (s + DISPATCH_LOOKAHEAD) % DISPATCH_SLOTS

        @pl.when(i + DISPATCH_LOOKAHEAD < n_tiles)
        def _():
            idx_copy(i + DISPATCH_LOOKAHEAD, ahead).start()
            in_copy(i + DISPATCH_LOOKAHEAD, ahead).start()

        idx_copy(i, s).wait()
        in_copy(i, s).wait()

        def start_rows(t, c):
            for k in range(TOP_K):
                _row_copy(hbuf_ref.at[s], t, xs_hbm, idx_smem[s][k * tile + t],
                          row_sem.at[s % 2]).start(priority=k)
            return c

        lax.fori_loop(0, tile, start_rows, 0, unroll=8)

        @pl.when(i >= 1)
        def _():
            wait_rows((s - 1) % DISPATCH_SLOTS)

        @pl.when(i == n_tiles - 1)
        def _():
            wait_rows(s)

    for s in range(DISPATCH_SLOTS):
        pl.when(lax.rem(i, DISPATCH_SLOTS) == s)(functools.partial(tile_step, s))


def _dispatch(zstart, dest_tiles, h, cap):
    t = h.shape[0] // SUBLANES
    any_spec = pl.BlockSpec(memory_space=pl.ANY)
    idx_buf = pltpu.SMEM((TOP_K * DISPATCH_ROWS,), jnp.int32)
    return pl.pallas_call(
        _dispatch_kernel,
        out_shape=jax.ShapeDtypeStruct((cap * SUBLANES, LANES), h.dtype),
        grid_spec=pltpu.PrefetchScalarGridSpec(
            num_scalar_prefetch=1,
            grid=(t // DISPATCH_ROWS,),
            in_specs=[any_spec, any_spec],
            out_specs=any_spec,
            scratch_shapes=[
                idx_buf, idx_buf, idx_buf, idx_buf,
                pltpu.VMEM((DISPATCH_SLOTS, DISPATCH_ROWS * SUBLANES, LANES), h.dtype),
                pltpu.VMEM((ZERO_ROWS * SUBLANES, LANES), h.dtype),
                pltpu.SemaphoreType.DMA((DISPATCH_SLOTS,)),
                pltpu.SemaphoreType.DMA((DISPATCH_SLOTS,)),
                pltpu.SemaphoreType.DMA,
                pltpu.SemaphoreType.DMA((2,)),
            ],
        ),
        compiler_params=pltpu.CompilerParams(
            dimension_semantics=("arbitrary",), has_side_effects=True),
        name="moe_dispatch",
    )(zstart, dest_tiles, h)


def _expert_kernel(first_ref, nblk_ref, total_ref, xs_hbm, w1_ref, w3_ref, w2_ref, ys_hbm,
                   w1b_ref, w3b_ref, w2b_ref, xbuf_ref, ybuf_ref, in_sem, out_sem):
    e = pl.program_id(0)
    n_blocks = nblk_ref[e]
    first = first_ref[e]
    total = total_ref[0]
    block_rows = EXPERT_ROWS * SUBLANES

    def rows_of(block):
        return pl.ds(pl.multiple_of(block * block_rows, block_rows), block_rows)

    def in_copy(block, slot):
        return pltpu.make_async_copy(xs_hbm.at[rows_of(block)], xbuf_ref.at[slot], in_sem.at[slot])

    def out_copy(block, slot):
        return pltpu.make_async_copy(ybuf_ref.at[slot], ys_hbm.at[rows_of(block)], out_sem.at[slot])

    @pl.when(n_blocks > 0)
    def _():
        @pl.when(first == 0)
        def _():
            for b in range(IN_LOOKAHEAD):
                @pl.when(b < total)
                def _():
                    in_copy(b, b).start(priority=1)

        w1b_ref[...] = w1_ref[...].astype(BF16)
        w3b_ref[...] = w3_ref[...].astype(BF16)
        w2b_ref[...] = w2_ref[...].astype(BF16)

        def block_step(j, carry):
            g = first + j
            in_slot = lax.rem(g, IN_LOOKAHEAD + 1)
            slot = lax.rem(g, OUT_SLOTS)

            @pl.when(g + IN_LOOKAHEAD < total)
            def _():
                in_copy(g + IN_LOOKAHEAD, lax.rem(g + IN_LOOKAHEAD, IN_LOOKAHEAD + 1)).start(priority=1)

            in_copy(g, in_slot).wait()

            @pl.when(g >= OUT_SLOTS)
            def _():
                out_copy(g - OUT_SLOTS, slot).wait()

            xb = _load_token_tiles(xbuf_ref.at[in_slot], 0, EXPERT_ROWS).astype(BF16)
            a = _dot(xb, w1b_ref[...])
            hid = a * _sigmoid(a) * _dot(xb, w3b_ref[...])
            _store_token_tiles(ybuf_ref.at[slot], _dot(hid.astype(BF16), w2b_ref[...]))
            out_copy(g, slot).start()
            return carry

        lax.fori_loop(0, n_blocks, block_step, 0)

        @pl.when(first + n_blocks == total)
        def _():
            for back in range(OUT_SLOTS, 0, -1):
                @pl.when(total >= back)
                def _():
                    out_copy(total - back, lax.rem(total - back, OUT_SLOTS)).wait()


def _experts(l, first_block, n_blocks, total_blocks, xs, w1, w3, w2):
    _, n_experts, d, f = w1.shape
    tile = (EXPERT_ROWS * SUBLANES, LANES)
    any_spec = pl.BlockSpec(memory_space=pl.ANY)
    weight_spec = lambda rows, cols: pl.BlockSpec((None, None, rows, cols), lambda e, *_: (l, e, 0, 0))
    return pl.pallas_call(
        _expert_kernel,
        out_shape=jax.ShapeDtypeStruct(xs.shape, F32),
        grid_spec=pltpu.PrefetchScalarGridSpec(
            num_scalar_prefetch=3,
            grid=(n_experts,),
            in_specs=[any_spec, weight_spec(d, f), weight_spec(d, f), weight_spec(f, d)],
            out_specs=any_spec,
            scratch_shapes=[
                pltpu.VMEM((d, f), BF16), pltpu.VMEM((d, f), BF16), pltpu.VMEM((f, d), BF16),
                pltpu.VMEM((IN_LOOKAHEAD + 1,) + tile, F32), pltpu.VMEM((OUT_SLOTS,) + tile, F32),
                pltpu.SemaphoreType.DMA((IN_LOOKAHEAD + 1,)), pltpu.SemaphoreType.DMA((OUT_SLOTS,)),
            ],
        ),
        compiler_params=pltpu.CompilerParams(
            dimension_semantics=("arbitrary",), vmem_limit_bytes=VMEM_LIMIT_BYTES),
        name="moe_experts",
    )(first_block, n_blocks, total_blocks, xs, w1, w3, w2)


def _gather_expert_rows(dest_hbm, ys_hbm, idx_smem, ybuf_ref, idx_sem, row_sem):
    i = pl.program_id(0)
    n_tiles = pl.num_programs(0)
    n_idx = idx_smem[0].shape[0]
    tile = n_idx // TOP_K

    def idx_copy(tile_index, s):
        src = dest_hbm.at[pl.ds(pl.multiple_of(tile_index * n_idx, n_idx), n_idx)]
        return pltpu.make_async_copy(src, idx_smem[s], idx_sem.at[s])

    def start_rows(s):
        def body(t, carry):
            for k in range(TOP_K):
                row = k * tile + t
                pltpu.make_async_copy(_token_tile(ys_hbm, idx_smem[s][row]),
                                      _token_tile(ybuf_ref.at[s], row), row_sem.at[s]).start(priority=k)
            return carry
        lax.fori_loop(0, tile, body, 0, unroll=8)

    def wait_rows(s):
        pltpu.make_async_copy(ybuf_ref.at[s], ybuf_ref.at[s], row_sem.at[s]).wait()

    @pl.when(i == 0)
    def _():
        idx_copy(0, 0).start()
        idx_copy(0, 0).wait()
        start_rows(0)

        @pl.when(n_tiles > 1)
        def _():
            idx_copy(1, 1).start()

    def step(slot):
        other = 1 - slot

        @pl.when(i + 1 < n_tiles)
        def _():
            idx_copy(i + 1, other).wait()
            start_rows(other)

        @pl.when(i + 2 < n_tiles)
        def _():
            idx_copy(i + 2, slot).start()

        wait_rows(slot)

    for slot in range(2):
        pl.when(lax.rem(i, 2) == slot)(functools.partial(step, slot))
    return lax.rem(i, 2)


def _combined_residual(slot, ybuf_ref, x_ref, g2_ref, wt_ref):
    tile = x_ref.shape[0]
    wt = wt_ref[...]
    ytiles = ybuf_ref.at[slot]
    y = (wt[:, 2 * TOP_K:2 * TOP_K + 1] * _load_token_tiles(ytiles, 0, tile)
         + wt[:, 2 * TOP_K + 1:2 * TOP_K + 2] * _load_token_tiles(ytiles, tile, tile))
    return x_ref[...] + _per_batch(y, lambda y3: y3 * g2_ref[...])


def _combine_final_kernel(dest_hbm, ys_hbm, x_ref, g2_ref, wt_ref, fg_ref, o_ref,
                          idx0_smem, idx1_smem, ybuf_ref, idx_sem, row_sem):
    slot = _gather_expert_rows(dest_hbm, ys_hbm, (idx0_smem, idx1_smem), ybuf_ref, idx_sem, row_sem)
    out = _combined_residual(slot, ybuf_ref, x_ref, g2_ref, wt_ref)
    ms = jnp.mean(out * out, axis=-1, keepdims=True)
    out = out * lax.rsqrt(ms + NORM_EPS) * fg_ref[...]
    bsz, steps, d = o_ref.shape
    o_ref[...] = pltpu.einshape("sbd->bsd", out.reshape(steps, bsz, d))


def _combine_inproj_kernel(dest_hbm, ys_hbm, x_ref, g2_ref, wt_ref, g_ref, sc_ref, sh_ref, w_ref,
                           xo_ref, p_ref, idx0_smem, idx1_smem, ybuf_ref, idx_sem, row_sem):
    slot = _gather_expert_rows(dest_hbm, ys_hbm, (idx0_smem, idx1_smem), ybuf_ref, idx_sem, row_sem)
    out = _combined_residual(slot, ybuf_ref, x_ref, g2_ref, wt_ref)
    xo_ref[...] = out
    h = _norm_modulate(out, g_ref[...], sc_ref[...], sh_ref[...])
    p_ref[...] = _dot(h.astype(BF16), w_ref[...]).astype(p_ref.dtype)


def _combine(l, dest_tiles, ys, x, mod, wt, final_g=None, next_layer=None):
    t, d = x.shape
    any_spec = pl.BlockSpec(memory_space=pl.ANY)
    row_tile = lambda cols: pl.BlockSpec((COMBINE_ROWS, cols), lambda i: (i, 0))
    in_specs = [any_spec, any_spec, row_tile(d), _layer_spec(mod, l, MOD_GATE2), row_tile(SUBLANES)]
    if next_layer is None:
        body, operands = _combine_final_kernel, (final_g,)
        in_specs += [pl.BlockSpec((1, d), lambda i: (0, 0))]
        out_shape = jax.ShapeDtypeStruct((BATCH, t // BATCH, d), F32)
        out_specs = pl.BlockSpec((BATCH, COMBINE_ROWS // BATCH, d), lambda i: (0, i, 0))
    else:
        mix_g, w_in = next_layer
        n = w_in.shape[2]
        body, operands = _combine_inproj_kernel, (mix_g, mod, mod, w_in)
        in_specs += [_layer_spec(mix_g, l + 1), _layer_spec(mod, l + 1, MOD_SCALE1),
                     _layer_spec(mod, l + 1, MOD_SHIFT1), _layer_spec(w_in, l + 1)]
        out_shape = (jax.ShapeDtypeStruct((t, d), F32), jax.ShapeDtypeStruct((t, n), BF16))
        out_specs = (row_tile(d), row_tile(n))
    return pl.pallas_call(
        body,
        out_shape=out_shape,
        grid=(t // COMBINE_ROWS,),
        in_specs=in_specs,
        out_specs=out_specs,
        scratch_shapes=[
            pltpu.SMEM((TOP_K * COMBINE_ROWS,), jnp.int32),
            pltpu.SMEM((TOP_K * COMBINE_ROWS,), jnp.int32),
            pltpu.VMEM((2, TOP_K * COMBINE_ROWS * SUBLANES, LANES), F32),
            pltpu.SemaphoreType.DMA((2,)),
            pltpu.SemaphoreType.DMA((2,)),
        ],
        compiler_params=pltpu.CompilerParams(
            dimension_semantics=("arbitrary",), vmem_limit_bytes=VMEM_LIMIT_BYTES),
        name="moe_combine",
    )(dest_tiles, ys, x, mod, wt, *operands)


def _ssm_params(lam_re, lam_im, log_step, b_re, b_im, c_re, c_im):
    step = jnp.exp(log_step)[:, None]
    mag = jnp.exp(lam_re * step)
    a_re, a_im = mag * jnp.cos(lam_im * step), mag * jnp.sin(lam_im * step)
    den = lam_re * lam_re + lam_im * lam_im
    n_re, n_im = a_re - 1.0, a_im
    k_re = (n_re * lam_re + n_im * lam_im) / den
    k_im = (n_im * lam_re - n_re * lam_im) / den
    bb_re = k_re[..., None] * b_re - k_im[..., None] * b_im
    bb_im = k_re[..., None] * b_im + k_im[..., None] * b_re
    gpc = SSM_GROUPS // SSM_CHUNKS
    eye = jnp.eye(gpc, dtype=F32)

    def in_blocks(bb):
        bb = bb.reshape(SSM_CHUNKS, gpc, SSM_STATE, SSM_GROUP)
        return jnp.einsum("cgph,gk->cghkp", bb, eye).reshape(SSM_CHUNKS, SSM_CHUNK_IN, SSM_CHUNK_STATE)

    def out_blocks(cc):
        cc = cc.reshape(SSM_CHUNKS, gpc, SSM_GROUP, SSM_STATE)
        return jnp.einsum("cghp,gk->cgpkh", cc, eye).reshape(SSM_CHUNKS, SSM_CHUNK_STATE, SSM_CHUNK_IN)

    wb = jnp.concatenate([in_blocks(bb_re), in_blocks(bb_im)], axis=2).astype(BF16)
    wc = jnp.concatenate([out_blocks(c_re), out_blocks(-c_im)], axis=1).astype(BF16)
    bcast = lambda a: jnp.broadcast_to(a.reshape(SSM_CHUNKS, 1, SSM_CHUNK_STATE),
                                       (SSM_CHUNKS, BATCH, SSM_CHUNK_STATE))
    return wb, wc, bcast(a_re), bcast(a_im)


def _rows8(w):
    depth, k, c = w.shape
    return jnp.broadcast_to(w[:, :, None, :], (depth, k, SUBLANES, c)).reshape(depth, k * SUBLANES, c)


def _router_weights(w_rg, b_rg, w_re, b_re):
    depth, d, _ = w_rg.shape
    pad = SUBLANES - N_GROUPS
    tail = LANES - ROUTER_ROWS
    w_cols = jnp.concatenate([w_rg, jnp.zeros((depth, d, pad), F32), w_re,
                              jnp.zeros((depth, d, tail), F32)], axis=2)
    b = jnp.concatenate([b_rg, jnp.zeros((depth, pad), F32), b_re, jnp.zeros((depth, tail), F32)], axis=1)
    hi, lo = _split_bf16(w_cols)
    return hi, lo, b.reshape(depth, 1, LANES)


def _routing_tables(route, counts):
    assert DISPATCH_ROWS == MERGE_ROWS and COMBINE_ROWS == MERGE_ROWS
    t = route.shape[0] * route.shape[2]
    counts = counts.astype(jnp.int32)
    padded = (counts + EXPERT_ROWS - 1) // EXPERT_ROWS * EXPERT_ROWS
    pad_end = jnp.cumsum(padded)
    pad_start = pad_end - padded
    experts = route[:, 0:TOP_K, :].astype(jnp.int32)
    ranks = route[:, TOP_K:2 * TOP_K, :].astype(jnp.int32)
    expert_ids = jnp.arange(N_EXPERTS, dtype=jnp.int32)
    is_expert = experts[..., None] == expert_ids
    dest_tiles = (jnp.sum(jnp.where(is_expert, pad_start, 0), axis=-1) + ranks).reshape(-1)

    capacity = TOP_K * t + N_EXPERTS * EXPERT_ROWS
    first_block = pad_start // EXPERT_ROWS
    n_blocks = padded // EXPERT_ROWS
    last_start = pad_end - EXPERT_ROWS
    last_valid = counts - (padded - EXPERT_ROWS)
    piece = jnp.arange(EXPERT_ROWS // ZERO_ROWS, dtype=jnp.int32)
    has_pad = (counts[:, None] > 0) & ((piece[None, :] + 1) * ZERO_ROWS > last_valid[:, None])
    zstart = jnp.where(has_pad, last_start[:, None] + piece[None, :] * ZERO_ROWS, -1).reshape(-1)
    total_blocks = (pad_end[-1] // EXPERT_ROWS).reshape(1)
    return (dest_tiles, dest_tiles, first_block.astype(jnp.int32), n_blocks.astype(jnp.int32),
            total_blocks.astype(jnp.int32), zstart.astype(jnp.int32), capacity)


def kernel(x, c, norm_mix_g, norm_ffn_g, w_ada, b_ada, w_in, lam_re, lam_im, log_step, ssm_b_re, ssm_b_im, ssm_c_re, ssm_c_im, ssm_d, w_glu, conf_dw_w, conf_dw_b, conf_ln_g, conf_ln_b, sconv_w, w_branch, w_gate, b_gate, w_out, w_router_group, b_router_group, w_router_expert, b_router_expert, w_exp_gate, w_exp_up, w_exp_down, final_norm_g):
    bsz, seq, d = x.shape
    assert bsz == BATCH and d == D_MODEL
    depth = w_in.shape[0]
    t = bsz * seq
    rows = lambda v: v.reshape(depth, 1, -1)

    mix_g, ffn_g = rows(norm_mix_g), rows(norm_ffn_g)
    seq_params = (*jax.vmap(_ssm_params)(lam_re, lam_im, log_step, ssm_b_re, ssm_b_im, ssm_c_re, ssm_c_im),
                  rows(ssm_d), w_glu.astype(BF16), _rows8(conf_dw_w), rows(conf_dw_b), rows(conf_ln_g),
                  rows(conf_ln_b), _rows8(sconv_w))
    w_in_b, w_gate_b, w_branch_b, w_out_b = (w.astype(BF16) for w in (w_in, w_gate, w_branch, w_out))
    r_hi, r_lo, r_b = _router_weights(w_router_group, b_router_group, w_router_expert, b_router_expert)
    tri = jnp.triu(jnp.ones((ROUTE_ROWS, ROUTE_ROWS), BF16))

    mod = _ada(c, w_ada, b_ada)
    xt, proj = _inproj_first(x, mix_g, mod, w_in_b)
    for l in range(depth):
        ycat = _seq_mixers(l, proj, *seq_params)
        xt, h2, route, route_t, counts = _merge_router(l, xt, mix_g, ffn_g, mod, ycat, w_gate_b, rows(b_gate),
                                                       w_branch_b, w_out_b, r_hi, r_lo, r_b, tri)
        (dest_d, dest_c, first_block, n_blocks, total_blocks, zstart,
         cap) = _routing_tables(route, counts[:, 0])
        xs = _dispatch(zstart, dest_d, h2, cap)
        ys = _experts(l, first_block, n_blocks, total_blocks, xs, w_exp_gate, w_exp_up, w_exp_down)
        if l + 1 < depth:
            xt, proj = _combine(l, dest_c, ys, xt, mod, route_t, next_layer=(mix_g, w_in_b))
        else:
            return _combine(l, dest_c, ys, xt, mod, route_t, final_g=final_norm_g.reshape(1, d))
```
